```python
import jax, jax.numpy as jnp
from jax import lax
import numpy as np

D_MODEL = 1024
BATCH = 4
SEQ = 4096
DEPTH = 1
DEC_BATCH = 32
DEC_SEQ = 1
PAST_LEN = 8192
PAGE_SIZE = 128

HEAD_DIM = 64
N_ATTN_HEADS = 12
D_ATTN = N_ATTN_HEADS * HEAD_DIM
D_CONV = D_MODEL - D_ATTN
CONV_WIDTH = 3
DILATED_BRANCHES = ((128, 1), (512, 4), (2048, 16))
MAX_WINDOW = 2048
ATT_BLOCK = 128
ATTN_SCALE = HEAD_DIM ** -0.5
ROPE_THETA = 10000.0
EPS = 1e-6
N_GROUPS = 4
EXPERTS_PER_GROUP = 8
N_EXPERTS = N_GROUPS * EXPERTS_PER_GROUP
TOP_K_IN_GROUP = 2
D_EXPERT = 256
D_IN_PROJ = 3 * D_ATTN + 3 * D_CONV
SPLITS = (D_ATTN, 2 * D_ATTN, 3 * D_ATTN, 3 * D_ATTN + D_CONV, 3 * D_ATTN + 2 * D_CONV)

kernel_name = "hymba_dilated_window_shortconv_hier_moe_step"


def rms_norm(x, g):
    xf = x.astype(jnp.float32)
    y = xf * lax.rsqrt(jnp.mean(xf * xf, axis=-1, keepdims=True) + EPS)
    return (y * g.astype(jnp.float32)).astype(x.dtype)


def rope(x, pos):
    half = HEAD_DIM // 2
    inv = ROPE_THETA ** (-jnp.arange(half, dtype=jnp.float32) / half)
    ang = pos.astype(jnp.float32)[:, None] * inv[None, :]
    cos = jnp.cos(ang)[:, None, :]
    sin = jnp.sin(ang)[:, None, :]
    xf = x.astype(jnp.float32)
    x1, x2 = xf[..., :half], xf[..., half:]
    return jnp.concatenate([x1 * cos - x2 * sin, x2 * cos + x1 * sin], axis=-1).astype(x.dtype)


def mix_input(x, pos, g_norm_mix, w_in, g_q, g_k):
    B_, T_ = x.shape[:2]
    xn = rms_norm(x, g_norm_mix)
    z = xn @ w_in
    q, k, v, b_gate, c_gate, u_in = jnp.split(z, SPLITS, axis=-1)
    heads = lambda t: t.reshape(B_, T_, N_ATTN_HEADS, HEAD_DIM)
    q = rope(rms_norm(heads(q), g_q), pos)
    k = rope(rms_norm(heads(k), g_k), pos)
    return q, k, heads(v), b_gate, c_gate * u_in


def dilated_branch_prompt(q, k, v, dil, sub_window):
    B_, S_, H_, Dh_ = q.shape
    L = S_ // dil
    nb = -(-L // ATT_BLOCK)
    Lp = nb * ATT_BLOCK

    def sub_blocks(t):
        t = t.reshape(B_, L, dil, H_, Dh_).transpose(0, 2, 1, 3, 4)
        t = jnp.pad(t, ((0, 0), (0, 0), (0, Lp - L), (0, 0), (0, 0)))
        return t.reshape(B_, dil, nb, ATT_BLOCK, H_, Dh_)

    qb, kb, vb = sub_blocks(q), sub_blocks(k), sub_blocks(v)
    prev = lambda t: jnp.pad(t, ((0, 0), (0, 0), (1, 0), (0, 0), (0, 0), (0, 0)))[:, :, :-1]
    kk = jnp.concatenate([prev(kb), kb], axis=3)
    vv = jnp.concatenate([prev(vb), vb], axis=3)
    s = jnp.einsum('brnqhd,brnkhd->brnhqk', qb, kk) * ATTN_SCALE
    qi = jnp.arange(ATT_BLOCK)[:, None] + ATT_BLOCK
    kj = jnp.arange(2 * ATT_BLOCK)[None, :]
    dist = qi - kj
    band = (dist >= 0) & (dist <= sub_window)
    has_prev = (jnp.arange(nb)[:, None, None] > 0) | (kj >= ATT_BLOCK)[None]
    valid = band[None] & has_prev
    s = jnp.where(valid[:, None], s, -jnp.inf)
    m = jnp.max(s, axis=-1)
    p = jnp.exp(s - m[..., None])
    den = jnp.sum(p, axis=-1)
    num = jnp.einsum('brnhqk,brnkhd->brnqhd', p, vv)

    num = num.reshape(B_, dil, Lp, H_, Dh_)[:, :, :L].transpose(0, 2, 1, 3, 4).reshape(B_, S_, H_, Dh_)
    unblock = lambda t: t.transpose(0, 1, 2, 4, 3).reshape(B_, dil, Lp, H_)[:, :, :L].transpose(0, 2, 1, 3).reshape(B_, S_, H_)
    return num, unblock(den), unblock(m)


def dilated_branch_sample(q, k_all, v_all, w_buf, dil, sub_window):
    T_ = q.shape[1]
    n_keys = sub_window + 1
    idx = w_buf + jnp.arange(T_)[:, None] - dil * jnp.arange(n_keys)[None, :]
    valid = idx >= 0
    idx_c = jnp.maximum(idx, 0)
    kg = k_all[:, idx_c]
    vg = v_all[:, idx_c]
    s = jnp.einsum('bthd,btnhd->bthn', q, kg) * ATTN_SCALE
    s = jnp.where(valid[None, :, None, :], s, -jnp.inf)
    m = jnp.max(s, axis=-1)
    p = jnp.exp(s - m[..., None])
    den = jnp.sum(p, axis=-1)
    num = jnp.einsum('bthn,btnhd->bthd', p, vg)
    return num, den, m


def merge_by_denominator(parts):
    m_all = parts[0][2]
    for _, _, m in parts[1:]:
        m_all = jnp.maximum(m_all, m)
    num_t, den_t = 0.0, 0.0
    for num, den, m in parts:
        c = jnp.exp(m - m_all)
        num_t = num_t + num * c[..., None]
        den_t = den_t + den * c
    return num_t / den_t[..., None]


def hier_moe(xn, w_router_group, w_router_expert, w_gate, w_up, w_down):
    shp = xn.shape
    t = xn.reshape(-1, D_MODEL)
    lg = (t @ w_router_group).astype(jnp.float32)
    pg = jax.nn.softmax(lg, axis=-1)
    g = jnp.argmax(lg, axis=-1)
    p_g = jnp.max(pg, axis=-1, keepdims=True)
    le = (t @ w_router_expert).astype(jnp.float32).reshape(-1, N_GROUPS, EXPERTS_PER_GROUP)
    le_g = jnp.einsum('nge,ng->ne', le, jax.nn.one_hot(g, N_GROUPS, dtype=jnp.float32))
    top_v, top_i = lax.top_k(le_g, TOP_K_IN_GROUP)
    w_sel = jax.nn.softmax(top_v, axis=-1) * p_g
    eid = g[:, None] * EXPERTS_PER_GROUP + top_i
    combine = jnp.einsum('nk,nke->ne', w_sel, jax.nn.one_hot(eid, N_EXPERTS, dtype=jnp.float32))

    def expert_step(acc, ew):
        wg, wu, wd, cw = ew
        h = jax.nn.silu(t @ wg) * (t @ wu)
        return acc + cw[:, None] * (h @ wd).astype(jnp.float32), None

    y, _ = lax.scan(expert_step, jnp.zeros(t.shape, jnp.float32), (w_gate, w_up, w_down, combine.T))
    return y.astype(xn.dtype).reshape(shp)


def finish_layer(x, attn, b_gate, conv_out, g_attn_out, g_conv_out, w_out,
                 g_norm_ffn, w_router_group, w_router_expert, w_gate, w_up, w_down):
    B_, T_ = x.shape[:2]
    a = rms_norm(attn.reshape(B_, T_, D_ATTN), g_attn_out)
    c = rms_norm(b_gate * conv_out, g_conv_out)
    h = x + jnp.concatenate([a, c], axis=-1) @ w_out
    return h + hier_moe(rms_norm(h, g_norm_ffn), w_router_group, w_router_expert, w_gate, w_up, w_down)


def prompt_layer(x, g_norm_mix, w_in, g_q, g_k, w_conv, g_attn_out, g_conv_out, w_out,
                 g_norm_ffn, w_router_group, w_router_expert, w_gate, w_up, w_down):
    S_ = x.shape[1]
    q, k, v, b_gate, u = mix_input(x, jnp.arange(S_), g_norm_mix, w_in, g_q, g_k)
    qf, kf, vf = q.astype(jnp.float32), k.astype(jnp.float32), v.astype(jnp.float32)
    parts = [dilated_branch_prompt(qf, kf, vf, d, w // d) for (w, d) in DILATED_BRANCHES]
    attn = merge_by_denominator(parts).astype(x.dtype)
    up = jnp.pad(u, ((0, 0), (CONV_WIDTH - 1, 0), (0, 0)))
    conv_out = sum(w_conv[j] * up[:, j:j + S_] for j in range(CONV_WIDTH))
    y = finish_layer(x, attn, b_gate, conv_out, g_attn_out, g_conv_out, w_out,
                     g_norm_ffn, w_router_group, w_router_expert, w_gate, w_up, w_down)
    keep = min(MAX_WINDOW, S_)
    return y, k[:, S_ - keep:], v[:, S_ - keep:], u[:, S_ - (CONV_WIDTH - 1):]


def sample_layer(x, cache_k, cache_v, state_conv, g_norm_mix, w_in, g_q, g_k, w_conv, g_attn_out, g_conv_out, w_out,
                 g_norm_ffn, w_router_group, w_router_expert, w_gate, w_up, w_down):
    T_ = x.shape[1]
    w_buf = cache_k.shape[1]
    q, k, v, b_gate, u = mix_input(x, PAST_LEN + jnp.arange(T_), g_norm_mix, w_in, g_q, g_k)
    k_all = jnp.concatenate([cache_k.astype(jnp.float32), k.astype(jnp.float32)], axis=1)
    v_all = jnp.concatenate([cache_v.astype(jnp.float32), v.astype(jnp.float32)], axis=1)
    qf = q.astype(jnp.float32)
    parts = [dilated_branch_sample(qf, k_all, v_all, w_buf, d, w // d) for (w, d) in DILATED_BRANCHES]
    attn = merge_by_denominator(parts).astype(x.dtype)
    full = jnp.concatenate([state_conv.astype(u.dtype), u], axis=1)
    conv_out = sum(w_conv[j] * full[:, j:j + T_] for j in range(CONV_WIDTH))
    y = finish_layer(x, attn, b_gate, conv_out, g_attn_out, g_conv_out, w_out,
                     g_norm_ffn, w_router_group, w_router_expert, w_gate, w_up, w_down)
    return y, k, v, full[:, T_:]


def setup_inputs(seed: int = 0) -> dict:
    key = jax.random.key(seed)
    ks = jax.random.split(key, 20)
    f32 = jnp.float32
    w_buf = min(MAX_WINDOW, PAST_LEN)
    nrm = lambda kk, shape, scale: jax.random.normal(kk, shape, f32) * scale
    gain = lambda kk, shape: 1.0 + 0.02 * jax.random.normal(kk, shape, f32)
    return {
        "x_prompt": nrm(ks[0], (BATCH, SEQ, D_MODEL), 1.0),
        "x_sample": nrm(ks[1], (DEC_BATCH, DEC_SEQ, D_MODEL), 1.0),
        "cache_k": nrm(ks[2], (DEPTH, DEC_BATCH, w_buf, N_ATTN_HEADS, HEAD_DIM), 1.0),
        "cache_v": nrm(ks[3], (DEPTH, DEC_BATCH, w_buf, N_ATTN_HEADS, HEAD_DIM), 1.0),
        "state_conv": nrm(ks[4], (DEPTH, DEC_BATCH, CONV_WIDTH - 1, D_CONV), 1.0),
        "g_norm_mix": gain(ks[5], (DEPTH, D_MODEL)),
        "w_in": nrm(ks[6], (DEPTH, D_MODEL, D_IN_PROJ), D_MODEL ** -0.5),
        "g_q": gain(ks[7], (DEPTH, HEAD_DIM)),
        "g_k": gain(ks[8], (DEPTH, HEAD_DIM)),
        "w_conv": nrm(ks[9], (DEPTH, CONV_WIDTH, D_CONV), CONV_WIDTH ** -0.5),
        "g_attn_out": gain(ks[10], (DEPTH, D_ATTN)),
        "g_conv_out": gain(ks[11], (DEPTH, D_CONV)),
        "w_out": nrm(ks[12], (DEPTH, D_MODEL, D_MODEL), D_MODEL ** -0.5),
        "g_norm_ffn": gain(ks[13], (DEPTH, D_MODEL)),
        "w_router_group": nrm(ks[14], (DEPTH, D_MODEL, N_GROUPS), D_MODEL ** -0.5),
        "w_router_expert": nrm(ks[15], (DEPTH, D_MODEL, N_EXPERTS), D_MODEL ** -0.5),
        "w_gate": nrm(ks[16], (DEPTH, N_EXPERTS, D_MODEL, D_EXPERT), D_MODEL ** -0.5),
        "w_up": nrm(ks[17], (DEPTH, N_EXPERTS, D_MODEL, D_EXPERT), D_MODEL ** -0.5),
        "w_down": nrm(ks[18], (DEPTH, N_EXPERTS, D_EXPERT, D_MODEL), D_EXPERT ** -0.5),
    }


def reference(x_prompt, x_sample, cache_k, cache_v, state_conv, g_norm_mix, w_in, g_q, g_k, w_conv,
              g_attn_out, g_conv_out, w_out, g_norm_ffn, w_router_group, w_router_expert, w_gate, w_up, w_down):
    yp, ys = x_prompt, x_sample
    kp, vp, cp, ksm, vsm, csm = [], [], [], [], [], []
    for l in range(DEPTH):
        weights = (g_norm_mix[l], w_in[l], g_q[l], g_k[l], w_conv[l], g_attn_out[l], g_conv_out[l], w_out[l],
                   g_norm_ffn[l], w_router_group[l], w_router_expert[l], w_gate[l], w_up[l], w_down[l])
        yp, k_p, v_p, c_p = prompt_layer(yp, *weights)
        ys, k_s, v_s, c_s = sample_layer(ys, cache_k[l], cache_v[l], state_conv[l], *weights)
        kp.append(k_p); vp.append(v_p); cp.append(c_p)
        ksm.append(k_s); vsm.append(v_s); csm.append(c_s)
    return (yp, ys, jnp.stack(kp), jnp.stack(vp), jnp.stack(cp), jnp.stack(ksm), jnp.stack(vsm), jnp.stack(csm))
```

```python
import functools

import numpy as np
import jax
import jax.numpy as jnp
from jax import lax
from jax.experimental import pallas as pl
from jax.experimental.pallas import tpu as pltpu

F32 = jnp.float32
BF16 = jnp.bfloat16

D_MODEL = 1024
HEAD_DIM = 64
N_HEADS = 12
D_ATTN = N_HEADS * HEAD_DIM
D_CONV = D_MODEL - D_ATTN
D_IN_PROJ = 3 * D_ATTN + 3 * D_CONV
CONV_WIDTH = 3
DILATIONS = (1, 4, 16)
ATT_BLOCK = 128
MAX_WINDOW = 2048
PAST_LEN = 8192
ATTN_SCALE = HEAD_DIM ** -0.5
ROPE_THETA = 10000.0
EPS = 1e-6
N_GROUPS = 4
EXPERTS_PER_GROUP = 8
N_EXPERTS = N_GROUPS * EXPERTS_PER_GROUP
D_EXPERT = 256

LANES = 128
SUBLANES = 8
MXU_DIM = 256
NEG = -1e30
VMEM_LIMIT = 48 * 1024 * 1024


def _cparams(sem):
    return pltpu.CompilerParams(dimension_semantics=sem, vmem_limit_bytes=VMEM_LIMIT)


def _mm(a, b, hi):
    if hi:
        return jnp.dot(a, b, preferred_element_type=F32, precision=lax.Precision.HIGHEST)
    return jnp.dot(a.astype(BF16), b.astype(BF16), preferred_element_type=F32)


def _rms(x, g):
    return x * lax.rsqrt(jnp.mean(x * x, axis=-1, keepdims=True) + EPS) * g


def _rope_tables(pos):
    half = HEAD_DIM // 2
    inv = ROPE_THETA ** (-np.arange(half, dtype=np.float64) / half)
    ang = np.asarray(pos, np.float64)[:, None] * inv[None, :]
    cos, sin = np.cos(ang), np.sin(ang)
    cos2 = np.concatenate([cos, cos, cos, cos], axis=-1)
    sin2 = np.concatenate([-sin, sin, -sin, sin], axis=-1)
    return jnp.asarray(cos2, F32), jnp.asarray(sin2, F32)


def _inproj_kernel(*refs, tm, tiles_per_seq, keep_tiles, hi, given_prev):
    if given_prev:
        (x_ref, gmix_ref, w_ref, gq_ref, gk_ref, cos_ref, sin_ref, wconv_ref, p1_ref, p2_ref,
         q_ref, k_ref, v_ref, bc_ref, u_ref) = refs
    else:
        (x_ref, gmix_ref, w_ref, gq_ref, gk_ref, cos_ref, sin_ref, wconv_ref,
         q_ref, k_ref, v_ref, bc_ref, tail_ref, kt_ref, vt_ref, ubuf) = refs

    xn = _rms(x_ref[...], gmix_ref[...])
    if not hi:
        xn = xn.astype(BF16)

    r_i = lax.broadcasted_iota(jnp.int32, (MXU_DIM, MXU_DIM), 0) // HEAD_DIM
    c_i = lax.broadcasted_iota(jnp.int32, (MXU_DIM, MXU_DIM), 1) // HEAD_DIM
    head_sum = jnp.where(r_i == c_i, 1.0, 0.0).astype(F32 if hi else BF16)

    reps = D_ATTN // LANES
    cos = jnp.concatenate([cos_ref[...]] * reps, axis=-1)
    sin = jnp.concatenate([sin_ref[...]] * reps, axis=-1)
    lane = lax.broadcasted_iota(jnp.int32, (1, D_ATTN), 1)
    first_half = (lane % HEAD_DIM) < (HEAD_DIM // 2)

    def qk(col0, g_ref):
        z = _mm(xn, w_ref[:, col0:col0 + D_ATTN], hi)
        sq = z * z
        ssq = jnp.concatenate(
            [_mm(sq[:, c * MXU_DIM:(c + 1) * MXU_DIM], head_sum, hi)
             for c in range(D_ATTN // MXU_DIM)], axis=-1)
        zn = z * lax.rsqrt(ssq * (1.0 / HEAD_DIM) + EPS) * g_ref[...]
        partner = jnp.where(first_half,
                            pltpu.roll(zn, D_ATTN - HEAD_DIM // 2, 1),
                            pltpu.roll(zn, HEAD_DIM // 2, 1))
        return zn * cos + partner * sin

    q_ref[...] = qk(0, gq_ref) * ATTN_SCALE
    k = qk(D_ATTN, gk_ref)
    v = _mm(xn, w_ref[:, 2 * D_ATTN:3 * D_ATTN], hi)
    k_ref[...] = k
    v_ref[...] = v
    if not given_prev:
        @pl.when(pl.program_id(0) % tiles_per_seq >= tiles_per_seq - keep_tiles)
        def _():
            kt_ref[0] = k.T.reshape(N_HEADS, HEAD_DIM, tm)
            vt_ref[0] = v.T.reshape(N_HEADS, HEAD_DIM, tm)
    c0 = 3 * D_ATTN
    b_gate = _mm(xn, w_ref[:, c0:c0 + D_CONV], hi)
    c_gate = _mm(xn, w_ref[:, c0 + D_CONV:c0 + 2 * D_CONV], hi)
    u_in = _mm(xn, w_ref[:, c0 + 2 * D_CONV:c0 + 3 * D_CONV], hi)
    u = c_gate * u_in

    if given_prev:
        u1, u2 = p1_ref[...], p2_ref[...]
        u_ref[...] = u
    else:
        i = pl.program_id(0)

        @pl.when(i % tiles_per_seq == 0)
        def _():
            ubuf[0:SUBLANES, :] = jnp.zeros((SUBLANES, D_CONV), F32)

        @pl.when(i % tiles_per_seq != 0)
        def _():
            ubuf[0:SUBLANES, :] = ubuf[tm:tm + SUBLANES, :]

        ubuf[SUBLANES:tm + SUBLANES, :] = u
        u1 = ubuf[SUBLANES - 1:tm + SUBLANES - 1, :]
        u2 = ubuf[SUBLANES - 2:tm + SUBLANES - 2, :]
        tail_ref[...] = ubuf[tm:tm + SUBLANES, :]

    wc = wconv_ref[...]
    conv = wc[0:1, :] * u2 + wc[1:2, :] * u1 + wc[2:3, :] * u
    bc_ref[...] = b_gate * conv


def _inproj(x, gmix, w_in, gq, gk, cos, sin, w_conv, *, tm, seq_len, hi, prev=None):
    n = x.shape[0]
    nt = n // tm
    tiles_per_seq = seq_len // tm
    row = lambda w: pl.BlockSpec((tm, w), lambda i: (i, 0))
    full = lambda a: pl.BlockSpec(a.shape, lambda i: (0,) * a.ndim)
    tab = pl.BlockSpec((tm, LANES), lambda i: (i % tiles_per_seq, 0))
    in_specs = [row(D_MODEL), full(gmix), full(w_in), full(gq), full(gk), tab, tab, full(w_conv)]
    args = [x, gmix, w_in, gq, gk, cos, sin, w_conv]
    out_shape = [jax.ShapeDtypeStruct((n, D_ATTN), F32)] * 3 + [jax.ShapeDtypeStruct((n, D_CONV), F32)]
    out_specs = [row(D_ATTN)] * 3 + [row(D_CONV)]
    scratch = []
    keep_tiles = min(MAX_WINDOW, seq_len) // tm
    if prev is not None:
        in_specs += [row(D_CONV), row(D_CONV)]
        args += list(prev)
        out_shape.append(jax.ShapeDtypeStruct((n, D_CONV), F32))
        out_specs.append(row(D_CONV))
    else:
        n_seq = n // seq_len
        out_shape.append(jax.ShapeDtypeStruct((n_seq * SUBLANES, D_CONV), F32))
        out_specs.append(pl.BlockSpec((SUBLANES, D_CONV), lambda i: (i // tiles_per_seq, 0)))
        kept = pl.BlockSpec(
            (1, N_HEADS, HEAD_DIM, tm),
            lambda i: (i // tiles_per_seq, 0, 0,
                       jnp.maximum(i % tiles_per_seq - (tiles_per_seq - keep_tiles), 0)))
        out_shape += [jax.ShapeDtypeStruct((n_seq, N_HEADS, HEAD_DIM, keep_tiles * tm), F32)] * 2
        out_specs += [kept, kept]
        scratch.append(pltpu.VMEM((tm + SUBLANES, D_CONV), F32))
    return pl.pallas_call(
        functools.partial(_inproj_kernel, tm=tm, tiles_per_seq=tiles_per_seq,
                          keep_tiles=keep_tiles, hi=hi, given_prev=prev is not None),
        grid=(nt,), in_specs=in_specs, out_specs=out_specs, out_shape=out_shape,
        scratch_shapes=scratch, compiler_params=_cparams(("arbitrary",)),
        name="inproj_hi" if hi else "inproj",
    )(*args)


def _attn_kernel(q_ref, k_ref, v_ref, o_ref, num_ref, den_ref, m_ref, bias_ref, *, seq_len):
    blk = ATT_BLOCK
    qi = lax.broadcasted_iota(jnp.int32, (blk, 2 * blk), 0)
    kj = lax.broadcasted_iota(jnp.int32, (blk, 2 * blk), 1)
    bias_ref[...] = jnp.where((kj >= qi) & (kj <= qi + blk), 0.0, NEG)
    head0 = lax.broadcasted_iota(jnp.int32, (1, LANES), 1) < HEAD_DIM
    dn = (((1,), (1,)), ((), ()))

    def rows(start, size, d):
        return pl.ds(start, size) if d == 1 else pl.ds(start, size, stride=d)

    def tile(d, r, n, first, mode):
        q_rows = rows(n * (blk * d) + r, blk, d)
        if first:
            k_rows = rows(r, blk, d)
            bias = bias_ref[:, blk:2 * blk]
        else:
            k_rows = rows((n - 1) * (blk * d) + r, 2 * blk, d)
            bias = bias_ref[...]
        qt = q_ref[0, q_rows, :]
        kt = k_ref[0, k_rows, :].astype(BF16)
        vt = v_ref[0, k_rows, :]
        s0 = lax.dot_general(jnp.where(head0, qt, 0.0).astype(BF16), kt, dn,
                             preferred_element_type=F32) + bias
        s1 = lax.dot_general(jnp.where(head0, 0.0, qt).astype(BF16), kt, dn,
                             preferred_element_type=F32) + bias
        m0 = jnp.max(s0, axis=-1, keepdims=True)
        m1 = jnp.max(s1, axis=-1, keepdims=True)
        p0 = jnp.exp(s0 - m0).astype(BF16)
        p1 = jnp.exp(s1 - m1).astype(BF16)
        r0 = jnp.dot(p0, jnp.where(head0, vt, 1.0).astype(BF16), preferred_element_type=F32)
        r1 = jnp.dot(p1, jnp.where(head0, 1.0, vt).astype(BF16), preferred_element_type=F32)
        num = jnp.where(head0, r0, r1)
        den = pltpu.roll(jnp.where(head0, r1, r0), HEAD_DIM, 1)
        m = jnp.where(head0, m0, m1)
        if mode == "init":
            num_ref[q_rows, :] = num
            den_ref[q_rows, :] = den
            m_ref[q_rows, :] = m
            return
        m_old = m_ref[q_rows, :]
        m_new = jnp.maximum(m_old, m)
        a = jnp.exp(m_old - m_new)
        b = jnp.exp(m - m_new)
        num = num_ref[q_rows, :] * a + num * b
        den = den_ref[q_rows, :] * a + den * b
        if mode == "merge":
            num_ref[q_rows, :] = num
            den_ref[q_rows, :] = den
            m_ref[q_rows, :] = m_new
        else:
            o_ref[0, q_rows, :] = num / den

    modes = ("init", "merge", "final")
    for d, mode in zip(DILATIONS, modes):
        nb = seq_len // (d * blk)

        def per_residue(r, carry, d=d, mode=mode, nb=nb):
            tile(d, r, 0, True, mode)

            def per_block(n, c):
                tile(d, r, n, False, mode)
                return c

            lax.fori_loop(1, nb, per_block, 0)
            return carry

        lax.fori_loop(0, d, per_residue, 0)


def _attention(q, k, v):
    b, s, _ = q.shape
    spec = pl.BlockSpec((1, s, LANES), lambda i, p: (i, 0, p))
    return pl.pallas_call(
        functools.partial(_attn_kernel, seq_len=s),
        grid=(b, D_ATTN // LANES), in_specs=[spec] * 3, out_specs=spec,
        out_shape=jax.ShapeDtypeStruct((b, s, D_ATTN), F32),
        scratch_shapes=[pltpu.VMEM((s, LANES), F32)] * 3 + [pltpu.VMEM((ATT_BLOCK, 2 * ATT_BLOCK), F32)],
        compiler_params=_cparams(("arbitrary", "arbitrary")), name="attn_prompt",
    )(q, k, v)


def _attn_sample_kernel(q_ref, kn_ref, vn_ref, kc_ref, vc_ref, o_ref, *, w_buf, heads):
    age = w_buf - lax.broadcasted_iota(jnp.int32, (1, w_buf), 1)
    valid = [(age % d == 0) & (age <= d * ATT_BLOCK) for d in DILATIONS]
    for h in range(heads):
        q = q_ref[0, h]
        s = jnp.sum(kc_ref[0, h] * q, axis=0, keepdims=True)
        s_self = jnp.sum(kn_ref[0, h] * q, axis=0, keepdims=True)
        parts = []
        for ok in valid:
            sd = jnp.where(ok, s, NEG)
            m = jnp.maximum(jnp.max(sd, axis=1, keepdims=True), s_self)
            p = jnp.exp(sd - m)
            p_self = jnp.exp(s_self - m)
            parts.append((p, p_self, jnp.sum(p, axis=1, keepdims=True) + p_self, m))
        m_all = jnp.maximum(jnp.maximum(parts[0][3], parts[1][3]), parts[2][3])
        w = jnp.zeros((1, w_buf), F32)
        w_self = jnp.zeros((1, 1), F32)
        den_t = jnp.zeros((1, 1), F32)
        for p, p_self, den, m in parts:
            c = jnp.exp(m - m_all)
            w = w + p * c
            w_self = w_self + p_self * c
            den_t = den_t + den * c
        num_t = jnp.sum(vc_ref[0, h] * w, axis=1, keepdims=True) + vn_ref[0, h] * w_self
        o_ref[0, h] = num_t / den_t


def _attention_sample(q, kn, vn, cache_kt, cache_vt, *, heads=6):
    db, _, _, w_buf = cache_kt.shape
    tok = pl.BlockSpec((1, heads, HEAD_DIM, 1), lambda i, g: (i, g, 0, 0))
    cache = pl.BlockSpec((1, heads, HEAD_DIM, w_buf), lambda i, g: (i, g, 0, 0))
    return pl.pallas_call(
        functools.partial(_attn_sample_kernel, w_buf=w_buf, heads=heads),
        grid=(db, N_HEADS // heads), in_specs=[tok, tok, tok, cache, cache], out_specs=tok,
        out_shape=jax.ShapeDtypeStruct((db, N_HEADS, HEAD_DIM, 1), F32),
        compiler_params=_cparams(("arbitrary", "arbitrary")), name="attn_sample",
    )(q, kn, vn, cache_kt, cache_vt)


def _outproj_kernel(attn_ref, bc_ref, x_ref, ga_ref, gc_ref, wo_ref, gf_ref, wr_ref,
                    h_ref, hn_ref, cw_ref, *, hi):
    a = _rms(attn_ref[...], ga_ref[...])
    c = _rms(bc_ref[...], gc_ref[...])
    mix = _mm(a, wo_ref[0:D_ATTN, :], hi) + _mm(c, wo_ref[D_ATTN:D_MODEL, :], hi)
    h = x_ref[...] + mix
    h_ref[...] = h
    hn = _rms(h, gf_ref[...])
    hn_ref[...] = hn.astype(hn_ref.dtype)

    lg = _mm(hn, wr_ref[...], True)
    lane_i = lax.broadcasted_iota(jnp.int32, lg.shape, 1)
    lane = lane_i.astype(F32)
    lane_group = (lane_i // EXPERTS_PER_GROUP).astype(F32)
    is_e = lane_i < N_EXPERTS
    is_g = (lane_i >= N_EXPERTS) & (lane_i < N_EXPERTS + N_GROUPS)
    big = float(LANES)
    first_at = lambda t, v: jnp.min(jnp.where(t == v, lane, big), axis=-1, keepdims=True)

    gl = jnp.where(is_g, lg, NEG)
    mg = jnp.max(gl, axis=-1, keepdims=True)
    p_g = 1.0 / jnp.sum(jnp.exp(gl - mg), axis=-1, keepdims=True)
    g_idx = first_at(gl, mg) - float(N_EXPERTS)
    el = jnp.where(is_e & (lane_group == g_idx), lg, NEG)
    v1 = jnp.max(el, axis=-1, keepdims=True)
    i1 = first_at(el, v1)
    el2 = jnp.where(lane == i1, NEG, el)
    v2 = jnp.max(el2, axis=-1, keepdims=True)
    i2 = first_at(el2, v2)
    e2 = jnp.exp(v2 - v1)
    w1 = p_g / (1.0 + e2)
    w2 = p_g * e2 / (1.0 + e2)
    cw_ref[...] = jnp.where(lane == i1, w1, 0.0) + jnp.where(lane == i2, w2, 0.0)


def _outproj(attn, bc, x, ga, gc, w_out, gf, w_router, *, tm, hi):
    n = x.shape[0]
    row = lambda w: pl.BlockSpec((tm, w), lambda i: (i, 0))
    full = lambda a: pl.BlockSpec(a.shape, lambda i: (0,) * a.ndim)
    return pl.pallas_call(
        functools.partial(_outproj_kernel, hi=hi),
        grid=(n // tm,),
        in_specs=[row(D_ATTN), row(D_CONV), row(D_MODEL), full(ga), full(gc), full(w_out),
                  full(gf), full(w_router)],
        out_specs=[row(D_MODEL), row(D_MODEL), row(LANES)],
        out_shape=[jax.ShapeDtypeStruct((n, D_MODEL), F32),
                   jax.ShapeDtypeStruct((n, D_MODEL), BF16),
                   jax.ShapeDtypeStruct((n, LANES), F32)],
        compiler_params=_cparams(("arbitrary",)), name="outproj_hi" if hi else "outproj",
    )(attn, bc, x, ga, gc, w_out, gf, w_router)


def _moe_kernel(hn_ref, h_ref, cw_ref, wg_ref, wu_ref, wd_ref, y_ref):
    e = pl.program_id(1)

    @pl.when(e == 0)
    def _():
        y_ref[...] = h_ref[...]

    x = hn_ref[...]
    g = jnp.dot(x, wg_ref[0].astype(BF16), preferred_element_type=F32)
    u = jnp.dot(x, wu_ref[0].astype(BF16), preferred_element_type=F32)
    act = (jax.nn.silu(g) * u).astype(BF16)
    o = jnp.dot(act, wd_ref[0].astype(BF16), preferred_element_type=F32)
    lane = lax.broadcasted_iota(jnp.int32, cw_ref.shape, 1)
    cw = jnp.sum(jnp.where(lane == e, cw_ref[...], 0.0), axis=-1, keepdims=True)
    y_ref[...] += cw * o


def _moe(hn, h, cw, w_gate, w_up, w_down, *, tm):
    n = hn.shape[0]
    row = lambda w: pl.BlockSpec((tm, w), lambda i, e: (i, 0))
    return pl.pallas_call(
        _moe_kernel, grid=(n // tm, N_EXPERTS),
        in_specs=[row(D_MODEL), row(D_MODEL), row(LANES),
                  pl.BlockSpec((1, D_MODEL, D_EXPERT), lambda i, e: (e, 0, 0)),
                  pl.BlockSpec((1, D_MODEL, D_EXPERT), lambda i, e: (e, 0, 0)),
                  pl.BlockSpec((1, D_EXPERT, D_MODEL), lambda i, e: (e, 0, 0))],
        out_specs=row(D_MODEL), out_shape=jax.ShapeDtypeStruct((n, D_MODEL), F32),
        compiler_params=_cparams(("arbitrary", "arbitrary")), name="moe",
    )(hn, h, cw, w_gate, w_up, w_down)


def kernel(x_prompt, x_sample, cache_k, cache_v, state_conv, g_norm_mix, w_in, g_q, g_k, w_conv,
           g_attn_out, g_conv_out, w_out, g_norm_ffn, w_router_group, w_router_expert,
           w_gate, w_up, w_down):
    depth = g_norm_mix.shape[0]
    assert depth == 1
    nb, seq, _ = x_prompt.shape
    db, dec_seq, _ = x_sample.shape
    assert dec_seq == 1
    l = 0

    gmix = g_norm_mix[l][None, :]
    gq = jnp.tile(g_q[l], N_HEADS)[None, :]
    gk = jnp.tile(g_k[l], N_HEADS)[None, :]
    ga = g_attn_out[l][None, :]
    gc = g_conv_out[l][None, :]
    gf = g_norm_ffn[l][None, :]
    w_router = jnp.concatenate(
        [w_router_expert[l], w_router_group[l],
         jnp.zeros((D_MODEL, LANES - N_EXPERTS - N_GROUPS), F32)], axis=1)

    cos_p, sin_p = _rope_tables(np.arange(seq))
    xp = x_prompt.reshape(nb * seq, D_MODEL)
    q, k, v, bc, tail, kt, vt = _inproj(xp, gmix, w_in[l].astype(BF16), gq, gk, cos_p, sin_p,
                                        w_conv[l], tm=256, seq_len=seq, hi=False)
    attn = _attention(q.reshape(nb, seq, D_ATTN), k.reshape(nb, seq, D_ATTN),
                      v.reshape(nb, seq, D_ATTN))
    h, hn, cw = _outproj(attn.reshape(nb * seq, D_ATTN), bc, xp, ga, gc, w_out[l].astype(BF16),
                         gf, w_router, tm=256, hi=False)
    y_prompt = _moe(hn, h, cw, w_gate[l], w_up[l], w_down[l], tm=1024).reshape(nb, seq, D_MODEL)
    k_prompt = jnp.transpose(kt, (0, 3, 1, 2))
    v_prompt = jnp.transpose(vt, (0, 3, 1, 2))
    conv_prompt = tail.reshape(nb, SUBLANES, D_CONV)[:, SUBLANES - (CONV_WIDTH - 1):]

    cos_s, sin_s = _rope_tables(np.full((db,), PAST_LEN))
    xs = x_sample.reshape(db, D_MODEL)
    st = state_conv[l]
    qs, ks, vs, bcs, us = _inproj(xs, gmix, w_in[l], gq, gk, cos_s, sin_s, w_conv[l],
                                  tm=db, seq_len=db, hi=True, prev=(st[:, 1], st[:, 0]))
    cols = lambda t: t.reshape(db, N_HEADS, HEAD_DIM, 1)
    attn_s = _attention_sample(cols(qs), cols(ks), cols(vs),
                               jnp.transpose(cache_k[l], (0, 2, 3, 1)),
                               jnp.transpose(cache_v[l], (0, 2, 3, 1)))
    hs, hns, cws = _outproj(attn_s.reshape(db, D_ATTN), bcs, xs, ga, gc, w_out[l], gf, w_router,
                            tm=db, hi=True)
    y_sample = _moe(hns, hs, cws, w_gate[l], w_up[l], w_down[l], tm=db).reshape(db, 1, D_MODEL)
    k_sample = ks.reshape(db, 1, N_HEADS, HEAD_DIM)
    v_sample = vs.reshape(db, 1, N_HEADS, HEAD_DIM)
    conv_sample = jnp.stack([st[:, 1], us], axis=1)

    return (y_prompt, y_sample, k_prompt[None], v_prompt[None], conv_prompt[None],
            k_sample[None], v_sample[None], conv_sample[None])
```

```python
import functools

import numpy as np
import jax
import jax.numpy as jnp
from jax import lax
from jax.experimental import pallas as pl
from jax.experimental.pallas import tpu as pltpu

F32 = jnp.float32
BF16 = jnp.bfloat16

D_MODEL = 1024
HEAD_DIM = 64
N_HEADS = 12
D_ATTN = N_HEADS * HEAD_DIM
D_CONV = D_MODEL - D_ATTN
D_IN_PROJ = 3 * D_ATTN + 3 * D_CONV
CONV_WIDTH = 3
DILATIONS = (1, 4, 16)
ATT_BLOCK = 128
ATT_UNROLL = 8
MAX_WINDOW = 2048
PAST_LEN = 8192
ATTN_SCALE = HEAD_DIM ** -0.5
ROPE_THETA = 10000.0
EPS = 1e-6
N_GROUPS = 4
EXPERTS_PER_GROUP = 8
N_EXPERTS = N_GROUPS * EXPERTS_PER_GROUP
D_EXPERT = 256

LANES = 128
SUBLANES = 8
MXU_DIM = 256
NEG = -1e30
VMEM_LIMIT = 48 * 1024 * 1024


def _cparams(sem):
    return pltpu.CompilerParams(dimension_semantics=sem, vmem_limit_bytes=VMEM_LIMIT)


def _mm(a, b, hi):
    if hi:
        return jnp.dot(a, b, preferred_element_type=F32, precision=lax.Precision.HIGHEST)
    return jnp.dot(a.astype(BF16), b.astype(BF16), preferred_element_type=F32)


def _rms(x, g):
    return x * lax.rsqrt(jnp.mean(x * x, axis=-1, keepdims=True) + EPS) * g


def _rope_tables(pos):
    half = HEAD_DIM // 2
    inv = ROPE_THETA ** (-np.arange(half, dtype=np.float64) / half)
    ang = np.asarray(pos, np.float64)[:, None] * inv[None, :]
    cos, sin = np.cos(ang), np.sin(ang)
    cos2 = np.concatenate([cos, cos, cos, cos], axis=-1)
    sin2 = np.concatenate([-sin, sin, -sin, sin], axis=-1)
    return jnp.asarray(cos2, F32), jnp.asarray(sin2, F32)


def _inproj_kernel(*refs, tm, tiles_per_seq, keep_tiles, hi, given_prev):
    if given_prev:
        (x_ref, gmix_ref, w_ref, gq_ref, gk_ref, cos_ref, sin_ref, wconv_ref, p1_ref, p2_ref,
         q_ref, k_ref, v_ref, bc_ref, u_ref) = refs
    else:
        (x_ref, gmix_ref, w_ref, gq_ref, gk_ref, cos_ref, sin_ref, wconv_ref,
         q_ref, k_ref, v_ref, bc_ref, tail_ref, kt_ref, vt_ref, ubuf) = refs

    xn = _rms(x_ref[...], gmix_ref[...])
    if not hi:
        xn = xn.astype(BF16)

    r_i = lax.broadcasted_iota(jnp.int32, (MXU_DIM, MXU_DIM), 0) // HEAD_DIM
    c_i = lax.broadcasted_iota(jnp.int32, (MXU_DIM, MXU_DIM), 1) // HEAD_DIM
    head_sum = jnp.where(r_i == c_i, 1.0, 0.0).astype(F32 if hi else BF16)

    reps = D_ATTN // LANES
    cos = jnp.concatenate([cos_ref[...]] * reps, axis=-1)
    sin = jnp.concatenate([sin_ref[...]] * reps, axis=-1)
    lane = lax.broadcasted_iota(jnp.int32, (1, D_ATTN), 1)
    first_half = (lane % HEAD_DIM) < (HEAD_DIM // 2)

    def qk(col0, g_ref):
        z = _mm(xn, w_ref[:, col0:col0 + D_ATTN], hi)
        sq = z * z
        ssq = jnp.concatenate(
            [_mm(sq[:, c * MXU_DIM:(c + 1) * MXU_DIM], head_sum, hi)
             for c in range(D_ATTN // MXU_DIM)], axis=-1)
        zn = z * lax.rsqrt(ssq * (1.0 / HEAD_DIM) + EPS) * g_ref[...]
        partner = jnp.where(first_half,
                            pltpu.roll(zn, D_ATTN - HEAD_DIM // 2, 1),
                            pltpu.roll(zn, HEAD_DIM // 2, 1))
        return zn * cos + partner * sin

    q_ref[...] = qk(0, gq_ref) * ATTN_SCALE
    k = qk(D_ATTN, gk_ref)
    v = _mm(xn, w_ref[:, 2 * D_ATTN:3 * D_ATTN], hi)
    k_ref[...] = k
    v_ref[...] = v
    if not given_prev:
        @pl.when(pl.program_id(0) % tiles_per_seq >= tiles_per_seq - keep_tiles)
        def _():
            kt_ref[0] = k.T.reshape(N_HEADS, HEAD_DIM, tm)
            vt_ref[0] = v.T.reshape(N_HEADS, HEAD_DIM, tm)
    c0 = 3 * D_ATTN
    b_gate = _mm(xn, w_ref[:, c0:c0 + D_CONV], hi)
    c_gate = _mm(xn, w_ref[:, c0 + D_CONV:c0 + 2 * D_CONV], hi)
    u_in = _mm(xn, w_ref[:, c0 + 2 * D_CONV:c0 + 3 * D_CONV], hi)
    u = c_gate * u_in

    if given_prev:
        u1, u2 = p1_ref[...], p2_ref[...]
        u_ref[...] = u
    else:
        i = pl.program_id(0)

        @pl.when(i % tiles_per_seq == 0)
        def _():
            ubuf[0:SUBLANES, :] = jnp.zeros((SUBLANES, D_CONV), F32)

        @pl.when(i % tiles_per_seq != 0)
        def _():
            ubuf[0:SUBLANES, :] = ubuf[tm:tm + SUBLANES, :]

        ubuf[SUBLANES:tm + SUBLANES, :] = u
        u1 = ubuf[SUBLANES - 1:tm + SUBLANES - 1, :]
        u2 = ubuf[SUBLANES - 2:tm + SUBLANES - 2, :]
        tail_ref[...] = ubuf[tm:tm + SUBLANES, :]

    wc = wconv_ref[...]
    conv = wc[0:1, :] * u2 + wc[1:2, :] * u1 + wc[2:3, :] * u
    bc_ref[...] = b_gate * conv


def _inproj(x, gmix, w_in, gq, gk, cos, sin, w_conv, *, tm, seq_len, hi, prev=None):
    n = x.shape[0]
    nt = n // tm
    tiles_per_seq = seq_len // tm
    row = lambda w: pl.BlockSpec((tm, w), lambda i: (i, 0))
    full = lambda a: pl.BlockSpec(a.shape, lambda i: (0,) * a.ndim)
    tab = pl.BlockSpec((tm, LANES), lambda i: (i % tiles_per_seq, 0))
    in_specs = [row(D_MODEL), full(gmix), full(w_in), full(gq), full(gk), tab, tab, full(w_conv)]
    args = [x, gmix, w_in, gq, gk, cos, sin, w_conv]
    out_shape = [jax.ShapeDtypeStruct((n, D_ATTN), F32)] * 3 + [jax.ShapeDtypeStruct((n, D_CONV), F32)]
    out_specs = [row(D_ATTN)] * 3 + [row(D_CONV)]
    scratch = []
    keep_tiles = min(MAX_WINDOW, seq_len) // tm
    if prev is not None:
        in_specs += [row(D_CONV), row(D_CONV)]
        args += list(prev)
        out_shape.append(jax.ShapeDtypeStruct((n, D_CONV), F32))
        out_specs.append(row(D_CONV))
    else:
        n_seq = n // seq_len
        out_shape.append(jax.ShapeDtypeStruct((n_seq * SUBLANES, D_CONV), F32))
        out_specs.append(pl.BlockSpec((SUBLANES, D_CONV), lambda i: (i // tiles_per_seq, 0)))
        kept = pl.BlockSpec(
            (1, N_HEADS, HEAD_DIM, tm),
            lambda i: (i // tiles_per_seq, 0, 0,
                       jnp.maximum(i % tiles_per_seq - (tiles_per_seq - keep_tiles), 0)))
        out_shape += [jax.ShapeDtypeStruct((n_seq, N_HEADS, HEAD_DIM, keep_tiles * tm), F32)] * 2
        out_specs += [kept, kept]
        scratch.append(pltpu.VMEM((tm + SUBLANES, D_CONV), F32))
    return pl.pallas_call(
        functools.partial(_inproj_kernel, tm=tm, tiles_per_seq=tiles_per_seq,
                          keep_tiles=keep_tiles, hi=hi, given_prev=prev is not None),
        grid=(nt,), in_specs=in_specs, out_specs=out_specs, out_shape=out_shape,
        scratch_shapes=scratch, compiler_params=_cparams(("arbitrary",)),
        name="inproj_hi" if hi else "inproj",
    )(*args)


def _attn_kernel(q_ref, k_ref, v_ref, o_ref, num_ref, den_ref, m_ref, bias_ref, *, seq_len):
    blk = ATT_BLOCK
    qi = lax.broadcasted_iota(jnp.int32, (blk, 2 * blk), 0)
    kj = lax.broadcasted_iota(jnp.int32, (blk, 2 * blk), 1)
    bias_ref[...] = jnp.where((kj >= qi) & (kj <= qi + blk), 0.0, NEG)
    head0 = lax.broadcasted_iota(jnp.int32, (1, LANES), 1) < HEAD_DIM
    dn = (((1,), (1,)), ((), ()))

    def rows(start, size, d):
        return pl.ds(start, size) if d == 1 else pl.ds(start, size, stride=d)

    def group(d, r, n0, u, first, mode):
        q_rows = rows(n0 * (blk * d) + r, u * blk, d)
        if first:
            k_rows = rows(r, u * blk, d)
        else:
            k_rows = rows((n0 - 1) * (blk * d) + r, (u + 1) * blk, d)
        qa = q_ref[0, q_rows, :]
        ka = k_ref[0, k_rows, :].astype(BF16)
        va = v_ref[0, k_rows, :]
        qa0 = jnp.where(head0, qa, 0.0).astype(BF16)
        qa1 = jnp.where(head0, 0.0, qa).astype(BF16)
        va0 = jnp.where(head0, va, 1.0).astype(BF16)
        va1 = jnp.where(head0, 1.0, va).astype(BF16)
        if mode != "init":
            m_old = m_ref[q_rows, :]
            num_old = num_ref[q_rows, :]
            den_old = den_ref[q_rows, :]
        key_slices, scores = [], []
        for i in range(u):
            qs = slice(i * blk, (i + 1) * blk)
            if first and i == 0:
                ks = slice(0, blk)
                bias = bias_ref[:, blk:2 * blk]
            else:
                k0 = (i - 1) * blk if first else i * blk
                ks = slice(k0, k0 + 2 * blk)
                bias = bias_ref[...]
            key_slices.append(ks)
            scores.append(
                (lax.dot_general(qa0[qs], ka[ks], dn, preferred_element_type=F32) + bias,
                 lax.dot_general(qa1[qs], ka[ks], dn, preferred_element_type=F32) + bias))
        probs, ms = [], []
        for s0, s1 in scores:
            m0 = jnp.max(s0, axis=-1, keepdims=True)
            m1 = jnp.max(s1, axis=-1, keepdims=True)
            probs.append((jnp.exp(s0 - m0).astype(BF16), jnp.exp(s1 - m1).astype(BF16)))
            ms.append(jnp.where(head0, m0, m1))
        nums, dens = [], []
        for (p0, p1), ks in zip(probs, key_slices):
            r0 = jnp.dot(p0, va0[ks], preferred_element_type=F32)
            r1 = jnp.dot(p1, va1[ks], preferred_element_type=F32)
            nums.append(jnp.where(head0, r0, r1))
            dens.append(pltpu.roll(jnp.where(head0, r1, r0), HEAD_DIM, 1))
        num = jnp.concatenate(nums, axis=0)
        den = jnp.concatenate(dens, axis=0)
        m = jnp.concatenate(ms, axis=0)
        if mode == "init":
            return [(num_ref, q_rows, num), (den_ref, q_rows, den), (m_ref, q_rows, m)]
        m_new = jnp.maximum(m_old, m)
        a = jnp.exp(m_old - m_new)
        b = jnp.exp(m - m_new)
        num = num_old * a + num * b
        den = den_old * a + den * b
        if mode == "merge":
            return [(num_ref, q_rows, num), (den_ref, q_rows, den), (m_ref, q_rows, m_new)]
        return [(o_ref.at[0], q_rows, num / den)]

    def run(groups):
        stores = [s for g in groups for s in group(*g)]
        for ref, at, val in stores:
            ref[at, :] = val

    for d, mode in zip(DILATIONS, ("init", "merge", "final")):
        nb = seq_len // (d * blk)
        u = min(ATT_UNROLL, nb)
        if nb == u:
            per_body = ATT_UNROLL // u

            def residues(rr, carry, d=d, mode=mode, u=u, per_body=per_body):
                run([(d, rr * per_body + t, 0, u, True, mode) for t in range(per_body)])
                return carry

            lax.fori_loop(0, d // per_body, residues, 0)
        else:
            def per_residue(r, carry, d=d, mode=mode, u=u, nb=nb):
                run([(d, r, 0, u, True, mode)])

                def per_block(g, c):
                    run([(d, r, g * u, u, False, mode)])
                    return c

                lax.fori_loop(1, nb // u, per_block, 0)
                return carry

            lax.fori_loop(0, d, per_residue, 0)


def _attention(q, k, v):
    b, s, _ = q.shape
    spec = pl.BlockSpec((1, s, LANES), lambda i, p: (i, 0, p))
    return pl.pallas_call(
        functools.partial(_attn_kernel, seq_len=s),
        grid=(b, D_ATTN // LANES), in_specs=[spec] * 3, out_specs=spec,
        out_shape=jax.ShapeDtypeStruct((b, s, D_ATTN), F32),
        scratch_shapes=[pltpu.VMEM((s, LANES), F32)] * 3 + [pltpu.VMEM((ATT_BLOCK, 2 * ATT_BLOCK), F32)],
        compiler_params=_cparams(("arbitrary", "arbitrary")), name="attn_prompt",
    )(q, k, v)


def _attn_sample_kernel(q_ref, kn_ref, vn_ref, kc_ref, vc_ref, o_ref, *, w_buf, heads):
    age = w_buf - lax.broadcasted_iota(jnp.int32, (1, w_buf), 1)
    valid = [(age % d == 0) & (age <= d * ATT_BLOCK) for d in DILATIONS]
    for h in range(heads):
        q = q_ref[0, h]
        s = jnp.sum(kc_ref[0, h] * q, axis=0, keepdims=True)
        s_self = jnp.sum(kn_ref[0, h] * q, axis=0, keepdims=True)
        parts = []
        for ok in valid:
            sd = jnp.where(ok, s, NEG)
            m = jnp.maximum(jnp.max(sd, axis=1, keepdims=True), s_self)
            p = jnp.exp(sd - m)
            p_self = jnp.exp(s_self - m)
            parts.append((p, p_self, jnp.sum(p, axis=1, keepdims=True) + p_self, m))
        m_all = jnp.maximum(jnp.maximum(parts[0][3], parts[1][3]), parts[2][3])
        w = jnp.zeros((1, w_buf), F32)
        w_self = jnp.zeros((1, 1), F32)
        den_t = jnp.zeros((1, 1), F32)
        for p, p_self, den, m in parts:
            c = jnp.exp(m - m_all)
            w = w + p * c
            w_self = w_self + p_self * c
            den_t = den_t + den * c
        num_t = jnp.sum(vc_ref[0, h] * w, axis=1, keepdims=True) + vn_ref[0, h] * w_self
        o_ref[0, h] = num_t / den_t


def _attention_sample(q, kn, vn, cache_kt, cache_vt, *, heads=6):
    db, _, _, w_buf = cache_kt.shape
    tok = pl.BlockSpec((1, heads, HEAD_DIM, 1), lambda i, g: (i, g, 0, 0))
    cache = pl.BlockSpec((1, heads, HEAD_DIM, w_buf), lambda i, g: (i, g, 0, 0))
    return pl.pallas_call(
        functools.partial(_attn_sample_kernel, w_buf=w_buf, heads=heads),
        grid=(db, N_HEADS // heads), in_specs=[tok, tok, tok, cache, cache], out_specs=tok,
        out_shape=jax.ShapeDtypeStruct((db, N_HEADS, HEAD_DIM, 1), F32),
        compiler_params=_cparams(("arbitrary", "arbitrary")), name="attn_sample",
    )(q, kn, vn, cache_kt, cache_vt)


def _outproj_kernel(attn_ref, bc_ref, x_ref, ga_ref, gc_ref, wo_ref, gf_ref, wr_ref,
                    h_ref, hn_ref, cw_ref, *, hi):
    a = _rms(attn_ref[...], ga_ref[...])
    c = _rms(bc_ref[...], gc_ref[...])
    mix = _mm(a, wo_ref[0:D_ATTN, :], hi) + _mm(c, wo_ref[D_ATTN:D_MODEL, :], hi)
    h = x_ref[...] + mix
    h_ref[...] = h
    hn = _rms(h, gf_ref[...])
    hn_ref[...] = hn.astype(hn_ref.dtype)

    if hi:
        lg = _mm(hn, wr_ref[...], True)
    else:
        hn_hi = hn.astype(BF16)
        hn_lo = (hn - hn_hi.astype(F32)).astype(BF16)
        both = jnp.dot(hn_hi, wr_ref[...], preferred_element_type=F32)
        lg = (both[:, :LANES] + both[:, LANES:]
              + jnp.dot(hn_lo, wr_ref[:, :LANES], preferred_element_type=F32))
    lane_i = lax.broadcasted_iota(jnp.int32, lg.shape, 1)
    lane = lane_i.astype(F32)
    lane_group = (lane_i // EXPERTS_PER_GROUP).astype(F32)
    is_e = lane_i < N_EXPERTS
    is_g = (lane_i >= N_EXPERTS) & (lane_i < N_EXPERTS + N_GROUPS)
    big = float(LANES)
    first_at = lambda t, v: jnp.min(jnp.where(t == v, lane, big), axis=-1, keepdims=True)

    gl = jnp.where(is_g, lg, NEG)
    mg = jnp.max(gl, axis=-1, keepdims=True)
    p_g = 1.0 / jnp.sum(jnp.exp(gl - mg), axis=-1, keepdims=True)
    g_idx = first_at(gl, mg) - float(N_EXPERTS)
    el = jnp.where(is_e & (lane_group == g_idx), lg, NEG)
    v1 = jnp.max(el, axis=-1, keepdims=True)
    i1 = first_at(el, v1)
    el2 = jnp.where(lane == i1, NEG, el)
    v2 = jnp.max(el2, axis=-1, keepdims=True)
    i2 = first_at(el2, v2)
    e2 = jnp.exp(v2 - v1)
    w1 = p_g / (1.0 + e2)
    w2 = p_g * e2 / (1.0 + e2)
    cw_ref[...] = jnp.where(lane == i1, w1, 0.0) + jnp.where(lane == i2, w2, 0.0)


def _outproj(attn, bc, x, ga, gc, w_out, gf, w_router, *, tm, hi):
    n = x.shape[0]
    row = lambda w: pl.BlockSpec((tm, w), lambda i: (i, 0))
    full = lambda a: pl.BlockSpec(a.shape, lambda i: (0,) * a.ndim)
    return pl.pallas_call(
        functools.partial(_outproj_kernel, hi=hi),
        grid=(n // tm,),
        in_specs=[row(D_ATTN), row(D_CONV), row(D_MODEL), full(ga), full(gc), full(w_out),
                  full(gf), full(w_router)],
        out_specs=[row(D_MODEL), row(D_MODEL), row(LANES)],
        out_shape=[jax.ShapeDtypeStruct((n, D_MODEL), F32),
                   jax.ShapeDtypeStruct((n, D_MODEL), BF16),
                   jax.ShapeDtypeStruct((n, LANES), F32)],
        compiler_params=_cparams(("arbitrary",)), name="outproj_hi" if hi else "outproj",
    )(attn, bc, x, ga, gc, w_out, gf, w_router)


def _moe_kernel(hn_ref, h_ref, cw_ref, wg_ref, wu_ref, wd_ref, y_ref):
    e = pl.program_id(1)

    @pl.when(e == 0)
    def _():
        y_ref[...] = h_ref[...]

    x = hn_ref[...]
    g = jnp.dot(x, wg_ref[0].astype(BF16), preferred_element_type=F32)
    u = jnp.dot(x, wu_ref[0].astype(BF16), preferred_element_type=F32)
    act = (jax.nn.silu(g) * u).astype(BF16)
    o = jnp.dot(act, wd_ref[0].astype(BF16), preferred_element_type=F32)
    lane = lax.broadcasted_iota(jnp.int32, cw_ref.shape, 1)
    cw = jnp.sum(jnp.where(lane == e, cw_ref[...], 0.0), axis=-1, keepdims=True)
    y_ref[...] += cw * o


def _moe(hn, h, cw, w_gate, w_up, w_down, *, tm):
    n = hn.shape[0]
    row = lambda w: pl.BlockSpec((tm, w), lambda i, e: (i, 0))
    return pl.pallas_call(
        _moe_kernel, grid=(n // tm, N_EXPERTS),
        in_specs=[row(D_MODEL), row(D_MODEL), row(LANES),
                  pl.BlockSpec((1, D_MODEL, D_EXPERT), lambda i, e: (e, 0, 0)),
                  pl.BlockSpec((1, D_MODEL, D_EXPERT), lambda i, e: (e, 0, 0)),
                  pl.BlockSpec((1, D_EXPERT, D_MODEL), lambda i, e: (e, 0, 0))],
        out_specs=row(D_MODEL), out_shape=jax.ShapeDtypeStruct((n, D_MODEL), F32),
        compiler_params=_cparams(("arbitrary", "arbitrary")), name="moe",
    )(hn, h, cw, w_gate, w_up, w_down)


def kernel(x_prompt, x_sample, cache_k, cache_v, state_conv, g_norm_mix, w_in, g_q, g_k, w_conv,
           g_attn_out, g_conv_out, w_out, g_norm_ffn, w_router_group, w_router_expert,
           w_gate, w_up, w_down):
    depth = g_norm_mix.shape[0]
    assert depth == 1
    nb, seq, _ = x_prompt.shape
    db, dec_seq, _ = x_sample.shape
    assert dec_seq == 1
    l = 0

    gmix = g_norm_mix[l][None, :]
    gq = jnp.tile(g_q[l], N_HEADS)[None, :]
    gk = jnp.tile(g_k[l], N_HEADS)[None, :]
    ga = g_attn_out[l][None, :]
    gc = g_conv_out[l][None, :]
    gf = g_norm_ffn[l][None, :]
    w_router = jnp.concatenate(
        [w_router_expert[l], w_router_group[l],
         jnp.zeros((D_MODEL, LANES - N_EXPERTS - N_GROUPS), F32)], axis=1)
    w_router_hi = w_router.astype(BF16)
    w_router_split = jnp.concatenate(
        [w_router_hi, (w_router - w_router_hi.astype(F32)).astype(BF16)], axis=1)

    cos_p, sin_p = _rope_tables(np.arange(seq))
    xp = x_prompt.reshape(nb * seq, D_MODEL)
    q, k, v, bc, tail, kt, vt = _inproj(xp, gmix, w_in[l].astype(BF16), gq, gk, cos_p, sin_p,
                                        w_conv[l], tm=256, seq_len=seq, hi=False)
    attn = _attention(q.reshape(nb, seq, D_ATTN), k.reshape(nb, seq, D_ATTN),
                      v.reshape(nb, seq, D_ATTN))
    h, hn, cw = _outproj(attn.reshape(nb * seq, D_ATTN), bc, xp, ga, gc, w_out[l].astype(BF16),
                         gf, w_router_split, tm=256, hi=False)
    y_prompt = _moe(hn, h, cw, w_gate[l], w_up[l], w_down[l], tm=1024).reshape(nb, seq, D_MODEL)
    k_prompt = jnp.transpose(kt, (0, 3, 1, 2))
    v_prompt = jnp.transpose(vt, (0, 3, 1, 2))
    conv_prompt = tail.reshape(nb, SUBLANES, D_CONV)[:, SUBLANES - (CONV_WIDTH - 1):]

    cos_s, sin_s = _rope_tables(np.full((db,), PAST_LEN))
    xs = x_sample.reshape(db, D_MODEL)
    st = state_conv[l]
    qs, ks, vs, bcs, us = _inproj(xs, gmix, w_in[l], gq, gk, cos_s, sin_s, w_conv[l],
                                  tm=db, seq_len=db, hi=True, prev=(st[:, 1], st[:, 0]))
    cols = lambda t: t.reshape(db, N_HEADS, HEAD_DIM, 1)
    attn_s = _attention_sample(cols(qs), cols(ks), cols(vs),
                               jnp.transpose(cache_k[l], (0, 2, 3, 1)),
                               jnp.transpose(cache_v[l], (0, 2, 3, 1)))
    hs, hns, cws = _outproj(attn_s.reshape(db, D_ATTN), bcs, xs, ga, gc, w_out[l], gf, w_router,
                            tm=db, hi=True)
    y_sample = _moe(hns, hs, cws, w_gate[l], w_up[l], w_down[l], tm=db).reshape(db, 1, D_MODEL)
    k_sample = ks.reshape(db, 1, N_HEADS, HEAD_DIM)
    v_sample = vs.reshape(db, 1, N_HEADS, HEAD_DIM)
    conv_sample = jnp.stack([st[:, 1], us], axis=1)

    return (y_prompt, y_sample, k_prompt[None], v_prompt[None], conv_prompt[None],
            k_sample[None], v_sample[None], conv_sample[None])
```

```python
import functools

import numpy as np
import jax
import jax.numpy as jnp
from jax import lax
from jax.experimental import pallas as pl
from jax.experimental.pallas import tpu as pltpu

F32 = jnp.float32
BF16 = jnp.bfloat16

D_MODEL = 1024
HEAD_DIM = 64
N_HEADS = 12
D_ATTN = N_HEADS * HEAD_DIM
D_CONV = D_MODEL - D_ATTN
D_IN_PROJ = 3 * D_ATTN + 3 * D_CONV
CONV_WIDTH = 3
DILATIONS = (1, 4, 16)
ATT_BLOCK = 128
ATT_UNROLL = 8
MAX_WINDOW = 2048
PAST_LEN = 8192
ATTN_SCALE = HEAD_DIM ** -0.5
ROPE_THETA = 10000.0
EPS = 1e-6
N_GROUPS = 4
EXPERTS_PER_GROUP = 8
N_EXPERTS = N_GROUPS * EXPERTS_PER_GROUP
D_EXPERT = 256
MOE_TILE = 128

LANES = 128
SUBLANES = 8
MXU_DIM = 256
NEG = -1e30
VMEM_LIMIT = 48 * 1024 * 1024


def _cparams(sem):
    return pltpu.CompilerParams(dimension_semantics=sem, vmem_limit_bytes=VMEM_LIMIT)


def _mm(a, b, hi):
    if hi:
        return jnp.dot(a, b, preferred_element_type=F32, precision=lax.Precision.HIGHEST)
    return jnp.dot(a.astype(BF16), b.astype(BF16), preferred_element_type=F32)


def _rms(x, g):
    return x * lax.rsqrt(jnp.mean(x * x, axis=-1, keepdims=True) + EPS) * g


def _rope_tables(pos):
    half = HEAD_DIM // 2
    inv = ROPE_THETA ** (-np.arange(half, dtype=np.float64) / half)
    ang = np.asarray(pos, np.float64)[:, None] * inv[None, :]
    cos, sin = np.cos(ang), np.sin(ang)
    cos2 = np.concatenate([cos, cos, cos, cos], axis=-1)
    sin2 = np.concatenate([-sin, sin, -sin, sin], axis=-1)
    return jnp.asarray(cos2, F32), jnp.asarray(sin2, F32)


def _inproj_kernel(*refs, tm, tiles_per_seq, keep_tiles, hi, given_prev):
    if given_prev:
        (x_ref, gmix_ref, w_ref, gq_ref, gk_ref, cos_ref, sin_ref, wconv_ref, p1_ref, p2_ref,
         q_ref, k_ref, v_ref, bc_ref, u_ref) = refs
    else:
        (x_ref, gmix_ref, w_ref, gq_ref, gk_ref, cos_ref, sin_ref, wconv_ref,
         q_ref, k_ref, v_ref, bc_ref, tail_ref, kt_ref, vt_ref, ubuf) = refs

    xn = _rms(x_ref[...], gmix_ref[...])
    if not hi:
        xn = xn.astype(BF16)

    r_i = lax.broadcasted_iota(jnp.int32, (MXU_DIM, MXU_DIM), 0) // HEAD_DIM
    c_i = lax.broadcasted_iota(jnp.int32, (MXU_DIM, MXU_DIM), 1) // HEAD_DIM
    head_sum = jnp.where(r_i == c_i, 1.0, 0.0).astype(F32 if hi else BF16)

    reps = D_ATTN // LANES
    cos = jnp.concatenate([cos_ref[...]] * reps, axis=-1)
    sin = jnp.concatenate([sin_ref[...]] * reps, axis=-1)
    lane = lax.broadcasted_iota(jnp.int32, (1, D_ATTN), 1)
    first_half = (lane % HEAD_DIM) < (HEAD_DIM // 2)

    def qk(col0, g_ref):
        z = _mm(xn, w_ref[:, col0:col0 + D_ATTN], hi)
        sq = z * z
        ssq = jnp.concatenate(
            [_mm(sq[:, c * MXU_DIM:(c + 1) * MXU_DIM], head_sum, hi)
             for c in range(D_ATTN // MXU_DIM)], axis=-1)
        zn = z * lax.rsqrt(ssq * (1.0 / HEAD_DIM) + EPS) * g_ref[...]
        partner = jnp.where(first_half,
                            pltpu.roll(zn, D_ATTN - HEAD_DIM // 2, 1),
                            pltpu.roll(zn, HEAD_DIM // 2, 1))
        return zn * cos + partner * sin

    q_ref[...] = qk(0, gq_ref) * ATTN_SCALE
    k = qk(D_ATTN, gk_ref)
    v = _mm(xn, w_ref[:, 2 * D_ATTN:3 * D_ATTN], hi)
    k_ref[...] = k
    v_ref[...] = v
    if not given_prev:
        @pl.when(pl.program_id(0) % tiles_per_seq >= tiles_per_seq - keep_tiles)
        def _():
            kt_ref[0] = k.T.reshape(N_HEADS, HEAD_DIM, tm)
            vt_ref[0] = v.T.reshape(N_HEADS, HEAD_DIM, tm)
    c0 = 3 * D_ATTN
    b_gate = _mm(xn, w_ref[:, c0:c0 + D_CONV], hi)
    c_gate = _mm(xn, w_ref[:, c0 + D_CONV:c0 + 2 * D_CONV], hi)
    u_in = _mm(xn, w_ref[:, c0 + 2 * D_CONV:c0 + 3 * D_CONV], hi)
    u = c_gate * u_in

    if given_prev:
        u1, u2 = p1_ref[...], p2_ref[...]
        u_ref[...] = u
    else:
        i = pl.program_id(0)

        @pl.when(i % tiles_per_seq == 0)
        def _():
            ubuf[0:SUBLANES, :] = jnp.zeros((SUBLANES, D_CONV), F32)

        @pl.when(i % tiles_per_seq != 0)
        def _():
            ubuf[0:SUBLANES, :] = ubuf[tm:tm + SUBLANES, :]

        ubuf[SUBLANES:tm + SUBLANES, :] = u
        u1 = ubuf[SUBLANES - 1:tm + SUBLANES - 1, :]
        u2 = ubuf[SUBLANES - 2:tm + SUBLANES - 2, :]
        tail_ref[...] = ubuf[tm:tm + SUBLANES, :]

    wc = wconv_ref[...]
    conv = wc[0:1, :] * u2 + wc[1:2, :] * u1 + wc[2:3, :] * u
    bc_ref[...] = b_gate * conv


def _inproj(x, gmix, w_in, gq, gk, cos, sin, w_conv, *, tm, seq_len, hi, prev=None):
    n = x.shape[0]
    nt = n // tm
    tiles_per_seq = seq_len // tm
    row = lambda w: pl.BlockSpec((tm, w), lambda i: (i, 0))
    full = lambda a: pl.BlockSpec(a.shape, lambda i: (0,) * a.ndim)
    tab = pl.BlockSpec((tm, LANES), lambda i: (i % tiles_per_seq, 0))
    in_specs = [row(D_MODEL), full(gmix), full(w_in), full(gq), full(gk), tab, tab, full(w_conv)]
    args = [x, gmix, w_in, gq, gk, cos, sin, w_conv]
    out_shape = [jax.ShapeDtypeStruct((n, D_ATTN), F32)] * 3 + [jax.ShapeDtypeStruct((n, D_CONV), F32)]
    out_specs = [row(D_ATTN)] * 3 + [row(D_CONV)]
    scratch = []
    keep_tiles = min(MAX_WINDOW, seq_len) // tm
    if prev is not None:
        in_specs += [row(D_CONV), row(D_CONV)]
        args += list(prev)
        out_shape.append(jax.ShapeDtypeStruct((n, D_CONV), F32))
        out_specs.append(row(D_CONV))
    else:
        n_seq = n // seq_len
        out_shape.append(jax.ShapeDtypeStruct((n_seq * SUBLANES, D_CONV), F32))
        out_specs.append(pl.BlockSpec((SUBLANES, D_CONV), lambda i: (i // tiles_per_seq, 0)))
        kept = pl.BlockSpec(
            (1, N_HEADS, HEAD_DIM, tm),
            lambda i: (i // tiles_per_seq, 0, 0,
                       jnp.maximum(i % tiles_per_seq - (tiles_per_seq - keep_tiles), 0)))
        out_shape += [jax.ShapeDtypeStruct((n_seq, N_HEADS, HEAD_DIM, keep_tiles * tm), F32)] * 2
        out_specs += [kept, kept]
        scratch.append(pltpu.VMEM((tm + SUBLANES, D_CONV), F32))
    return pl.pallas_call(
        functools.partial(_inproj_kernel, tm=tm, tiles_per_seq=tiles_per_seq,
                          keep_tiles=keep_tiles, hi=hi, given_prev=prev is not None),
        grid=(nt,), in_specs=in_specs, out_specs=out_specs, out_shape=out_shape,
        scratch_shapes=scratch, compiler_params=_cparams(("arbitrary",)),
        name="inproj_hi" if hi else "inproj",
    )(*args)


def _attn_kernel(q_ref, k_ref, v_ref, o_ref, num_ref, den_ref, m_ref, bias_ref, *, seq_len):
    blk = ATT_BLOCK
    qi = lax.broadcasted_iota(jnp.int32, (blk, 2 * blk), 0)
    kj = lax.broadcasted_iota(jnp.int32, (blk, 2 * blk), 1)
    bias_ref[...] = jnp.where((kj >= qi) & (kj <= qi + blk), 0.0, NEG)
    head0 = lax.broadcasted_iota(jnp.int32, (1, LANES), 1) < HEAD_DIM
    dn = (((1,), (1,)), ((), ()))

    def rows(start, size, d):
        return pl.ds(start, size) if d == 1 else pl.ds(start, size, stride=d)

    def group(d, r, n0, u, first, mode):
        q_rows = rows(n0 * (blk * d) + r, u * blk, d)
        if first:
            k_rows = rows(r, u * blk, d)
        else:
            k_rows = rows((n0 - 1) * (blk * d) + r, (u + 1) * blk, d)
        qa = q_ref[0, q_rows, :]
        ka = k_ref[0, k_rows, :].astype(BF16)
        va = v_ref[0, k_rows, :]
        qa0 = jnp.where(head0, qa, 0.0).astype(BF16)
        qa1 = jnp.where(head0, 0.0, qa).astype(BF16)
        va0 = jnp.where(head0, va, 1.0).astype(BF16)
        va1 = jnp.where(head0, 1.0, va).astype(BF16)
        if mode != "init":
            m_old = m_ref[q_rows, :]
            num_old = num_ref[q_rows, :]
            den_old = den_ref[q_rows, :]
        key_slices, scores = [], []
        for i in range(u):
            qs = slice(i * blk, (i + 1) * blk)
            if first and i == 0:
                ks = slice(0, blk)
                bias = bias_ref[:, blk:2 * blk]
            else:
                k0 = (i - 1) * blk if first else i * blk
                ks = slice(k0, k0 + 2 * blk)
                bias = bias_ref[...]
            key_slices.append(ks)
            scores.append(
                (lax.dot_general(qa0[qs], ka[ks], dn, preferred_element_type=F32) + bias,
                 lax.dot_general(qa1[qs], ka[ks], dn, preferred_element_type=F32) + bias))
        probs, ms = [], []
        for s0, s1 in scores:
            m0 = jnp.max(s0, axis=-1, keepdims=True)
            m1 = jnp.max(s1, axis=-1, keepdims=True)
            probs.append((jnp.exp(s0 - m0).astype(BF16), jnp.exp(s1 - m1).astype(BF16)))
            ms.append(jnp.where(head0, m0, m1))
        nums, dens = [], []
        for (p0, p1), ks in zip(probs, key_slices):
            r0 = jnp.dot(p0, va0[ks], preferred_element_type=F32)
            r1 = jnp.dot(p1, va1[ks], preferred_element_type=F32)
            nums.append(jnp.where(head0, r0, r1))
            dens.append(pltpu.roll(jnp.where(head0, r1, r0), HEAD_DIM, 1))
        num = jnp.concatenate(nums, axis=0)
        den = jnp.concatenate(dens, axis=0)
        m = jnp.concatenate(ms, axis=0)
        if mode == "init":
            return [(num_ref, q_rows, num), (den_ref, q_rows, den), (m_ref, q_rows, m)]
        m_new = jnp.maximum(m_old, m)
        a = jnp.exp(m_old - m_new)
        b = jnp.exp(m - m_new)
        num = num_old * a + num * b
        den = den_old * a + den * b
        if mode == "merge":
            return [(num_ref, q_rows, num), (den_ref, q_rows, den), (m_ref, q_rows, m_new)]
        return [(o_ref.at[0], q_rows, num / den)]

    def run(groups):
        stores = [s for g in groups for s in group(*g)]
        for ref, at, val in stores:
            ref[at, :] = val

    for d, mode in zip(DILATIONS, ("init", "merge", "final")):
        nb = seq_len // (d * blk)
        u = min(ATT_UNROLL, nb)
        if nb == u:
            per_body = ATT_UNROLL // u

            def residues(rr, carry, d=d, mode=mode, u=u, per_body=per_body):
                run([(d, rr * per_body + t, 0, u, True, mode) for t in range(per_body)])
                return carry

            lax.fori_loop(0, d // per_body, residues, 0)
        else:
            def per_residue(r, carry, d=d, mode=mode, u=u, nb=nb):
                run([(d, r, 0, u, True, mode)])

                def per_block(g, c):
                    run([(d, r, g * u, u, False, mode)])
                    return c

                lax.fori_loop(1, nb // u, per_block, 0)
                return carry

            lax.fori_loop(0, d, per_residue, 0)


def _attention(q, k, v):
    b, s, _ = q.shape
    spec = pl.BlockSpec((1, s, LANES), lambda i, p: (i, 0, p))
    return pl.pallas_call(
        functools.partial(_attn_kernel, seq_len=s),
        grid=(b, D_ATTN // LANES), in_specs=[spec] * 3, out_specs=spec,
        out_shape=jax.ShapeDtypeStruct((b, s, D_ATTN), F32),
        scratch_shapes=[pltpu.VMEM((s, LANES), F32)] * 3 + [pltpu.VMEM((ATT_BLOCK, 2 * ATT_BLOCK), F32)],
        compiler_params=_cparams(("arbitrary", "arbitrary")), name="attn_prompt",
    )(q, k, v)


def _attn_sample_kernel(q_ref, kn_ref, vn_ref, kc_ref, vc_ref, o_ref, *, w_buf, heads):
    age = w_buf - lax.broadcasted_iota(jnp.int32, (1, w_buf), 1)
    valid = [(age % d == 0) & (age <= d * ATT_BLOCK) for d in DILATIONS]
    for h in range(heads):
        q = q_ref[0, h]
        s = jnp.sum(kc_ref[0, h] * q, axis=0, keepdims=True)
        s_self = jnp.sum(kn_ref[0, h] * q, axis=0, keepdims=True)
        parts = []
        for ok in valid:
            sd = jnp.where(ok, s, NEG)
            m = jnp.maximum(jnp.max(sd, axis=1, keepdims=True), s_self)
            p = jnp.exp(sd - m)
            p_self = jnp.exp(s_self - m)
            parts.append((p, p_self, jnp.sum(p, axis=1, keepdims=True) + p_self, m))
        m_all = jnp.maximum(jnp.maximum(parts[0][3], parts[1][3]), parts[2][3])
        w = jnp.zeros((1, w_buf), F32)
        w_self = jnp.zeros((1, 1), F32)
        den_t = jnp.zeros((1, 1), F32)
        for p, p_self, den, m in parts:
            c = jnp.exp(m - m_all)
            w = w + p * c
            w_self = w_self + p_self * c
            den_t = den_t + den * c
        num_t = jnp.sum(vc_ref[0, h] * w, axis=1, keepdims=True) + vn_ref[0, h] * w_self
        o_ref[0, h] = num_t / den_t


def _attention_sample(q, kn, vn, cache_kt, cache_vt, *, heads=6):
    db, _, _, w_buf = cache_kt.shape
    tok = pl.BlockSpec((1, heads, HEAD_DIM, 1), lambda i, g: (i, g, 0, 0))
    cache = pl.BlockSpec((1, heads, HEAD_DIM, w_buf), lambda i, g: (i, g, 0, 0))
    return pl.pallas_call(
        functools.partial(_attn_sample_kernel, w_buf=w_buf, heads=heads),
        grid=(db, N_HEADS // heads), in_specs=[tok, tok, tok, cache, cache], out_specs=tok,
        out_shape=jax.ShapeDtypeStruct((db, N_HEADS, HEAD_DIM, 1), F32),
        compiler_params=_cparams(("arbitrary", "arbitrary")), name="attn_sample",
    )(q, kn, vn, cache_kt, cache_vt)


def _outproj_kernel(attn_ref, bc_ref, x_ref, ga_ref, gc_ref, wo_ref, gf_ref, wr_ref, *rest,
                    hi, n_tiles):
    hx_ref = rest[-1]
    a = _rms(attn_ref[...], ga_ref[...])
    c = _rms(bc_ref[...], gc_ref[...])
    mix = _mm(a, wo_ref[0:D_ATTN, :], hi) + _mm(c, wo_ref[D_ATTN:D_MODEL, :], hi)
    h = x_ref[...] + mix
    hx_ref[:, 0:D_MODEL] = h
    hn = _rms(h, gf_ref[...])

    if hi:
        lg = _mm(hn, wr_ref[...], True)
    else:
        hn_hi = hn.astype(BF16)
        hn_lo = (hn - hn_hi.astype(F32)).astype(BF16)
        both = jnp.dot(hn_hi, wr_ref[...], preferred_element_type=F32)
        lg = (both[:, :LANES] + both[:, LANES:]
              + jnp.dot(hn_lo, wr_ref[:, :LANES], preferred_element_type=F32))
    lane_i = lax.broadcasted_iota(jnp.int32, lg.shape, 1)
    lane = lane_i.astype(F32)
    lane_group = (lane_i // EXPERTS_PER_GROUP).astype(F32)
    is_e = lane_i < N_EXPERTS
    is_g = (lane_i >= N_EXPERTS) & (lane_i < N_EXPERTS + N_GROUPS)
    big = float(LANES)
    first_at = lambda t, v: jnp.min(jnp.where(t == v, lane, big), axis=-1, keepdims=True)

    gl = jnp.where(is_g, lg, NEG)
    mg = jnp.max(gl, axis=-1, keepdims=True)
    p_g = 1.0 / jnp.sum(jnp.exp(gl - mg), axis=-1, keepdims=True)
    g_idx = first_at(gl, mg) - float(N_EXPERTS)
    el = jnp.where(is_e & (lane_group == g_idx), lg, NEG)
    v1 = jnp.max(el, axis=-1, keepdims=True)
    i1 = first_at(el, v1)
    el2 = jnp.where(lane == i1, NEG, el)
    v2 = jnp.max(el2, axis=-1, keepdims=True)
    i2 = first_at(el2, v2)
    e2 = jnp.exp(v2 - v1)
    w1 = p_g / (1.0 + e2)
    w2 = p_g * e2 / (1.0 + e2)
    route = [i1, i2, w1, w2]
    hx_ref[:, D_MODEL:D_MODEL + LANES] = sum(
        jnp.where(lane_i == j, val, 0.0) for j, val in enumerate(route))

    @pl.when(pl.program_id(0) >= n_tiles)
    def _():
        hx_ref[...] = jnp.zeros(hx_ref.shape, F32)


def _outproj(attn, bc, x, ga, gc, w_out, gf, w_router, *, tm, hi, n_total, into=None):
    n = x.shape[0]
    n_tiles = n // tm
    row0 = 0 if into is None else n_total - n
    n_fill = 0 if into is not None else -(-(n_total - n) // tm)
    row = lambda w: pl.BlockSpec((tm, w), lambda i: (jnp.minimum(i, n_tiles - 1), 0))
    full = lambda a: pl.BlockSpec(a.shape, lambda i: (0,) * a.ndim)
    in_specs = [row(D_ATTN), row(D_CONV), row(D_MODEL), full(ga), full(gc), full(w_out),
                full(gf), full(w_router)]
    args = [attn, bc, x, ga, gc, w_out, gf, w_router]
    aliases = {}
    if into is not None:
        in_specs.append(pl.BlockSpec(memory_space=pl.ANY))
        args.append(into)
        aliases = {len(args) - 1: 0}
    return pl.pallas_call(
        functools.partial(_outproj_kernel, hi=hi, n_tiles=n_tiles),
        grid=(n_tiles + n_fill,), in_specs=in_specs,
        out_specs=pl.BlockSpec((tm, D_MODEL + LANES), lambda i: (i + row0 // tm, 0)),
        out_shape=jax.ShapeDtypeStruct((n_total, D_MODEL + LANES), F32),
        input_output_aliases=aliases,
        compiler_params=_cparams(("arbitrary",)), name="outproj_hi" if hi else "outproj",
    )(*args)


def _route_plan(route, n_slots):
    n = route.shape[0]
    i1 = route[:, 0].astype(jnp.int32)
    i2 = route[:, 1].astype(jnp.int32)
    n_cls = N_GROUPS * EXPERTS_PER_GROUP * EXPERTS_PER_GROUP
    cls = (i1 // EXPERTS_PER_GROUP) * (EXPERTS_PER_GROUP * EXPERTS_PER_GROUP) \
        + (jnp.minimum(i1, i2) % EXPERTS_PER_GROUP) * EXPERTS_PER_GROUP \
        + jnp.maximum(i1, i2) % EXPERTS_PER_GROUP
    order = jnp.argsort(cls, stable=True).astype(jnp.int32)
    cls_sorted = cls[order]
    counts = jnp.zeros((n_cls,), jnp.int32).at[cls].add(1)
    tiles = (counts + MOE_TILE - 1) // MOE_TILE
    tile_end = jnp.cumsum(tiles)
    tile_off = tile_end - tiles
    tok_off = jnp.cumsum(counts) - counts
    slot_sorted = tile_off[cls_sorted] * MOE_TILE + jnp.arange(n, dtype=jnp.int32) - tok_off[cls_sorted]
    slot_token = jnp.zeros((n_slots,), jnp.int32).at[slot_sorted].set(order)
    token_slot = jnp.zeros((n,), jnp.int32).at[order].set(slot_sorted)
    n_tiles = tile_end[-1]
    t = jnp.arange(n_slots // MOE_TILE, dtype=jnp.int32)
    tile_cls = jnp.searchsorted(tile_end, jnp.minimum(t, n_tiles - 1), side="right").astype(jnp.int32)
    pair = EXPERTS_PER_GROUP * EXPERTS_PER_GROUP
    return (tile_cls // pair, (tile_cls % pair) // EXPERTS_PER_GROUP, tile_cls % EXPERTS_PER_GROUP,
            n_tiles.reshape(1), slot_token, token_slot)


def _moe_kernel(tg_ref, ta_ref, tb_ref, nt_ref, tok_ref,
                hx_hbm, gf_ref, wg_ref, wu_ref, wd_ref, y_ref,
                xbuf, wgu_s, wd_s, sem):
    t = pl.program_id(0)
    n_tiles = nt_ref[0]
    buf = t % 2

    def row_copy(tile, j, b):
        tok = tok_ref[tile * MOE_TILE + j]
        return pltpu.make_async_copy(hx_hbm.at[pl.ds(tok, 1), :], xbuf.at[b, pl.ds(j, 1), :],
                                     sem.at[b])

    def gather(tile, b):
        for j in range(MOE_TILE):
            row_copy(tile, j, b).start()

    @pl.when(t == 0)
    def _():
        gather(0, 0)

    @pl.when(t + 1 < n_tiles)
    def _():
        gather(t + 1, 1 - buf)

    @pl.when(t < n_tiles)
    def _():
        grp = tg_ref[t]

        @pl.when((t == 0) | (grp != tg_ref[jnp.maximum(t - 1, 0)]))
        def _():
            for e in range(EXPERTS_PER_GROUP):
                wgu_s[e, :, 0:D_EXPERT] = wg_ref[0, e].astype(BF16)
                wgu_s[e, :, D_EXPERT:2 * D_EXPERT] = wu_ref[0, e].astype(BF16)
                wd_s[e] = wd_ref[0, e].astype(BF16)

        pltpu.make_async_copy(hx_hbm.at[pl.ds(0, MOE_TILE), :], xbuf.at[buf], sem.at[buf]).wait()
        x = xbuf[buf]
        h = x[:, 0:D_MODEL]
        route = x[:, D_MODEL:D_MODEL + LANES]
        i1, w1, w2 = route[:, 0:1], route[:, 2:3], route[:, 3:4]
        xn = _rms(h, gf_ref[...]).astype(BF16)
        ea, eb = ta_ref[t], tb_ref[t]
        first_is_a = i1 == (grp * EXPERTS_PER_GROUP + ea).astype(F32)

        def expert(e):
            gu = jnp.dot(xn, wgu_s[e], preferred_element_type=F32)
            act = (jax.nn.silu(gu[:, 0:D_EXPERT]) * gu[:, D_EXPERT:2 * D_EXPERT]).astype(BF16)
            return jnp.dot(act, wd_s[e], preferred_element_type=F32)

        y_ref[...] = (h + jnp.where(first_is_a, w1, w2) * expert(ea)
                      + jnp.where(first_is_a, w2, w1) * expert(eb))

    @pl.when(t >= n_tiles)
    def _():
        y_ref[...] = jnp.zeros(y_ref.shape, F32)


def _moe(hx, gf, w_gate, w_up, w_down, plan):
    tile_g, tile_a, tile_b, n_tiles, slot_token, _ = plan
    n_slots = slot_token.shape[0]
    grouped = lambda w: w.reshape(N_GROUPS, EXPERTS_PER_GROUP, *w.shape[1:])
    wspec = lambda r, c: pl.BlockSpec((1, EXPERTS_PER_GROUP, r, c),
                                      lambda t, tg, ta, tb, nt, tok: (tg[t], 0, 0, 0),
                                      pipeline_mode=pl.Buffered(1))
    grid_spec = pltpu.PrefetchScalarGridSpec(
        num_scalar_prefetch=5, grid=(n_slots // MOE_TILE,),
        in_specs=[pl.BlockSpec(memory_space=pl.ANY),
                  pl.BlockSpec(gf.shape, lambda t, *_: (0, 0)),
                  wspec(D_MODEL, D_EXPERT), wspec(D_MODEL, D_EXPERT), wspec(D_EXPERT, D_MODEL)],
        out_specs=pl.BlockSpec((MOE_TILE, D_MODEL), lambda t, *_: (t, 0)),
        scratch_shapes=[pltpu.VMEM((2, MOE_TILE, D_MODEL + LANES), F32),
                        pltpu.VMEM((EXPERTS_PER_GROUP, D_MODEL, 2 * D_EXPERT), BF16),
                        pltpu.VMEM((EXPERTS_PER_GROUP, D_EXPERT, D_MODEL), BF16),
                        pltpu.SemaphoreType.DMA((2,))])
    return pl.pallas_call(
        _moe_kernel, grid_spec=grid_spec,
        out_shape=jax.ShapeDtypeStruct((n_slots, D_MODEL), F32),
        compiler_params=_cparams(("arbitrary",)), name="moe",
    )(tile_g, tile_a, tile_b, n_tiles, slot_token, hx, gf,
      grouped(w_gate), grouped(w_up), grouped(w_down))


def _unsort_kernel(slot_ref, ys_hbm, y_hbm, sem, *, rows, row0):
    base = pl.program_id(0) * rows
    for j in range(rows):
        slot = slot_ref[row0 + base + j]
        pltpu.make_async_copy(ys_hbm.at[pl.ds(slot, 1), :], y_hbm.at[pl.ds(base + j, 1), :],
                              sem).start()
    pltpu.make_async_copy(ys_hbm.at[pl.ds(0, rows), :], y_hbm.at[pl.ds(base, rows), :], sem).wait()


def _unsort(ys, token_slot, *, n, row0, rows):
    grid_spec = pltpu.PrefetchScalarGridSpec(
        num_scalar_prefetch=1, grid=(n // rows,),
        in_specs=[pl.BlockSpec(memory_space=pl.ANY)],
        out_specs=pl.BlockSpec(memory_space=pl.ANY),
        scratch_shapes=[pltpu.SemaphoreType.DMA(())])
    return pl.pallas_call(
        functools.partial(_unsort_kernel, rows=rows, row0=row0), grid_spec=grid_spec,
        out_shape=jax.ShapeDtypeStruct((n, D_MODEL), F32),
        compiler_params=_cparams(("arbitrary",)), name="unsort",
    )(token_slot, ys)


def kernel(x_prompt, x_sample, cache_k, cache_v, state_conv, g_norm_mix, w_in, g_q, g_k, w_conv,
           g_attn_out, g_conv_out, w_out, g_norm_ffn, w_router_group, w_router_expert,
           w_gate, w_up, w_down):
    depth = g_norm_mix.shape[0]
    assert depth == 1
    nb, seq, _ = x_prompt.shape
    db, dec_seq, _ = x_sample.shape
    assert dec_seq == 1
    l = 0

    gmix = g_norm_mix[l][None, :]
    gq = jnp.tile(g_q[l], N_HEADS)[None, :]
    gk = jnp.tile(g_k[l], N_HEADS)[None, :]
    ga = g_attn_out[l][None, :]
    gc = g_conv_out[l][None, :]
    gf = g_norm_ffn[l][None, :]
    w_router = jnp.concatenate(
        [w_router_expert[l], w_router_group[l],
         jnp.zeros((D_MODEL, LANES - N_EXPERTS - N_GROUPS), F32)], axis=1)
    w_router_hi = w_router.astype(BF16)
    w_router_split = jnp.concatenate(
        [w_router_hi, (w_router - w_router_hi.astype(F32)).astype(BF16)], axis=1)

    cos_p, sin_p = _rope_tables(np.arange(seq))
    xp = x_prompt.reshape(nb * seq, D_MODEL)
    q, k, v, bc, tail, kt, vt = _inproj(xp, gmix, w_in[l].astype(BF16), gq, gk, cos_p, sin_p,
                                        w_conv[l], tm=256, seq_len=seq, hi=False)
    attn = _attention(q.reshape(nb, seq, D_ATTN), k.reshape(nb, seq, D_ATTN),
                      v.reshape(nb, seq, D_ATTN))
    n_prompt = nb * seq
    n_total = n_prompt + db
    hx = _outproj(attn.reshape(n_prompt, D_ATTN), bc, xp, ga, gc, w_out[l].astype(BF16),
                  gf, w_router_split, tm=256, hi=False, n_total=n_total)
    k_prompt = jnp.transpose(kt, (0, 3, 1, 2))
    v_prompt = jnp.transpose(vt, (0, 3, 1, 2))
    conv_prompt = tail.reshape(nb, SUBLANES, D_CONV)[:, SUBLANES - (CONV_WIDTH - 1):]

    cos_s, sin_s = _rope_tables(np.full((db,), PAST_LEN))
    xs = x_sample.reshape(db, D_MODEL)
    st = state_conv[l]
    qs, ks, vs, bcs, us = _inproj(xs, gmix, w_in[l], gq, gk, cos_s, sin_s, w_conv[l],
                                  tm=db, seq_len=db, hi=True, prev=(st[:, 1], st[:, 0]))
    cols = lambda t: t.reshape(db, N_HEADS, HEAD_DIM, 1)
    attn_s = _attention_sample(cols(qs), cols(ks), cols(vs),
                               jnp.transpose(cache_k[l], (0, 2, 3, 1)),
                               jnp.transpose(cache_v[l], (0, 2, 3, 1)))
    hx = _outproj(attn_s.reshape(db, D_ATTN), bcs, xs, ga, gc, w_out[l], gf, w_router,
                  tm=db, hi=True, n_total=n_total, into=hx)

    n_cls = N_GROUPS * EXPERTS_PER_GROUP * (EXPERTS_PER_GROUP - 1) // 2
    n_slots = -(-(n_total + n_cls * (MOE_TILE - 1)) // MOE_TILE) * MOE_TILE
    plan = _route_plan(hx[:, D_MODEL:D_MODEL + 2], n_slots)
    ys = _moe(hx, gf, w_gate[l], w_up[l], w_down[l], plan)
    y_prompt = _unsort(ys, plan[5], n=n_prompt, row0=0, rows=256).reshape(nb, seq, D_MODEL)
    y_sample = _unsort(ys, plan[5], n=db, row0=n_prompt, rows=db).reshape(db, 1, D_MODEL)
    k_sample = ks.reshape(db, 1, N_HEADS, HEAD_DIM)
    v_sample = vs.reshape(db, 1, N_HEADS, HEAD_DIM)
    conv_sample = jnp.stack([st[:, 1], us], axis=1)

    return (y_prompt, y_sample, k_prompt[None], v_prompt[None], conv_prompt[None],
            k_sample[None], v_sample[None], conv_sample[None])
```

```python
import functools

import numpy as np
import jax
import jax.numpy as jnp
from jax import lax
from jax.experimental import pallas as pl
from jax.experimental.pallas import tpu as pltpu

F32 = jnp.float32
BF16 = jnp.bfloat16

D_MODEL = 1024
HEAD_DIM = 64
N_HEADS = 12
D_ATTN = N_HEADS * HEAD_DIM
D_CONV = D_MODEL - D_ATTN
D_IN_PROJ = 3 * D_ATTN + 3 * D_CONV
CONV_WIDTH = 3
DILATIONS = (1, 4, 16)
ATT_BLOCK = 128
ATT_UNROLL = 8
MAX_WINDOW = 2048
PAST_LEN = 8192
ATTN_SCALE = HEAD_DIM ** -0.5
ROPE_THETA = 10000.0
EPS = 1e-6
N_GROUPS = 4
EXPERTS_PER_GROUP = 8
N_EXPERTS = N_GROUPS * EXPERTS_PER_GROUP
D_EXPERT = 256
MOE_TILE = 128
N_CLASSES = N_GROUPS * EXPERTS_PER_GROUP * EXPERTS_PER_GROUP
TOKEN_ROWS = 16
PLAN_ROWS = 640
TOKEN_PAD = 256

LANES = 128
SUBLANES = 8
MXU_DIM = 256
NEG = -1e30
VMEM_LIMIT = 48 * 1024 * 1024


def _cparams(sem):
    return pltpu.CompilerParams(dimension_semantics=sem, vmem_limit_bytes=VMEM_LIMIT)


def _mm(a, b, hi):
    if hi:
        return jnp.dot(a, b, preferred_element_type=F32, precision=lax.Precision.HIGHEST)
    return jnp.dot(a.astype(BF16), b.astype(BF16), preferred_element_type=F32)


def _rms(x, g):
    return x * lax.rsqrt(jnp.mean(x * x, axis=-1, keepdims=True) + EPS) * g


def _rope_tables(pos):
    half = HEAD_DIM // 2
    inv = ROPE_THETA ** (-np.arange(half, dtype=np.float64) / half)
    ang = np.asarray(pos, np.float64)[:, None] * inv[None, :]
    cos, sin = np.cos(ang), np.sin(ang)
    cos2 = np.concatenate([cos, cos, cos, cos], axis=-1)
    sin2 = np.concatenate([-sin, sin, -sin, sin], axis=-1)
    return jnp.asarray(cos2, F32), jnp.asarray(sin2, F32)


def _inproj_kernel(*refs, tm, tiles_per_seq, keep_tiles, hi, given_prev):
    if given_prev:
        (x_ref, gmix_ref, w_ref, gq_ref, gk_ref, cos_ref, sin_ref, wconv_ref, p1_ref, p2_ref,
         q_ref, k_ref, v_ref, bc_ref, u_ref) = refs
    else:
        (x_ref, gmix_ref, w_ref, gq_ref, gk_ref, cos_ref, sin_ref, wconv_ref,
         q_ref, k_ref, v_ref, bc_ref, tail_ref, kt_ref, vt_ref, ubuf) = refs

    xn = _rms(x_ref[...], gmix_ref[...])
    if not hi:
        xn = xn.astype(BF16)

    r_i = lax.broadcasted_iota(jnp.int32, (MXU_DIM, MXU_DIM), 0) // HEAD_DIM
    c_i = lax.broadcasted_iota(jnp.int32, (MXU_DIM, MXU_DIM), 1) // HEAD_DIM
    head_sum = jnp.where(r_i == c_i, 1.0, 0.0).astype(F32 if hi else BF16)

    reps = D_ATTN // LANES
    cos = jnp.concatenate([cos_ref[...]] * reps, axis=-1)
    sin = jnp.concatenate([sin_ref[...]] * reps, axis=-1)
    lane = lax.broadcasted_iota(jnp.int32, (1, D_ATTN), 1)
    first_half = (lane % HEAD_DIM) < (HEAD_DIM // 2)

    def qk(col0, g_ref):
        z = _mm(xn, w_ref[:, col0:col0 + D_ATTN], hi)
        sq = z * z
        ssq = jnp.concatenate(
            [_mm(sq[:, c * MXU_DIM:(c + 1) * MXU_DIM], head_sum, hi)
             for c in range(D_ATTN // MXU_DIM)], axis=-1)
        zn = z * lax.rsqrt(ssq * (1.0 / HEAD_DIM) + EPS) * g_ref[...]
        partner = jnp.where(first_half,
                            pltpu.roll(zn, D_ATTN - HEAD_DIM // 2, 1),
                            pltpu.roll(zn, HEAD_DIM // 2, 1))
        return zn * cos + partner * sin

    q_ref[...] = qk(0, gq_ref) * ATTN_SCALE
    k = qk(D_ATTN, gk_ref)
    v = _mm(xn, w_ref[:, 2 * D_ATTN:3 * D_ATTN], hi)
    k_ref[...] = k
    v_ref[...] = v
    if not given_prev:
        @pl.when(pl.program_id(0) % tiles_per_seq >= tiles_per_seq - keep_tiles)
        def _():
            kt_ref[0] = k.T.reshape(N_HEADS, HEAD_DIM, tm)
            vt_ref[0] = v.T.reshape(N_HEADS, HEAD_DIM, tm)
    c0 = 3 * D_ATTN
    b_gate = _mm(xn, w_ref[:, c0:c0 + D_CONV], hi)
    c_gate = _mm(xn, w_ref[:, c0 + D_CONV:c0 + 2 * D_CONV], hi)
    u_in = _mm(xn, w_ref[:, c0 + 2 * D_CONV:c0 + 3 * D_CONV], hi)
    u = c_gate * u_in

    if given_prev:
        u1, u2 = p1_ref[...], p2_ref[...]
        u_ref[...] = u
    else:
        i = pl.program_id(0)

        @pl.when(i % tiles_per_seq == 0)
        def _():
            ubuf[0:SUBLANES, :] = jnp.zeros((SUBLANES, D_CONV), F32)

        @pl.when(i % tiles_per_seq != 0)
        def _():
            ubuf[0:SUBLANES, :] = ubuf[tm:tm + SUBLANES, :]

        ubuf[SUBLANES:tm + SUBLANES, :] = u
        u1 = ubuf[SUBLANES - 1:tm + SUBLANES - 1, :]
        u2 = ubuf[SUBLANES - 2:tm + SUBLANES - 2, :]
        tail_ref[...] = ubuf[tm:tm + SUBLANES, :]

    wc = wconv_ref[...]
    conv = wc[0:1, :] * u2 + wc[1:2, :] * u1 + wc[2:3, :] * u
    bc_ref[...] = b_gate * conv


def _inproj(x, gmix, w_in, gq, gk, cos, sin, w_conv, *, tm, seq_len, hi, prev=None):
    n = x.shape[0]
    nt = n // tm
    tiles_per_seq = seq_len // tm
    row = lambda w: pl.BlockSpec((tm, w), lambda i: (i, 0))
    full = lambda a: pl.BlockSpec(a.shape, lambda i: (0,) * a.ndim)
    tab = pl.BlockSpec((tm, LANES), lambda i: (i % tiles_per_seq, 0))
    in_specs = [row(D_MODEL), full(gmix), full(w_in), full(gq), full(gk), tab, tab, full(w_conv)]
    args = [x, gmix, w_in, gq, gk, cos, sin, w_conv]
    out_shape = [jax.ShapeDtypeStruct((n, D_ATTN), F32)] * 3 + [jax.ShapeDtypeStruct((n, D_CONV), F32)]
    out_specs = [row(D_ATTN)] * 3 + [row(D_CONV)]
    scratch = []
    keep_tiles = min(MAX_WINDOW, seq_len) // tm
    if prev is not None:
        in_specs += [row(D_CONV), row(D_CONV)]
        args += list(prev)
        out_shape.append(jax.ShapeDtypeStruct((n, D_CONV), F32))
        out_specs.append(row(D_CONV))
    else:
        n_seq = n // seq_len
        out_shape.append(jax.ShapeDtypeStruct((n_seq * SUBLANES, D_CONV), F32))
        out_specs.append(pl.BlockSpec((SUBLANES, D_CONV), lambda i: (i // tiles_per_seq, 0)))
        kept = pl.BlockSpec(
            (1, N_HEADS, HEAD_DIM, tm),
            lambda i: (i // tiles_per_seq, 0, 0,
                       jnp.maximum(i % tiles_per_seq - (tiles_per_seq - keep_tiles), 0)))
        out_shape += [jax.ShapeDtypeStruct((n_seq, N_HEADS, HEAD_DIM, keep_tiles * tm), F32)] * 2
        out_specs += [kept, kept]
        scratch.append(pltpu.VMEM((tm + SUBLANES, D_CONV), F32))
    return pl.pallas_call(
        functools.partial(_inproj_kernel, tm=tm, tiles_per_seq=tiles_per_seq,
                          keep_tiles=keep_tiles, hi=hi, given_prev=prev is not None),
        grid=(nt,), in_specs=in_specs, out_specs=out_specs, out_shape=out_shape,
        scratch_shapes=scratch, compiler_params=_cparams(("arbitrary",)),
        name="inproj_hi" if hi else "inproj",
    )(*args)


def _attn_kernel(q_ref, k_ref, v_ref, o_ref, num_ref, den_ref, m_ref, bias_ref, *, seq_len):
    blk = ATT_BLOCK
    qi = lax.broadcasted_iota(jnp.int32, (blk, 2 * blk), 0)
    kj = lax.broadcasted_iota(jnp.int32, (blk, 2 * blk), 1)
    bias_ref[...] = jnp.where((kj >= qi) & (kj <= qi + blk), 0.0, NEG)
    head0 = lax.broadcasted_iota(jnp.int32, (1, LANES), 1) < HEAD_DIM
    dn = (((1,), (1,)), ((), ()))

    def rows(start, size, d):
        return pl.ds(start, size) if d == 1 else pl.ds(start, size, stride=d)

    def group(d, r, n0, u, first, mode):
        q_rows = rows(n0 * (blk * d) + r, u * blk, d)
        if first:
            k_rows = rows(r, u * blk, d)
        else:
            k_rows = rows((n0 - 1) * (blk * d) + r, (u + 1) * blk, d)
        qa = q_ref[0, q_rows, :]
        ka = k_ref[0, k_rows, :].astype(BF16)
        va = v_ref[0, k_rows, :]
        qa0 = jnp.where(head0, qa, 0.0).astype(BF16)
        qa1 = jnp.where(head0, 0.0, qa).astype(BF16)
        va0 = jnp.where(head0, va, 1.0).astype(BF16)
        va1 = jnp.where(head0, 1.0, va).astype(BF16)
        if mode != "init":
            m_old = m_ref[q_rows, :]
            num_old = num_ref[q_rows, :]
            den_old = den_ref[q_rows, :]
        key_slices, scores = [], []
        for i in range(u):
            qs = slice(i * blk, (i + 1) * blk)
            if first and i == 0:
                ks = slice(0, blk)
                bias = bias_ref[:, blk:2 * blk]
            else:
                k0 = (i - 1) * blk if first else i * blk
                ks = slice(k0, k0 + 2 * blk)
                bias = bias_ref[...]
            key_slices.append(ks)
            scores.append(
                (lax.dot_general(qa0[qs], ka[ks], dn, preferred_element_type=F32) + bias,
                 lax.dot_general(qa1[qs], ka[ks], dn, preferred_element_type=F32) + bias))
        probs, ms = [], []
        for s0, s1 in scores:
            m0 = jnp.max(s0, axis=-1, keepdims=True)
            m1 = jnp.max(s1, axis=-1, keepdims=True)
            probs.append((jnp.exp(s0 - m0).astype(BF16), jnp.exp(s1 - m1).astype(BF16)))
            ms.append(jnp.where(head0, m0, m1))
        nums, dens = [], []
        for (p0, p1), ks in zip(probs, key_slices):
            r0 = jnp.dot(p0, va0[ks], preferred_element_type=F32)
            r1 = jnp.dot(p1, va1[ks], preferred_element_type=F32)
            nums.append(jnp.where(head0, r0, r1))
            dens.append(pltpu.roll(jnp.where(head0, r1, r0), HEAD_DIM, 1))
        num = jnp.concatenate(nums, axis=0)
        den = jnp.concatenate(dens, axis=0)
        m = jnp.concatenate(ms, axis=0)
        if mode == "init":
            return [(num_ref, q_rows, num), (den_ref, q_rows, den), (m_ref, q_rows, m)]
        m_new = jnp.maximum(m_old, m)
        a = jnp.exp(m_old - m_new)
        b = jnp.exp(m - m_new)
        num = num_old * a + num * b
        den = den_old * a + den * b
        if mode == "merge":
            return [(num_ref, q_rows, num), (den_ref, q_rows, den), (m_ref, q_rows, m_new)]
        return [(o_ref.at[0], q_rows, num / den)]

    def run(groups):
        stores = [s for g in groups for s in group(*g)]
        for ref, at, val in stores:
            ref[at, :] = val

    for d, mode in zip(DILATIONS, ("init", "merge", "final")):
        nb = seq_len // (d * blk)
        u = min(ATT_UNROLL, nb)
        if nb == u:
            per_body = ATT_UNROLL // u

            def residues(rr, carry, d=d, mode=mode, u=u, per_body=per_body):
                run([(d, rr * per_body + t, 0, u, True, mode) for t in range(per_body)])
                return carry

            lax.fori_loop(0, d // per_body, residues, 0)
        else:
            def per_residue(r, carry, d=d, mode=mode, u=u, nb=nb):
                run([(d, r, 0, u, True, mode)])

                def per_block(g, c):
                    run([(d, r, g * u, u, False, mode)])
                    return c

                lax.fori_loop(1, nb // u, per_block, 0)
                return carry

            lax.fori_loop(0, d, per_residue, 0)


def _attention(q, k, v):
    b, s, _ = q.shape
    spec = pl.BlockSpec((1, s, LANES), lambda i, p: (i, 0, p))
    return pl.pallas_call(
        functools.partial(_attn_kernel, seq_len=s),
        grid=(b, D_ATTN // LANES), in_specs=[spec] * 3, out_specs=spec,
        out_shape=jax.ShapeDtypeStruct((b, s, D_ATTN), F32),
        scratch_shapes=[pltpu.VMEM((s, LANES), F32)] * 3 + [pltpu.VMEM((ATT_BLOCK, 2 * ATT_BLOCK), F32)],
        compiler_params=_cparams(("arbitrary", "arbitrary")), name="attn_prompt",
    )(q, k, v)


def _attn_sample_kernel(q_ref, kn_ref, vn_ref, kc_ref, vc_ref, o_ref, *, w_buf, heads):
    age = w_buf - lax.broadcasted_iota(jnp.int32, (1, w_buf), 1)
    valid = [(age % d == 0) & (age <= d * ATT_BLOCK) for d in DILATIONS]
    for h in range(heads):
        q = q_ref[0, h]
        s = jnp.sum(kc_ref[0, h] * q, axis=0, keepdims=True)
        s_self = jnp.sum(kn_ref[0, h] * q, axis=0, keepdims=True)
        parts = []
        for ok in valid:
            sd = jnp.where(ok, s, NEG)
            m = jnp.maximum(jnp.max(sd, axis=1, keepdims=True), s_self)
            p = jnp.exp(sd - m)
            p_self = jnp.exp(s_self - m)
            parts.append((p, p_self, jnp.sum(p, axis=1, keepdims=True) + p_self, m))
        m_all = jnp.maximum(jnp.maximum(parts[0][3], parts[1][3]), parts[2][3])
        w = jnp.zeros((1, w_buf), F32)
        w_self = jnp.zeros((1, 1), F32)
        den_t = jnp.zeros((1, 1), F32)
        for p, p_self, den, m in parts:
            c = jnp.exp(m - m_all)
            w = w + p * c
            w_self = w_self + p_self * c
            den_t = den_t + den * c
        num_t = jnp.sum(vc_ref[0, h] * w, axis=1, keepdims=True) + vn_ref[0, h] * w_self
        o_ref[0, h] = num_t / den_t


def _attention_sample(q, kn, vn, cache_kt, cache_vt, *, heads=6):
    db, _, _, w_buf = cache_kt.shape
    tok = pl.BlockSpec((1, heads, HEAD_DIM, 1), lambda i, g: (i, g, 0, 0))
    cache = pl.BlockSpec((1, heads, HEAD_DIM, w_buf), lambda i, g: (i, g, 0, 0))
    return pl.pallas_call(
        functools.partial(_attn_sample_kernel, w_buf=w_buf, heads=heads),
        grid=(db, N_HEADS // heads), in_specs=[tok, tok, tok, cache, cache], out_specs=tok,
        out_shape=jax.ShapeDtypeStruct((db, N_HEADS, HEAD_DIM, 1), F32),
        compiler_params=_cparams(("arbitrary", "arbitrary")), name="attn_sample",
    )(q, kn, vn, cache_kt, cache_vt)


def _outproj_kernel(attn_ref, bc_ref, x_ref, ga_ref, gc_ref, wo_ref, gf_ref, wr_ref, *rest,
                    hi, n_tiles):
    hx_ref, route_ref = rest[-2:]
    a = _rms(attn_ref[...], ga_ref[...])
    c = _rms(bc_ref[...], gc_ref[...])
    mix = _mm(a, wo_ref[0:D_ATTN, :], hi) + _mm(c, wo_ref[D_ATTN:D_MODEL, :], hi)
    h = x_ref[...] + mix
    hn = _rms(h, gf_ref[...])

    if hi:
        lg = _mm(hn, wr_ref[...], True)
    else:
        hn_hi = hn.astype(BF16)
        hn_lo = (hn - hn_hi.astype(F32)).astype(BF16)
        both = jnp.dot(hn_hi, wr_ref[...], preferred_element_type=F32)
        lg = (both[:, :LANES] + both[:, LANES:]
              + jnp.dot(hn_lo, wr_ref[:, :LANES], preferred_element_type=F32))
    lane_i = lax.broadcasted_iota(jnp.int32, lg.shape, 1)
    lane = lane_i.astype(F32)
    lane_group = (lane_i // EXPERTS_PER_GROUP).astype(F32)
    is_e = lane_i < N_EXPERTS
    is_g = (lane_i >= N_EXPERTS) & (lane_i < N_EXPERTS + N_GROUPS)
    big = float(LANES)
    first_at = lambda t, v: jnp.min(jnp.where(t == v, lane, big), axis=-1, keepdims=True)

    gl = jnp.where(is_g, lg, NEG)
    mg = jnp.max(gl, axis=-1, keepdims=True)
    p_g = 1.0 / jnp.sum(jnp.exp(gl - mg), axis=-1, keepdims=True)
    g_idx = first_at(gl, mg) - float(N_EXPERTS)
    el = jnp.where(is_e & (lane_group == g_idx), lg, NEG)
    v1 = jnp.max(el, axis=-1, keepdims=True)
    i1 = first_at(el, v1)
    el2 = jnp.where(lane == i1, NEG, el)
    v2 = jnp.max(el2, axis=-1, keepdims=True)
    i2 = first_at(el2, v2)
    e2 = jnp.exp(v2 - v1)
    w1 = p_g / (1.0 + e2)
    w2 = p_g * e2 / (1.0 + e2)
    route = sum(jnp.where(lane_i == j, val, 0.0) for j, val in enumerate([i1, i2, w1, w2]))
    route_ref[...] = route
    tm = h.shape[0]
    pad = jnp.zeros((tm, (TOKEN_ROWS - D_MODEL // LANES - 1) * LANES), F32)
    hx_ref[...] = jnp.concatenate([h, route, pad], axis=1).reshape(tm, TOKEN_ROWS, LANES)

    @pl.when(pl.program_id(0) >= n_tiles)
    def _():
        hx_ref[...] = jnp.zeros(hx_ref.shape, F32)
        route_ref[...] = jnp.zeros(route_ref.shape, F32)


def _outproj(attn, bc, x, ga, gc, w_out, gf, w_router, *, tm, hi, n_rows, row0, into=None):
    n = x.shape[0]
    n_tiles = n // tm
    n_fill = 0 if into is not None else -(-(n_rows - n) // tm)
    row = lambda w: pl.BlockSpec((tm, w), lambda i: (jnp.minimum(i, n_tiles - 1), 0))
    full = lambda a: pl.BlockSpec(a.shape, lambda i: (0,) * a.ndim)
    in_specs = [row(D_ATTN), row(D_CONV), row(D_MODEL), full(ga), full(gc), full(w_out),
                full(gf), full(w_router)]
    args = [attn, bc, x, ga, gc, w_out, gf, w_router]
    aliases = {}
    if into is not None:
        in_specs += [pl.BlockSpec(memory_space=pl.ANY)] * 2
        args += list(into)
        aliases = {len(args) - 2: 0, len(args) - 1: 1}
    return pl.pallas_call(
        functools.partial(_outproj_kernel, hi=hi, n_tiles=n_tiles),
        grid=(n_tiles + n_fill,), in_specs=in_specs,
        out_specs=[pl.BlockSpec((tm, TOKEN_ROWS, LANES), lambda i: (i + row0 // tm, 0, 0)),
                   pl.BlockSpec((tm, LANES), lambda i: (i + row0 // tm, 0))],
        out_shape=[jax.ShapeDtypeStruct((n_rows, TOKEN_ROWS, LANES), F32),
                   jax.ShapeDtypeStruct((n_rows, LANES), F32)],
        input_output_aliases=aliases,
        compiler_params=_cparams(("arbitrary",)), name="outproj_hi" if hi else "outproj",
    )(*args)


def _lane_pack(values, width=LANES):
    rows = values[0].shape[0]
    lane = lax.broadcasted_iota(jnp.int32, (rows, width), 1)
    return sum(jnp.where(lane == j, v, 0.0) for j, v in enumerate(values))


def _plan_rank_kernel(route_ref, cr_ref, meta_ref, tiles_ref, run_ref, *, n_steps, n_tokens):
    i = pl.program_id(0)
    rows = route_ref.shape[0]
    e = float(EXPERTS_PER_GROUP)

    @pl.when(i == 0)
    def _():
        run_ref[...] = jnp.zeros(run_ref.shape, F32)

    r = route_ref[...]
    i1, i2 = r[:, 0:1], r[:, 1:2]
    grp = jnp.floor(i1 * (1.0 / e))
    cls = grp * (e * e) + (jnp.minimum(i1, i2) - grp * e) * e + (jnp.maximum(i1, i2) - grp * e)
    token = i * rows + lax.broadcasted_iota(jnp.int32, (rows, 1), 0)
    lane = lax.broadcasted_iota(jnp.int32, (rows, N_CLASSES), 1).astype(F32)
    onehot = (lane == cls) & (token < n_tokens)
    earlier = (lax.broadcasted_iota(jnp.int32, (rows, rows), 0)
               > lax.broadcasted_iota(jnp.int32, (rows, rows), 1))
    before = jnp.dot(jnp.where(earlier, 1.0, 0.0).astype(BF16),
                     jnp.where(onehot, 1.0, 0.0).astype(BF16), preferred_element_type=F32)
    rank = jnp.sum(jnp.where(onehot, before + run_ref[...], 0.0), axis=-1, keepdims=True)
    run_ref[...] += jnp.sum(jnp.where(onehot, 1.0, 0.0), axis=0, keepdims=True)
    cr_ref[...] = _lane_pack([cls, rank])

    @pl.when(i == n_steps - 1)
    def _():
        counts = run_ref[...]
        tiles = jnp.floor((counts + (MOE_TILE - 1.0)) * (1.0 / MOE_TILE))
        upto = jnp.where(lax.broadcasted_iota(jnp.int32, (N_CLASSES, N_CLASSES), 0)
                         <= lax.broadcasted_iota(jnp.int32, (N_CLASSES, N_CLASSES), 1), 1.0, 0.0)
        both = jnp.concatenate([tiles, counts] + [jnp.zeros_like(counts)] * 6, axis=0)
        ends = _mm(both, upto, True)
        tile_end, tok_end = ends[0:1], ends[1:2]
        tile_off, tok_off = tile_end - tiles, tok_end - counts
        row8 = lax.broadcasted_iota(jnp.int32, (SUBLANES, N_CLASSES), 0)
        meta_ref[...] = sum(jnp.where(row8 == j, v, 0.0)
                            for j, v in enumerate([counts, tile_off, tok_off, tile_end]))
        n_tiles = jnp.max(tile_end, axis=-1, keepdims=True)
        t = lax.broadcasted_iota(jnp.int32, (N_CLASSES, 1), 0).astype(F32)
        t = jnp.minimum(t, n_tiles - 1.0)
        cls_t = jnp.sum(jnp.where(tile_end <= t, 1.0, 0.0), axis=-1, keepdims=True)
        mine = lax.broadcasted_iota(jnp.int32, (N_CLASSES, N_CLASSES), 1).astype(F32) == cls_t
        pick = lambda v: jnp.sum(jnp.where(mine, v, 0.0), axis=-1, keepdims=True)
        base = pick(tok_off) + (t - pick(tile_off)) * MOE_TILE
        tiles_ref[...] = _lane_pack([cls_t, base, n_tiles + jnp.zeros_like(t)])


def _plan_slot_kernel(cr_ref, meta_ref, sp_ref):
    cr = cr_ref[...]
    cls, rank = cr[:, 0:1], cr[:, 1:2]
    lane = lax.broadcasted_iota(jnp.int32, (cr.shape[0], N_CLASSES), 1).astype(F32)
    mine = lane == cls
    meta = meta_ref[...]
    pick = lambda v: jnp.sum(jnp.where(mine, v, 0.0), axis=-1, keepdims=True)
    sp_ref[...] = _lane_pack([pick(meta[1:2]) * MOE_TILE + rank, pick(meta[2:3]) + rank])


def _plan_order_kernel(pos_ref, order_ref, *, n_tokens):
    def place(t, carry):
        order_ref[pos_ref[t]] = t
        return carry

    lax.fori_loop(0, n_tokens, place, 0, unroll=8)
    for j in range(MOE_TILE):
        order_ref[n_tokens + j] = 0


def _route_plan(route, n_tokens):
    n_rows = route.shape[0]
    rows = PLAN_ROWS
    n_steps = n_rows // rows
    blk = pl.BlockSpec((rows, LANES), lambda i: (i, 0))
    whole = lambda r, c: pl.BlockSpec((r, c), lambda i: (0, 0))
    cr, meta, tiles = pl.pallas_call(
        functools.partial(_plan_rank_kernel, n_steps=n_steps, n_tokens=n_tokens),
        grid=(n_steps,), in_specs=[blk],
        out_specs=[blk, whole(SUBLANES, N_CLASSES), whole(N_CLASSES, LANES)],
        out_shape=[jax.ShapeDtypeStruct((n_rows, LANES), F32),
                   jax.ShapeDtypeStruct((SUBLANES, N_CLASSES), F32),
                   jax.ShapeDtypeStruct((N_CLASSES, LANES), F32)],
        scratch_shapes=[pltpu.VMEM((1, N_CLASSES), F32)],
        compiler_params=_cparams(("arbitrary",)), name="plan_rank",
    )(route)
    sp = pl.pallas_call(
        _plan_slot_kernel, grid=(n_steps,), in_specs=[blk, whole(SUBLANES, N_CLASSES)],
        out_specs=blk, out_shape=jax.ShapeDtypeStruct((n_rows, LANES), F32),
        compiler_params=_cparams(("arbitrary",)), name="plan_slot",
    )(cr, meta)
    token_slot = sp[:n_tokens, 0].astype(jnp.int32)
    token_pos = sp[:n_tokens, 1].astype(jnp.int32)
    order = pl.pallas_call(
        functools.partial(_plan_order_kernel, n_tokens=n_tokens),
        in_specs=[pl.BlockSpec(memory_space=pltpu.SMEM)],
        out_specs=pl.BlockSpec(memory_space=pltpu.SMEM),
        out_shape=jax.ShapeDtypeStruct((n_tokens + MOE_TILE,), jnp.int32), name="plan_order",
    )(token_pos)
    tile_cls = tiles[:, 0].astype(jnp.int32)
    tile_base = tiles[:, 1].astype(jnp.int32)
    n_tiles = tiles[0:1, 2].astype(jnp.int32)
    return tile_cls, tile_base, n_tiles, order, token_slot


def _moe_kernel(tcls_ref, tbase_ref, nt_ref, order_ref,
                hx_hbm, gf_ref, wg_ref, wu_ref, wd_ref, y_ref,
                xbuf, wgu_s, wd_s, sem):
    t = pl.program_id(0)
    n_tiles = nt_ref[0]
    buf = t % 2
    pair = EXPERTS_PER_GROUP * EXPERTS_PER_GROUP

    def gather(tile, b):
        base = tbase_ref[tile]
        for j in range(MOE_TILE):
            pltpu.make_async_copy(hx_hbm.at[order_ref[base + j]], xbuf.at[b, j], sem.at[b]).start()

    @pl.when(t == 0)
    def _():
        gather(0, 0)

    @pl.when(t + 1 < n_tiles)
    def _():
        gather(t + 1, 1 - buf)

    @pl.when(t < n_tiles)
    def _():
        cls = tcls_ref[t]
        grp = cls // pair

        @pl.when((t == 0) | (grp != tcls_ref[jnp.maximum(t - 1, 0)] // pair))
        def _():
            for e in range(EXPERTS_PER_GROUP):
                wgu_s[e, :, 0:D_EXPERT] = wg_ref[0, e].astype(BF16)
                wgu_s[e, :, D_EXPERT:2 * D_EXPERT] = wu_ref[0, e].astype(BF16)
                wd_s[e] = wd_ref[0, e].astype(BF16)

        pltpu.make_async_copy(hx_hbm.at[pl.ds(0, MOE_TILE)], xbuf.at[buf], sem.at[buf]).wait()
        x = xbuf[buf].reshape(MOE_TILE, TOKEN_ROWS * LANES)
        h = x[:, 0:D_MODEL]
        route = x[:, D_MODEL:D_MODEL + LANES]
        i1, w1, w2 = route[:, 0:1], route[:, 2:3], route[:, 3:4]
        xn = _rms(h, gf_ref[...]).astype(BF16)
        ea = (cls % pair) // EXPERTS_PER_GROUP
        eb = cls % EXPERTS_PER_GROUP
        first_is_a = i1 == (grp * EXPERTS_PER_GROUP + ea).astype(F32)

        def expert(e):
            gu = jnp.dot(xn, wgu_s[e], preferred_element_type=F32)
            act = (jax.nn.silu(gu[:, 0:D_EXPERT]) * gu[:, D_EXPERT:2 * D_EXPERT]).astype(BF16)
            return jnp.dot(act, wd_s[e], preferred_element_type=F32)

        y = (h + jnp.where(first_is_a, w1, w2) * expert(ea)
             + jnp.where(first_is_a, w2, w1) * expert(eb))
        y_ref[...] = y.reshape(MOE_TILE, D_MODEL // LANES, LANES)

    @pl.when(t >= n_tiles)
    def _():
        y_ref[...] = jnp.zeros(y_ref.shape, F32)


def _moe(hx, gf, w_gate, w_up, w_down, plan, n_slots):
    tile_cls, tile_base, n_tiles, order, _ = plan
    pair = EXPERTS_PER_GROUP * EXPERTS_PER_GROUP
    grouped = lambda w: w.reshape(N_GROUPS, EXPERTS_PER_GROUP, *w.shape[1:])
    wspec = lambda r, c: pl.BlockSpec((1, EXPERTS_PER_GROUP, r, c),
                                      lambda t, tc, tb, nt, od: (tc[t] // pair, 0, 0, 0),
                                      pipeline_mode=pl.Buffered(1))
    grid_spec = pltpu.PrefetchScalarGridSpec(
        num_scalar_prefetch=4, grid=(n_slots // MOE_TILE,),
        in_specs=[pl.BlockSpec(memory_space=pl.ANY),
                  pl.BlockSpec(gf.shape, lambda t, *_: (0, 0)),
                  wspec(D_MODEL, D_EXPERT), wspec(D_MODEL, D_EXPERT), wspec(D_EXPERT, D_MODEL)],
        out_specs=pl.BlockSpec((MOE_TILE, D_MODEL // LANES, LANES), lambda t, *_: (t, 0, 0)),
        scratch_shapes=[pltpu.VMEM((2, MOE_TILE, TOKEN_ROWS, LANES), F32),
                        pltpu.VMEM((EXPERTS_PER_GROUP, D_MODEL, 2 * D_EXPERT), BF16),
                        pltpu.VMEM((EXPERTS_PER_GROUP, D_EXPERT, D_MODEL), BF16),
                        pltpu.SemaphoreType.DMA((2,))])
    return pl.pallas_call(
        _moe_kernel, grid_spec=grid_spec,
        out_shape=jax.ShapeDtypeStruct((n_slots, D_MODEL // LANES, LANES), F32),
        compiler_params=_cparams(("arbitrary",)), name="moe",
    )(tile_cls, tile_base, n_tiles, order, hx, gf, grouped(w_gate), grouped(w_up), grouped(w_down))


def _unsort_kernel(slot_ref, ys_hbm, y_ref, buf, sem, *, rows, row0):
    i = pl.program_id(0)
    b = i % 2

    def gather(step, dst):
        for j in range(rows):
            pltpu.make_async_copy(ys_hbm.at[slot_ref[row0 + step * rows + j]], buf.at[dst, j],
                                  sem.at[dst]).start()

    @pl.when(i == 0)
    def _():
        gather(0, 0)

    @pl.when(i + 1 < pl.num_programs(0))
    def _():
        gather(i + 1, 1 - b)

    pltpu.make_async_copy(ys_hbm.at[pl.ds(0, rows)], buf.at[b], sem.at[b]).wait()
    y_ref[...] = buf[b].reshape(rows, D_MODEL)


def _unsort(ys, token_slot, *, n, row0, rows):
    grid_spec = pltpu.PrefetchScalarGridSpec(
        num_scalar_prefetch=1, grid=(n // rows,),
        in_specs=[pl.BlockSpec(memory_space=pl.ANY)],
        out_specs=pl.BlockSpec((rows, D_MODEL), lambda i, *_: (i, 0)),
        scratch_shapes=[pltpu.VMEM((2, rows, D_MODEL // LANES, LANES), F32),
                        pltpu.SemaphoreType.DMA((2,))])
    return pl.pallas_call(
        functools.partial(_unsort_kernel, rows=rows, row0=row0), grid_spec=grid_spec,
        out_shape=jax.ShapeDtypeStruct((n, D_MODEL), F32),
        compiler_params=_cparams(("arbitrary",)), name="unsort",
    )(token_slot, ys)


def kernel(x_prompt, x_sample, cache_k, cache_v, state_conv, g_norm_mix, w_in, g_q, g_k, w_conv,
           g_attn_out, g_conv_out, w_out, g_norm_ffn, w_router_group, w_router_expert,
           w_gate, w_up, w_down):
    depth = g_norm_mix.shape[0]
    assert depth == 1
    nb, seq, _ = x_prompt.shape
    db, dec_seq, _ = x_sample.shape
    assert dec_seq == 1
    l = 0

    gmix = g_norm_mix[l][None, :]
    gq = jnp.tile(g_q[l], N_HEADS)[None, :]
    gk = jnp.tile(g_k[l], N_HEADS)[None, :]
    ga = g_attn_out[l][None, :]
    gc = g_conv_out[l][None, :]
    gf = g_norm_ffn[l][None, :]
    w_router = jnp.concatenate(
        [w_router_expert[l], w_router_group[l],
         jnp.zeros((D_MODEL, LANES - N_EXPERTS - N_GROUPS), F32)], axis=1)
    w_router_hi = w_router.astype(BF16)
    w_router_split = jnp.concatenate(
        [w_router_hi, (w_router - w_router_hi.astype(F32)).astype(BF16)], axis=1)

    cos_p, sin_p = _rope_tables(np.arange(seq))
    xp = x_prompt.reshape(nb * seq, D_MODEL)
    q, k, v, bc, tail, kt, vt = _inproj(xp, gmix, w_in[l].astype(BF16), gq, gk, cos_p, sin_p,
                                        w_conv[l], tm=256, seq_len=seq, hi=False)
    attn = _attention(q.reshape(nb, seq, D_ATTN), k.reshape(nb, seq, D_ATTN),
                      v.reshape(nb, seq, D_ATTN))
    n_prompt = nb * seq
    n_total = n_prompt + db
    n_rows = -(-n_total // TOKEN_PAD) * TOKEN_PAD
    hx, route = _outproj(attn.reshape(n_prompt, D_ATTN), bc, xp, ga, gc, w_out[l].astype(BF16),
                         gf, w_router_split, tm=256, hi=False, n_rows=n_rows, row0=0)
    k_prompt = jnp.transpose(kt, (0, 3, 1, 2))
    v_prompt = jnp.transpose(vt, (0, 3, 1, 2))
    conv_prompt = tail.reshape(nb, SUBLANES, D_CONV)[:, SUBLANES - (CONV_WIDTH - 1):]

    cos_s, sin_s = _rope_tables(np.full((db,), PAST_LEN))
    xs = x_sample.reshape(db, D_MODEL)
    st = state_conv[l]
    qs, ks, vs, bcs, us = _inproj(xs, gmix, w_in[l], gq, gk, cos_s, sin_s, w_conv[l],
                                  tm=db, seq_len=db, hi=True, prev=(st[:, 1], st[:, 0]))
    cols = lambda t: t.reshape(db, N_HEADS, HEAD_DIM, 1)
    attn_s = _attention_sample(cols(qs), cols(ks), cols(vs),
                               jnp.transpose(cache_k[l], (0, 2, 3, 1)),
                               jnp.transpose(cache_v[l], (0, 2, 3, 1)))
    hx, route = _outproj(attn_s.reshape(db, D_ATTN), bcs, xs, ga, gc, w_out[l], gf, w_router,
                         tm=db, hi=True, n_rows=n_rows, row0=n_prompt, into=(hx, route))

    n_cls = N_GROUPS * EXPERTS_PER_GROUP * (EXPERTS_PER_GROUP - 1) // 2
    n_slots = -(-(n_total + n_cls * (MOE_TILE - 1)) // MOE_TILE) * MOE_TILE
    assert n_slots // MOE_TILE <= N_CLASSES and n_rows % PLAN_ROWS == 0
    plan = _route_plan(route, n_total)
    ys = _moe(hx, gf, w_gate[l], w_up[l], w_down[l], plan, n_slots)
    y_prompt = _unsort(ys, plan[4], n=n_prompt, row0=0, rows=256).reshape(nb, seq, D_MODEL)
    y_sample = _unsort(ys, plan[4], n=db, row0=n_prompt, rows=db).reshape(db, 1, D_MODEL)
    k_sample = ks.reshape(db, 1, N_HEADS, HEAD_DIM)
    v_sample = vs.reshape(db, 1, N_HEADS, HEAD_DIM)
    conv_sample = jnp.stack([st[:, 1], us], axis=1)

    return (y_prompt, y_sample, k_prompt[None], v_prompt[None], conv_prompt[None],
            k_sample[None], v_sample[None], conv_sample[None])
```

```python
import functools

import numpy as np
import jax
import jax.numpy as jnp
from jax import lax
from jax.experimental import pallas as pl
from jax.experimental.pallas import tpu as pltpu

F32 = jnp.float32
BF16 = jnp.bfloat16

D_MODEL = 1024
HEAD_DIM = 64
N_HEADS = 12
D_ATTN = N_HEADS * HEAD_DIM
D_CONV = D_MODEL - D_ATTN
D_IN_PROJ = 3 * D_ATTN + 3 * D_CONV
CONV_WIDTH = 3
DILATIONS = (1, 4, 16)
ATT_BLOCK = 128
ATT_UNROLL = 8
MAX_WINDOW = 2048
PAST_LEN = 8192
ATTN_SCALE = HEAD_DIM ** -0.5
ROPE_THETA = 10000.0
EPS = 1e-6
N_GROUPS = 4
EXPERTS_PER_GROUP = 8
N_EXPERTS = N_GROUPS * EXPERTS_PER_GROUP
D_EXPERT = 256
MOE_TILE = 128
N_CLASSES = N_GROUPS * EXPERTS_PER_GROUP * EXPERTS_PER_GROUP
PLAN_ROWS = 640
TOKEN_PAD = 256

LANES = 128
SUBLANES = 8
MXU_DIM = 256
TOKEN_ROWS = D_MODEL // LANES
NEG = -1e30
VMEM_LIMIT = 48 * 1024 * 1024


def _cparams(sem):
    return pltpu.CompilerParams(dimension_semantics=sem, vmem_limit_bytes=VMEM_LIMIT)


def _mm(a, b, hi):
    if hi:
        return jnp.dot(a, b, preferred_element_type=F32, precision=lax.Precision.HIGHEST)
    return jnp.dot(a.astype(BF16), b.astype(BF16), preferred_element_type=F32)


def _rms(x, g):
    return x * lax.rsqrt(jnp.mean(x * x, axis=-1, keepdims=True) + EPS) * g


def _rope_tables(pos):
    half = HEAD_DIM // 2
    inv = ROPE_THETA ** (-np.arange(half, dtype=np.float64) / half)
    ang = np.asarray(pos, np.float64)[:, None] * inv[None, :]
    cos, sin = np.cos(ang), np.sin(ang)
    cos2 = np.concatenate([cos, cos, cos, cos], axis=-1)
    sin2 = np.concatenate([-sin, sin, -sin, sin], axis=-1)
    return jnp.asarray(cos2, F32), jnp.asarray(sin2, F32)


def _inproj_kernel(*refs, tm, tiles_per_seq, keep_tiles, hi, given_prev):
    if given_prev:
        (x_ref, gmix_ref, w_ref, gq_ref, gk_ref, cos_ref, sin_ref, wconv_ref, p1_ref, p2_ref,
         q_ref, k_ref, v_ref, bc_ref, u_ref) = refs
    else:
        (x_ref, gmix_ref, w_ref, gq_ref, gk_ref, cos_ref, sin_ref, wconv_ref,
         q_ref, k_ref, v_ref, bc_ref, tail_ref, kt_ref, vt_ref, ubuf) = refs

    xn = _rms(x_ref[...], gmix_ref[...])
    if not hi:
        xn = xn.astype(BF16)

    r_i = lax.broadcasted_iota(jnp.int32, (MXU_DIM, MXU_DIM), 0) // HEAD_DIM
    c_i = lax.broadcasted_iota(jnp.int32, (MXU_DIM, MXU_DIM), 1) // HEAD_DIM
    head_sum = jnp.where(r_i == c_i, 1.0, 0.0).astype(F32 if hi else BF16)

    reps = D_ATTN // LANES
    cos = jnp.concatenate([cos_ref[...]] * reps, axis=-1)
    sin = jnp.concatenate([sin_ref[...]] * reps, axis=-1)
    lane = lax.broadcasted_iota(jnp.int32, (1, D_ATTN), 1)
    first_half = (lane % HEAD_DIM) < (HEAD_DIM // 2)

    def head_ssq(z):
        sq = z * z
        return jnp.concatenate(
            [_mm(sq[:, c * MXU_DIM:(c + 1) * MXU_DIM], head_sum, hi)
             for c in range(D_ATTN // MXU_DIM)], axis=-1)

    def norm_rope(z, ssq, g_ref):
        zn = z * lax.rsqrt(ssq * (1.0 / HEAD_DIM) + EPS) * g_ref[...]
        partner = jnp.where(first_half,
                            pltpu.roll(zn, D_ATTN - HEAD_DIM // 2, 1),
                            pltpu.roll(zn, HEAD_DIM // 2, 1))
        return zn * cos + partner * sin

    zq = _mm(xn, w_ref[:, 0:D_ATTN], hi)
    zk = _mm(xn, w_ref[:, D_ATTN:2 * D_ATTN], hi)
    ssq_q = head_ssq(zq)
    v = _mm(xn, w_ref[:, 2 * D_ATTN:3 * D_ATTN], hi)
    ssq_k = head_ssq(zk)
    c0 = 3 * D_ATTN
    b_gate = _mm(xn, w_ref[:, c0:c0 + D_CONV], hi)
    c_gate = _mm(xn, w_ref[:, c0 + D_CONV:c0 + 2 * D_CONV], hi)
    u_in = _mm(xn, w_ref[:, c0 + 2 * D_CONV:c0 + 3 * D_CONV], hi)
    q_ref[...] = norm_rope(zq, ssq_q, gq_ref) * ATTN_SCALE
    k = norm_rope(zk, ssq_k, gk_ref)
    k_ref[...] = k
    v_ref[...] = v
    if not given_prev:
        @pl.when(pl.program_id(0) % tiles_per_seq >= tiles_per_seq - keep_tiles)
        def _():
            kt_ref[0] = k.T.reshape(N_HEADS, HEAD_DIM, tm)
            vt_ref[0] = v.T.reshape(N_HEADS, HEAD_DIM, tm)
    u = c_gate * u_in

    if given_prev:
        u1, u2 = p1_ref[...], p2_ref[...]
        u_ref[...] = u
    else:
        i = pl.program_id(0)

        @pl.when(i % tiles_per_seq == 0)
        def _():
            ubuf[0:SUBLANES, :] = jnp.zeros((SUBLANES, D_CONV), F32)

        @pl.when(i % tiles_per_seq != 0)
        def _():
            ubuf[0:SUBLANES, :] = ubuf[tm:tm + SUBLANES, :]

        ubuf[SUBLANES:tm + SUBLANES, :] = u
        u1 = ubuf[SUBLANES - 1:tm + SUBLANES - 1, :]
        u2 = ubuf[SUBLANES - 2:tm + SUBLANES - 2, :]
        tail_ref[...] = ubuf[tm:tm + SUBLANES, :]

    wc = wconv_ref[...]
    conv = wc[0:1, :] * u2 + wc[1:2, :] * u1 + wc[2:3, :] * u
    bc_ref[...] = b_gate * conv


def _inproj(x, gmix, w_in, gq, gk, cos, sin, w_conv, *, tm, seq_len, hi, prev=None):
    n = x.shape[0]
    nt = n // tm
    tiles_per_seq = seq_len // tm
    row = lambda w: pl.BlockSpec((tm, w), lambda i: (i, 0))
    full = lambda a: pl.BlockSpec(a.shape, lambda i: (0,) * a.ndim)
    tab = pl.BlockSpec((tm, LANES), lambda i: (i % tiles_per_seq, 0))
    in_specs = [row(D_MODEL), full(gmix), full(w_in), full(gq), full(gk), tab, tab, full(w_conv)]
    args = [x, gmix, w_in, gq, gk, cos, sin, w_conv]
    out_shape = [jax.ShapeDtypeStruct((n, D_ATTN), F32)] * 3 + [jax.ShapeDtypeStruct((n, D_CONV), F32)]
    out_specs = [row(D_ATTN)] * 3 + [row(D_CONV)]
    scratch = []
    keep_tiles = min(MAX_WINDOW, seq_len) // tm
    if prev is not None:
        in_specs += [row(D_CONV), row(D_CONV)]
        args += list(prev)
        out_shape.append(jax.ShapeDtypeStruct((n, D_CONV), F32))
        out_specs.append(row(D_CONV))
    else:
        n_seq = n // seq_len
        out_shape.append(jax.ShapeDtypeStruct((n_seq * SUBLANES, D_CONV), F32))
        out_specs.append(pl.BlockSpec((SUBLANES, D_CONV), lambda i: (i // tiles_per_seq, 0)))
        kept = pl.BlockSpec(
            (1, N_HEADS, HEAD_DIM, tm),
            lambda i: (i // tiles_per_seq, 0, 0,
                       jnp.maximum(i % tiles_per_seq - (tiles_per_seq - keep_tiles), 0)))
        out_shape += [jax.ShapeDtypeStruct((n_seq, N_HEADS, HEAD_DIM, keep_tiles * tm), F32)] * 2
        out_specs += [kept, kept]
        scratch.append(pltpu.VMEM((tm + SUBLANES, D_CONV), F32))
    return pl.pallas_call(
        functools.partial(_inproj_kernel, tm=tm, tiles_per_seq=tiles_per_seq,
                          keep_tiles=keep_tiles, hi=hi, given_prev=prev is not None),
        grid=(nt,), in_specs=in_specs, out_specs=out_specs, out_shape=out_shape,
        scratch_shapes=scratch, compiler_params=_cparams(("arbitrary",)),
        name="inproj_hi" if hi else "inproj",
    )(*args)


def _attn_kernel(q_ref, k_ref, v_ref, o_ref, num_ref, den_ref, m_ref, bias_ref, *, seq_len):
    blk = ATT_BLOCK
    qi = lax.broadcasted_iota(jnp.int32, (blk, 2 * blk), 0)
    kj = lax.broadcasted_iota(jnp.int32, (blk, 2 * blk), 1)
    bias_ref[...] = jnp.where((kj >= qi) & (kj <= qi + blk), 0.0, NEG)
    head0 = lax.broadcasted_iota(jnp.int32, (1, LANES), 1) < HEAD_DIM
    dn = (((1,), (1,)), ((), ()))

    def rows(start, size, d):
        return pl.ds(start, size) if d == 1 else pl.ds(start, size, stride=d)

    def group(d, r, n0, u, first, mode):
        q_rows = rows(n0 * (blk * d) + r, u * blk, d)
        if first:
            k_rows = rows(r, u * blk, d)
        else:
            k_rows = rows((n0 - 1) * (blk * d) + r, (u + 1) * blk, d)
        qa = q_ref[0, q_rows, :]
        ka = k_ref[0, k_rows, :].astype(BF16)
        va = v_ref[0, k_rows, :]
        qa0 = jnp.where(head0, qa, 0.0).astype(BF16)
        qa1 = jnp.where(head0, 0.0, qa).astype(BF16)
        va0 = jnp.where(head0, va, 1.0).astype(BF16)
        va1 = jnp.where(head0, 1.0, va).astype(BF16)
        if mode != "init":
            m_old = m_ref[q_rows, :]
            num_old = num_ref[q_rows, :]
            den_old = den_ref[q_rows, :]
        key_slices, scores = [], []
        for i in range(u):
            qs = slice(i * blk, (i + 1) * blk)
            if first and i == 0:
                ks = slice(0, blk)
                bias = bias_ref[:, blk:2 * blk]
            else:
                k0 = (i - 1) * blk if first else i * blk
                ks = slice(k0, k0 + 2 * blk)
                bias = bias_ref[...]
            key_slices.append(ks)
            scores.append(
                (lax.dot_general(qa0[qs], ka[ks], dn, preferred_element_type=F32) + bias,
                 lax.dot_general(qa1[qs], ka[ks], dn, preferred_element_type=F32) + bias))
        probs, ms = [], []
        for s0, s1 in scores:
            m0 = jnp.max(s0, axis=-1, keepdims=True)
            m1 = jnp.max(s1, axis=-1, keepdims=True)
            probs.append((jnp.exp(s0 - m0).astype(BF16), jnp.exp(s1 - m1).astype(BF16)))
            ms.append(jnp.where(head0, m0, m1))
        nums, dens = [], []
        for (p0, p1), ks in zip(probs, key_slices):
            r0 = jnp.dot(p0, va0[ks], preferred_element_type=F32)
            r1 = jnp.dot(p1, va1[ks], preferred_element_type=F32)
            nums.append(jnp.where(head0, r0, r1))
            dens.append(pltpu.roll(jnp.where(head0, r1, r0), HEAD_DIM, 1))
        num = jnp.concatenate(nums, axis=0)
        den = jnp.concatenate(dens, axis=0)
        m = jnp.concatenate(ms, axis=0)
        if mode == "init":
            return [(num_ref, q_rows, num), (den_ref, q_rows, den), (m_ref, q_rows, m)]
        m_new = jnp.maximum(m_old, m)
        a = jnp.exp(m_old - m_new)
        b = jnp.exp(m - m_new)
        num = num_old * a + num * b
        den = den_old * a + den * b
        if mode == "merge":
            return [(num_ref, q_rows, num), (den_ref, q_rows, den), (m_ref, q_rows, m_new)]
        return [(o_ref.at[0], q_rows, num / den)]

    def run(groups):
        stores = [s for g in groups for s in group(*g)]
        for ref, at, val in stores:
            ref[at, :] = val

    for d, mode in zip(DILATIONS, ("init", "merge", "final")):
        nb = seq_len // (d * blk)
        u = min(ATT_UNROLL, nb)
        if nb == u:
            per_body = ATT_UNROLL // u

            def residues(rr, carry, d=d, mode=mode, u=u, per_body=per_body):
                run([(d, rr * per_body + t, 0, u, True, mode) for t in range(per_body)])
                return carry

            lax.fori_loop(0, d // per_body, residues, 0)
        else:
            def per_residue(r, carry, d=d, mode=mode, u=u, nb=nb):
                run([(d, r, 0, u, True, mode)])

                def per_block(g, c):
                    run([(d, r, g * u, u, False, mode)])
                    return c

                lax.fori_loop(1, nb // u, per_block, 0)
                return carry

            lax.fori_loop(0, d, per_residue, 0)


def _attention(q, k, v):
    b, s, _ = q.shape
    spec = pl.BlockSpec((1, s, LANES), lambda i, p: (i, 0, p))
    return pl.pallas_call(
        functools.partial(_attn_kernel, seq_len=s),
        grid=(b, D_ATTN // LANES), in_specs=[spec] * 3, out_specs=spec,
        out_shape=jax.ShapeDtypeStruct((b, s, D_ATTN), F32),
        scratch_shapes=[pltpu.VMEM((s, LANES), F32)] * 3 + [pltpu.VMEM((ATT_BLOCK, 2 * ATT_BLOCK), F32)],
        compiler_params=_cparams(("arbitrary", "arbitrary")), name="attn_prompt",
    )(q, k, v)


def _attn_sample_kernel(q_ref, kn_ref, vn_ref, kc_ref, vc_ref, o_ref, *, w_buf, heads):
    age = w_buf - lax.broadcasted_iota(jnp.int32, (1, w_buf), 1)
    valid = [(age % d == 0) & (age <= d * ATT_BLOCK) for d in DILATIONS]
    diag = (lax.broadcasted_iota(jnp.int32, (HEAD_DIM, HEAD_DIM), 0)
            == lax.broadcasted_iota(jnp.int32, (HEAD_DIM, HEAD_DIM), 1))
    to_col = lambda row: jnp.sum(jnp.where(diag, row, 0.0), axis=1, keepdims=True)
    to_row = lambda col: jnp.sum(jnp.where(diag, col, 0.0), axis=0, keepdims=True)
    q_rows, kn_rows, vn_rows = q_ref[0], kn_ref[0], vn_ref[0]
    s_self_all = jnp.sum(kn_rows * q_rows, axis=1, keepdims=True)
    for h in range(heads):
        q = to_col(q_rows[h:h + 1])
        vn = to_col(vn_rows[h:h + 1])
        s = jnp.sum(kc_ref[0, h] * q, axis=0, keepdims=True)
        s_self = s_self_all[h:h + 1]
        parts = []
        for ok in valid:
            sd = jnp.where(ok, s, NEG)
            m = jnp.maximum(jnp.max(sd, axis=1, keepdims=True), s_self)
            p = jnp.exp(sd - m)
            p_self = jnp.exp(s_self - m)
            parts.append((p, p_self, jnp.sum(p, axis=1, keepdims=True) + p_self, m))
        m_all = jnp.maximum(jnp.maximum(parts[0][3], parts[1][3]), parts[2][3])
        w = jnp.zeros((1, w_buf), F32)
        w_self = jnp.zeros((1, 1), F32)
        den_t = jnp.zeros((1, 1), F32)
        for p, p_self, den, m in parts:
            c = jnp.exp(m - m_all)
            w = w + p * c
            w_self = w_self + p_self * c
            den_t = den_t + den * c
        num_t = jnp.sum(vc_ref[0, h] * w, axis=1, keepdims=True) + vn * w_self
        o_ref[0, h:h + 1, :] = to_row(num_t / den_t)


def _attention_sample(q, kn, vn, cache_kt, cache_vt):
    db, heads, _, w_buf = cache_kt.shape
    tok = pl.BlockSpec((1, heads, HEAD_DIM), lambda i: (i, 0, 0))
    cache = pl.BlockSpec((1, heads, HEAD_DIM, w_buf), lambda i: (i, 0, 0, 0))
    return pl.pallas_call(
        functools.partial(_attn_sample_kernel, w_buf=w_buf, heads=heads),
        grid=(db,), in_specs=[tok, tok, tok, cache, cache], out_specs=tok,
        out_shape=jax.ShapeDtypeStruct((db, heads, HEAD_DIM), F32),
        compiler_params=_cparams(("arbitrary",)), name="attn_sample",
    )(q, kn, vn, cache_kt, cache_vt)


def _outproj_kernel(attn_ref, bc_ref, x_ref, ga_ref, gc_ref, wo_ref, gf_ref, wr_ref, *rest,
                    hi, n_tiles):
    hx_ref, route_ref = rest[-2:]
    a = _rms(attn_ref[...], ga_ref[...])
    c = _rms(bc_ref[...], gc_ref[...])
    mix = _mm(a, wo_ref[0:D_ATTN, :], hi) + _mm(c, wo_ref[D_ATTN:D_MODEL, :], hi)
    h = x_ref[...] + mix
    hn = _rms(h, gf_ref[...])

    if hi:
        lg = _mm(hn, wr_ref[...], True)
    else:
        hn_hi = hn.astype(BF16)
        hn_lo = (hn - hn_hi.astype(F32)).astype(BF16)
        both = jnp.dot(hn_hi, wr_ref[...], preferred_element_type=F32)
        lg = (both[:, :LANES] + both[:, LANES:]
              + jnp.dot(hn_lo, wr_ref[:, :LANES], preferred_element_type=F32))
    lane_i = lax.broadcasted_iota(jnp.int32, lg.shape, 1)
    lane = lane_i.astype(F32)
    lane_group = (lane_i // EXPERTS_PER_GROUP).astype(F32)
    is_e = lane_i < N_EXPERTS
    is_g = (lane_i >= N_EXPERTS) & (lane_i < N_EXPERTS + N_GROUPS)
    big = float(LANES)
    first_at = lambda t, v: jnp.min(jnp.where(t == v, lane, big), axis=-1, keepdims=True)

    gl = jnp.where(is_g, lg, NEG)
    mg = jnp.max(gl, axis=-1, keepdims=True)
    p_g = 1.0 / jnp.sum(jnp.exp(gl - mg), axis=-1, keepdims=True)
    g_idx = first_at(gl, mg) - float(N_EXPERTS)
    el = jnp.where(is_e & (lane_group == g_idx), lg, NEG)
    v1 = jnp.max(el, axis=-1, keepdims=True)
    i1 = first_at(el, v1)
    el2 = jnp.where(lane == i1, NEG, el)
    v2 = jnp.max(el2, axis=-1, keepdims=True)
    i2 = first_at(el2, v2)
    e2 = jnp.exp(v2 - v1)
    w1 = p_g / (1.0 + e2)
    w2 = p_g * e2 / (1.0 + e2)
    route = sum(jnp.where(lane_i == j, val, 0.0) for j, val in enumerate([i1, i2, w1, w2]))
    route_ref[...] = route
    hx_ref[...] = h.reshape(h.shape[0], TOKEN_ROWS, LANES)

    @pl.when(pl.program_id(0) >= n_tiles)
    def _():
        hx_ref[...] = jnp.zeros(hx_ref.shape, F32)
        route_ref[...] = jnp.zeros(route_ref.shape, F32)


def _outproj(attn, bc, x, ga, gc, w_out, gf, w_router, *, tm, hi, n_rows, row0, into=None):
    n = x.shape[0]
    n_tiles = n // tm
    n_fill = 0 if into is not None else -(-(n_rows - n) // tm)
    row = lambda w: pl.BlockSpec((tm, w), lambda i: (jnp.minimum(i, n_tiles - 1), 0))
    full = lambda a: pl.BlockSpec(a.shape, lambda i: (0,) * a.ndim)
    in_specs = [row(D_ATTN), row(D_CONV), row(D_MODEL), full(ga), full(gc), full(w_out),
                full(gf), full(w_router)]
    args = [attn, bc, x, ga, gc, w_out, gf, w_router]
    aliases = {}
    if into is not None:
        in_specs += [pl.BlockSpec(memory_space=pl.ANY)] * 2
        args += list(into)
        aliases = {len(args) - 2: 0, len(args) - 1: 1}
    return pl.pallas_call(
        functools.partial(_outproj_kernel, hi=hi, n_tiles=n_tiles),
        grid=(n_tiles + n_fill,), in_specs=in_specs,
        out_specs=[pl.BlockSpec((tm, TOKEN_ROWS, LANES), lambda i: (i + row0 // tm, 0, 0)),
                   pl.BlockSpec((tm, LANES), lambda i: (i + row0 // tm, 0))],
        out_shape=[jax.ShapeDtypeStruct((n_rows, TOKEN_ROWS, LANES), F32),
                   jax.ShapeDtypeStruct((n_rows, LANES), F32)],
        input_output_aliases=aliases,
        compiler_params=_cparams(("arbitrary",)), name="outproj_hi" if hi else "outproj",
    )(*args)


def _lane_pack(values, width=LANES):
    rows = values[0].shape[0]
    lane = lax.broadcasted_iota(jnp.int32, (rows, width), 1)
    return sum(jnp.where(lane == j, v, 0.0) for j, v in enumerate(values))


def _plan_rank_kernel(route_ref, cr_ref, meta_ref, tiles_ref, run_ref, *, n_steps, n_tokens):
    i = pl.program_id(0)
    rows = route_ref.shape[0]
    e = float(EXPERTS_PER_GROUP)

    @pl.when(i == 0)
    def _():
        run_ref[...] = jnp.zeros(run_ref.shape, F32)

    r = route_ref[...]
    i1, i2 = r[:, 0:1], r[:, 1:2]
    grp = jnp.floor(i1 * (1.0 / e))
    cls = grp * (e * e) + (jnp.minimum(i1, i2) - grp * e) * e + (jnp.maximum(i1, i2) - grp * e)
    token = i * rows + lax.broadcasted_iota(jnp.int32, (rows, 1), 0)
    lane = lax.broadcasted_iota(jnp.int32, (rows, N_CLASSES), 1).astype(F32)
    onehot = (lane == cls) & (token < n_tokens)
    earlier = (lax.broadcasted_iota(jnp.int32, (rows, rows), 0)
               > lax.broadcasted_iota(jnp.int32, (rows, rows), 1))
    before = jnp.dot(jnp.where(earlier, 1.0, 0.0).astype(BF16),
                     jnp.where(onehot, 1.0, 0.0).astype(BF16), preferred_element_type=F32)
    rank = jnp.sum(jnp.where(onehot, before + run_ref[...], 0.0), axis=-1, keepdims=True)
    run_ref[...] += jnp.sum(jnp.where(onehot, 1.0, 0.0), axis=0, keepdims=True)
    cr_ref[...] = _lane_pack([cls, rank])

    @pl.when(i == n_steps - 1)
    def _():
        counts = run_ref[...]
        tiles = jnp.floor((counts + (MOE_TILE - 1.0)) * (1.0 / MOE_TILE))
        upto = jnp.where(lax.broadcasted_iota(jnp.int32, (N_CLASSES, N_CLASSES), 0)
                         <= lax.broadcasted_iota(jnp.int32, (N_CLASSES, N_CLASSES), 1), 1.0, 0.0)
        both = jnp.concatenate([tiles, counts] + [jnp.zeros_like(counts)] * 6, axis=0)
        ends = _mm(both, upto, True)
        tile_end, tok_end = ends[0:1], ends[1:2]
        tile_off, tok_off = tile_end - tiles, tok_end - counts
        row8 = lax.broadcasted_iota(jnp.int32, (SUBLANES, N_CLASSES), 0)
        meta_ref[...] = sum(jnp.where(row8 == j, v, 0.0)
                            for j, v in enumerate([counts, tile_off, tok_off, tile_end]))
        n_tiles = jnp.max(tile_end, axis=-1, keepdims=True)
        t = lax.broadcasted_iota(jnp.int32, (N_CLASSES, 1), 0).astype(F32)
        t = jnp.minimum(t, n_tiles - 1.0)
        cls_t = jnp.sum(jnp.where(tile_end <= t, 1.0, 0.0), axis=-1, keepdims=True)
        mine = lax.broadcasted_iota(jnp.int32, (N_CLASSES, N_CLASSES), 1).astype(F32) == cls_t
        pick = lambda v: jnp.sum(jnp.where(mine, v, 0.0), axis=-1, keepdims=True)
        base = pick(tok_off) + (t - pick(tile_off)) * MOE_TILE
        tiles_ref[...] = _lane_pack([cls_t, base, n_tiles + jnp.zeros_like(t)])


def _plan_slot_kernel(cr_ref, meta_ref, route_ref, sp_ref):
    cr = cr_ref[...]
    cls, rank = cr[:, 0:1], cr[:, 1:2]
    lane = lax.broadcasted_iota(jnp.int32, (cr.shape[0], N_CLASSES), 1).astype(F32)
    mine = lane == cls
    meta = meta_ref[...]
    pick = lambda v: jnp.sum(jnp.where(mine, v, 0.0), axis=-1, keepdims=True)
    r = route_ref[...]
    i1, i2, w1, w2 = r[:, 0:1], r[:, 1:2], r[:, 2:3], r[:, 3:4]
    first_is_a = i1 < i2
    sp_ref[...] = _lane_pack([pick(meta[1:2]) * MOE_TILE + rank, pick(meta[2:3]) + rank,
                              jnp.where(first_is_a, w1, w2), jnp.where(first_is_a, w2, w1)])


def _plan_order_kernel(pos_ref, order_ref, *, n_tokens):
    def place(t, carry):
        order_ref[pos_ref[t]] = t
        return carry

    lax.fori_loop(0, n_tokens, place, 0, unroll=8)
    for j in range(MOE_TILE):
        order_ref[n_tokens + j] = 0


def _route_plan(route, n_tokens):
    n_rows = route.shape[0]
    rows = PLAN_ROWS
    n_steps = n_rows // rows
    blk = pl.BlockSpec((rows, LANES), lambda i: (i, 0))
    whole = lambda r, c: pl.BlockSpec((r, c), lambda i: (0, 0))
    cr, meta, tiles = pl.pallas_call(
        functools.partial(_plan_rank_kernel, n_steps=n_steps, n_tokens=n_tokens),
        grid=(n_steps,), in_specs=[blk],
        out_specs=[blk, whole(SUBLANES, N_CLASSES), whole(N_CLASSES, LANES)],
        out_shape=[jax.ShapeDtypeStruct((n_rows, LANES), F32),
                   jax.ShapeDtypeStruct((SUBLANES, N_CLASSES), F32),
                   jax.ShapeDtypeStruct((N_CLASSES, LANES), F32)],
        scratch_shapes=[pltpu.VMEM((1, N_CLASSES), F32)],
        compiler_params=_cparams(("arbitrary",)), name="plan_rank",
    )(route)
    sp = pl.pallas_call(
        _plan_slot_kernel, grid=(n_steps,), in_specs=[blk, whole(SUBLANES, N_CLASSES), blk],
        out_specs=blk, out_shape=jax.ShapeDtypeStruct((n_rows, LANES), F32),
        compiler_params=_cparams(("arbitrary",)), name="plan_slot",
    )(cr, meta, route)
    token_slot = sp[:n_tokens, 0].astype(jnp.int32)
    token_pos = sp[:n_tokens, 1].astype(jnp.int32)
    order = pl.pallas_call(
        functools.partial(_plan_order_kernel, n_tokens=n_tokens),
        in_specs=[pl.BlockSpec(memory_space=pltpu.SMEM)],
        out_specs=pl.BlockSpec(memory_space=pltpu.SMEM),
        out_shape=jax.ShapeDtypeStruct((n_tokens + MOE_TILE,), jnp.int32), name="plan_order",
    )(token_pos)
    tile_cls = tiles[:, 0].astype(jnp.int32)
    tile_base = tiles[:, 1].astype(jnp.int32)
    n_tiles = tiles[0:1, 2].astype(jnp.int32)
    return tile_cls, tile_base, n_tiles, order, sp[:n_tokens, 2], sp[:n_tokens, 3], token_slot


def _moe_kernel(tcls_ref, tbase_ref, nt_ref, order_ref, wa_ref, wb_ref,
                hx_hbm, gf_ref, wg_ref, wu_ref, wd_ref, y_ref,
                xbuf, wgu_s, wd_s, sem):
    t = pl.program_id(0)
    n_tiles = nt_ref[0]
    buf = t % 2
    pair = EXPERTS_PER_GROUP * EXPERTS_PER_GROUP

    def gather(tile, b):
        base = tbase_ref[tile]
        for j in range(MOE_TILE):
            pltpu.make_async_copy(hx_hbm.at[order_ref[base + j]], xbuf.at[b, j], sem.at[b]).start()

    @pl.when(t == 0)
    def _():
        gather(0, 0)

    @pl.when(t + 1 < n_tiles)
    def _():
        gather(t + 1, 1 - buf)

    @pl.when(t < n_tiles)
    def _():
        cls = tcls_ref[t]
        grp = cls // pair

        @pl.when((t == 0) | (grp != tcls_ref[jnp.maximum(t - 1, 0)] // pair))
        def _():
            for e in range(EXPERTS_PER_GROUP):
                wgu_s[e, :, 0:D_EXPERT] = wg_ref[0, e].astype(BF16)
                wgu_s[e, :, D_EXPERT:2 * D_EXPERT] = wu_ref[0, e].astype(BF16)
                wd_s[e] = wd_ref[0, e].astype(BF16)

        pltpu.make_async_copy(hx_hbm.at[pl.ds(0, MOE_TILE)], xbuf.at[buf], sem.at[buf]).wait()
        h = xbuf[buf].reshape(MOE_TILE, D_MODEL)
        xn = _rms(h, gf_ref[...]).astype(BF16)
        ea = (cls % pair) // EXPERTS_PER_GROUP
        eb = cls % EXPERTS_PER_GROUP

        def column(w_ref):
            base = tbase_ref[t]
            sub = lax.broadcasted_iota(jnp.int32, (SUBLANES, 1), 0)
            chunks = []
            for c in range(MOE_TILE // SUBLANES):
                acc = jnp.zeros((SUBLANES, 1), F32)
                for k in range(SUBLANES):
                    acc = jnp.where(sub == k, w_ref[order_ref[base + c * SUBLANES + k]], acc)
                chunks.append(acc)
            return jnp.concatenate(chunks, axis=0)

        def expert(e):
            gu = jnp.dot(xn, wgu_s[e], preferred_element_type=F32)
            act = (jax.nn.silu(gu[:, 0:D_EXPERT]) * gu[:, D_EXPERT:2 * D_EXPERT]).astype(BF16)
            return jnp.dot(act, wd_s[e], preferred_element_type=F32)

        y = h + column(wa_ref) * expert(ea) + column(wb_ref) * expert(eb)
        y_ref[...] = y.reshape(MOE_TILE, D_MODEL // LANES, LANES)

    @pl.when(t >= n_tiles)
    def _():
        y_ref[...] = jnp.zeros(y_ref.shape, F32)


def _moe(hx, gf, w_gate, w_up, w_down, plan, n_slots):
    tile_cls, tile_base, n_tiles, order, wa, wb, _ = plan
    pair = EXPERTS_PER_GROUP * EXPERTS_PER_GROUP
    grouped = lambda w: w.reshape(N_GROUPS, EXPERTS_PER_GROUP, *w.shape[1:])
    wspec = lambda r, c: pl.BlockSpec((1, EXPERTS_PER_GROUP, r, c),
                                      lambda t, tc, *_: (tc[t] // pair, 0, 0, 0),
                                      pipeline_mode=pl.Buffered(1))
    grid_spec = pltpu.PrefetchScalarGridSpec(
        num_scalar_prefetch=6, grid=(n_slots // MOE_TILE,),
        in_specs=[pl.BlockSpec(memory_space=pl.ANY),
                  pl.BlockSpec(gf.shape, lambda t, *_: (0, 0)),
                  wspec(D_MODEL, D_EXPERT), wspec(D_MODEL, D_EXPERT), wspec(D_EXPERT, D_MODEL)],
        out_specs=pl.BlockSpec((MOE_TILE, D_MODEL // LANES, LANES), lambda t, *_: (t, 0, 0)),
        scratch_shapes=[pltpu.VMEM((2, MOE_TILE, TOKEN_ROWS, LANES), F32),
                        pltpu.VMEM((EXPERTS_PER_GROUP, D_MODEL, 2 * D_EXPERT), BF16),
                        pltpu.VMEM((EXPERTS_PER_GROUP, D_EXPERT, D_MODEL), BF16),
                        pltpu.SemaphoreType.DMA((2,))])
    return pl.pallas_call(
        _moe_kernel, grid_spec=grid_spec,
        out_shape=jax.ShapeDtypeStruct((n_slots, D_MODEL // LANES, LANES), F32),
        compiler_params=_cparams(("arbitrary",)), name="moe",
    )(tile_cls, tile_base, n_tiles, order, wa, wb, hx, gf,
      grouped(w_gate), grouped(w_up), grouped(w_down))


def _unsort_kernel(slot_ref, ys_hbm, y_ref, buf, sem, *, rows, row0):
    i = pl.program_id(0)
    b = i % 2

    def gather(step, dst):
        for j in range(rows):
            pltpu.make_async_copy(ys_hbm.at[slot_ref[row0 + step * rows + j]], buf.at[dst, j],
                                  sem.at[dst]).start()

    @pl.when(i == 0)
    def _():
        gather(0, 0)

    @pl.when(i + 1 < pl.num_programs(0))
    def _():
        gather(i + 1, 1 - b)

    pltpu.make_async_copy(ys_hbm.at[pl.ds(0, rows)], buf.at[b], sem.at[b]).wait()
    y_ref[...] = buf[b].reshape(rows, D_MODEL)


def _unsort(ys, token_slot, *, n, row0, rows):
    grid_spec = pltpu.PrefetchScalarGridSpec(
        num_scalar_prefetch=1, grid=(n // rows,),
        in_specs=[pl.BlockSpec(memory_space=pl.ANY)],
        out_specs=pl.BlockSpec((rows, D_MODEL), lambda i, *_: (i, 0)),
        scratch_shapes=[pltpu.VMEM((2, rows, D_MODEL // LANES, LANES), F32),
                        pltpu.SemaphoreType.DMA((2,))])
    return pl.pallas_call(
        functools.partial(_unsort_kernel, rows=rows, row0=row0), grid_spec=grid_spec,
        out_shape=jax.ShapeDtypeStruct((n, D_MODEL), F32),
        compiler_params=_cparams(("arbitrary",)), name="unsort",
    )(token_slot, ys)


def kernel(x_prompt, x_sample, cache_k, cache_v, state_conv, g_norm_mix, w_in, g_q, g_k, w_conv,
           g_attn_out, g_conv_out, w_out, g_norm_ffn, w_router_group, w_router_expert,
           w_gate, w_up, w_down):
    depth = g_norm_mix.shape[0]
    assert depth == 1
    nb, seq, _ = x_prompt.shape
    db, dec_seq, _ = x_sample.shape
    assert dec_seq == 1
    l = 0

    gmix = g_norm_mix[l][None, :]
    gq = jnp.tile(g_q[l], N_HEADS)[None, :]
    gk = jnp.tile(g_k[l], N_HEADS)[None, :]
    ga = g_attn_out[l][None, :]
    gc = g_conv_out[l][None, :]
    gf = g_norm_ffn[l][None, :]
    w_router = jnp.concatenate(
        [w_router_expert[l], w_router_group[l],
         jnp.zeros((D_MODEL, LANES - N_EXPERTS - N_GROUPS), F32)], axis=1)
    w_router_hi = w_router.astype(BF16)
    w_router_split = jnp.concatenate(
        [w_router_hi, (w_router - w_router_hi.astype(F32)).astype(BF16)], axis=1)

    cos_p, sin_p = _rope_tables(np.arange(seq))
    xp = x_prompt.reshape(nb * seq, D_MODEL)
    q, k, v, bc, tail, kt, vt = _inproj(xp, gmix, w_in[l].astype(BF16), gq, gk, cos_p, sin_p,
                                        w_conv[l], tm=256, seq_len=seq, hi=False)
    attn = _attention(q.reshape(nb, seq, D_ATTN), k.reshape(nb, seq, D_ATTN),
                      v.reshape(nb, seq, D_ATTN))
    n_prompt = nb * seq
    n_total = n_prompt + db
    n_rows = -(-n_total // TOKEN_PAD) * TOKEN_PAD
    hx, route = _outproj(attn.reshape(n_prompt, D_ATTN), bc, xp, ga, gc, w_out[l].astype(BF16),
                         gf, w_router_split, tm=256, hi=False, n_rows=n_rows, row0=0)
    k_prompt = jnp.transpose(kt, (0, 3, 1, 2))
    v_prompt = jnp.transpose(vt, (0, 3, 1, 2))
    conv_prompt = tail.reshape(nb, SUBLANES, D_CONV)[:, SUBLANES - (CONV_WIDTH - 1):]

    cos_s, sin_s = _rope_tables(np.full((db,), PAST_LEN))
    xs = x_sample.reshape(db, D_MODEL)
    st = state_conv[l]
    qs, ks, vs, bcs, us = _inproj(xs, gmix, w_in[l], gq, gk, cos_s, sin_s, w_conv[l],
                                  tm=db, seq_len=db, hi=True, prev=(st[:, 1], st[:, 0]))
    heads = lambda t: t.reshape(db, N_HEADS, HEAD_DIM)
    attn_s = _attention_sample(heads(qs), heads(ks), heads(vs),
                               jnp.transpose(cache_k[l], (0, 2, 3, 1)),
                               jnp.transpose(cache_v[l], (0, 2, 3, 1)))
    hx, route = _outproj(attn_s.reshape(db, D_ATTN), bcs, xs, ga, gc, w_out[l], gf, w_router,
                         tm=db, hi=True, n_rows=n_rows, row0=n_prompt, into=(hx, route))

    n_cls = N_GROUPS * EXPERTS_PER_GROUP * (EXPERTS_PER_GROUP - 1) // 2
    n_slots = -(-(n_total + n_cls * (MOE_TILE - 1)) // MOE_TILE) * MOE_TILE
    assert n_slots // MOE_TILE <= N_CLASSES and n_rows % PLAN_ROWS == 0
    plan = _route_plan(route, n_total)
    ys = _moe(hx, gf, w_gate[l], w_up[l], w_down[l], plan, n_slots)
    y_prompt = _unsort(ys, plan[-1], n=n_prompt, row0=0, rows=256).reshape(nb, seq, D_MODEL)
    y_sample = _unsort(ys, plan[-1], n=db, row0=n_prompt, rows=db).reshape(db, 1, D_MODEL)
    k_sample = ks.reshape(db, 1, N_HEADS, HEAD_DIM)
    v_sample = vs.reshape(db, 1, N_HEADS, HEAD_DIM)
    conv_sample = jnp.stack([st[:, 1], us], axis=1)

    return (y_prompt, y_sample, k_prompt[None], v_prompt[None], conv_prompt[None],
            k_sample[None], v_sample[None], conv_sample[None])
```

```python
import functools

import numpy as np
import jax
import jax.numpy as jnp
from jax import lax
from jax.experimental import pallas as pl
from jax.experimental.pallas import tpu as pltpu

F32 = jnp.float32
BF16 = jnp.bfloat16

D_MODEL = 1024
HEAD_DIM = 64
N_HEADS = 12
D_ATTN = N_HEADS * HEAD_DIM
D_CONV = D_MODEL - D_ATTN
D_IN_PROJ = 3 * D_ATTN + 3 * D_CONV
CONV_WIDTH = 3
DILATIONS = (1, 4, 16)
ATT_BLOCK = 128
ATT_UNROLL = 8
REGROUP_STEP = 4
MAX_WINDOW = 2048
PAST_LEN = 8192
ATTN_SCALE = HEAD_DIM ** -0.5
ROPE_THETA = 10000.0
EPS = 1e-6
N_GROUPS = 4
EXPERTS_PER_GROUP = 8
N_EXPERTS = N_GROUPS * EXPERTS_PER_GROUP
D_EXPERT = 256
MOE_TILE = 192
N_CLASSES = N_GROUPS * EXPERTS_PER_GROUP * EXPERTS_PER_GROUP
PLAN_ROWS = 640
TOKEN_PAD = 256

LANES = 128
SUBLANES = 8
MXU_DIM = 256
TOKEN_ROWS = D_MODEL // LANES
NEG = -1e30
VMEM_LIMIT = 48 * 1024 * 1024


def _cparams(sem):
    return pltpu.CompilerParams(dimension_semantics=sem, vmem_limit_bytes=VMEM_LIMIT)


def _mm(a, b, hi):
    if hi:
        return jnp.dot(a, b, preferred_element_type=F32, precision=lax.Precision.HIGHEST)
    return jnp.dot(a.astype(BF16), b.astype(BF16), preferred_element_type=F32)


def _rms(x, g):
    return x * lax.rsqrt(jnp.mean(x * x, axis=-1, keepdims=True) + EPS) * g


def _rope_tables(pos):
    half = HEAD_DIM // 2
    inv = ROPE_THETA ** (-np.arange(half, dtype=np.float64) / half)
    ang = np.asarray(pos, np.float64)[:, None] * inv[None, :]
    cos, sin = np.cos(ang), np.sin(ang)
    cos2 = np.concatenate([cos, cos, cos, cos], axis=-1)
    sin2 = np.concatenate([-sin, sin, -sin, sin], axis=-1)
    return jnp.asarray(cos2, F32), jnp.asarray(sin2, F32)


def _inproj_kernel(*refs, tm, tiles_per_seq, keep_tiles, hi, given_prev):
    if given_prev:
        (x_ref, gmix_ref, w_ref, gq_ref, gk_ref, cos_ref, sin_ref, wconv_ref, p1_ref, p2_ref,
         q_ref, k_ref, v_ref, bc_ref, u_ref) = refs
    else:
        (x_ref, gmix_ref, w_ref, gq_ref, gk_ref, cos_ref, sin_ref, wconv_ref,
         q_ref, k_ref, v_ref, bc_ref, tail_ref, kt_ref, vt_ref, ubuf) = refs

    xn = _rms(x_ref[...], gmix_ref[...])
    if not hi:
        xn = xn.astype(BF16)

    r_i = lax.broadcasted_iota(jnp.int32, (MXU_DIM, MXU_DIM), 0) // HEAD_DIM
    c_i = lax.broadcasted_iota(jnp.int32, (MXU_DIM, MXU_DIM), 1) // HEAD_DIM
    head_sum = jnp.where(r_i == c_i, 1.0, 0.0).astype(F32 if hi else BF16)

    reps = D_ATTN // LANES
    cos = jnp.concatenate([cos_ref[...]] * reps, axis=-1)
    sin = jnp.concatenate([sin_ref[...]] * reps, axis=-1)
    lane = lax.broadcasted_iota(jnp.int32, (1, D_ATTN), 1)
    first_half = (lane % HEAD_DIM) < (HEAD_DIM // 2)

    def head_ssq(z):
        sq = z * z
        return jnp.concatenate(
            [_mm(sq[:, c * MXU_DIM:(c + 1) * MXU_DIM], head_sum, hi)
             for c in range(D_ATTN // MXU_DIM)], axis=-1)

    def norm_rope(z, ssq, g_ref):
        zn = z * lax.rsqrt(ssq * (1.0 / HEAD_DIM) + EPS) * g_ref[...]
        partner = jnp.where(first_half,
                            pltpu.roll(zn, D_ATTN - HEAD_DIM // 2, 1),
                            pltpu.roll(zn, HEAD_DIM // 2, 1))
        return zn * cos + partner * sin

    zq = _mm(xn, w_ref[:, 0:D_ATTN], hi)
    zk = _mm(xn, w_ref[:, D_ATTN:2 * D_ATTN], hi)
    ssq_q = head_ssq(zq)
    v = _mm(xn, w_ref[:, 2 * D_ATTN:3 * D_ATTN], hi)
    ssq_k = head_ssq(zk)
    c0 = 3 * D_ATTN
    b_gate = _mm(xn, w_ref[:, c0:c0 + D_CONV], hi)
    c_gate = _mm(xn, w_ref[:, c0 + D_CONV:c0 + 2 * D_CONV], hi)
    u_in = _mm(xn, w_ref[:, c0 + 2 * D_CONV:c0 + 3 * D_CONV], hi)
    q_ref[...] = norm_rope(zq, ssq_q, gq_ref) * ATTN_SCALE
    k = norm_rope(zk, ssq_k, gk_ref)
    k_ref[...] = k
    v_ref[...] = v
    if not given_prev:
        @pl.when(pl.program_id(0) % tiles_per_seq >= tiles_per_seq - keep_tiles)
        def _():
            kt_ref[0] = k.T.reshape(N_HEADS, HEAD_DIM, tm)
            vt_ref[0] = v.T.reshape(N_HEADS, HEAD_DIM, tm)
    u = c_gate * u_in

    if given_prev:
        u1, u2 = p1_ref[...], p2_ref[...]
        u_ref[...] = u
    else:
        i = pl.program_id(0)

        @pl.when(i % tiles_per_seq == 0)
        def _():
            ubuf[0:SUBLANES, :] = jnp.zeros((SUBLANES, D_CONV), F32)

        @pl.when(i % tiles_per_seq != 0)
        def _():
            ubuf[0:SUBLANES, :] = ubuf[tm:tm + SUBLANES, :]

        ubuf[SUBLANES:tm + SUBLANES, :] = u
        u1 = ubuf[SUBLANES - 1:tm + SUBLANES - 1, :]
        u2 = ubuf[SUBLANES - 2:tm + SUBLANES - 2, :]
        tail_ref[...] = ubuf[tm:tm + SUBLANES, :]

    wc = wconv_ref[...]
    conv = wc[0:1, :] * u2 + wc[1:2, :] * u1 + wc[2:3, :] * u
    bc_ref[...] = b_gate * conv


def _inproj(x, gmix, w_in, gq, gk, cos, sin, w_conv, *, tm, seq_len, hi, prev=None):
    n = x.shape[0]
    nt = n // tm
    tiles_per_seq = seq_len // tm
    row = lambda w: pl.BlockSpec((tm, w), lambda i: (i, 0))
    full = lambda a: pl.BlockSpec(a.shape, lambda i: (0,) * a.ndim)
    tab = pl.BlockSpec((tm, LANES), lambda i: (i % tiles_per_seq, 0))
    in_specs = [row(D_MODEL), full(gmix), full(w_in), full(gq), full(gk), tab, tab, full(w_conv)]
    args = [x, gmix, w_in, gq, gk, cos, sin, w_conv]
    out_shape = [jax.ShapeDtypeStruct((n, D_ATTN), F32)] * 3 + [jax.ShapeDtypeStruct((n, D_CONV), F32)]
    out_specs = [row(D_ATTN)] * 3 + [row(D_CONV)]
    scratch = []
    keep_tiles = min(MAX_WINDOW, seq_len) // tm
    if prev is not None:
        in_specs += [row(D_CONV), row(D_CONV)]
        args += list(prev)
        out_shape.append(jax.ShapeDtypeStruct((n, D_CONV), F32))
        out_specs.append(row(D_CONV))
    else:
        n_seq = n // seq_len
        out_shape.append(jax.ShapeDtypeStruct((n_seq * SUBLANES, D_CONV), F32))
        out_specs.append(pl.BlockSpec((SUBLANES, D_CONV), lambda i: (i // tiles_per_seq, 0)))
        kept = pl.BlockSpec(
            (1, N_HEADS, HEAD_DIM, tm),
            lambda i: (i // tiles_per_seq, 0, 0,
                       jnp.maximum(i % tiles_per_seq - (tiles_per_seq - keep_tiles), 0)))
        out_shape += [jax.ShapeDtypeStruct((n_seq, N_HEADS, HEAD_DIM, keep_tiles * tm), F32)] * 2
        out_specs += [kept, kept]
        scratch.append(pltpu.VMEM((tm + SUBLANES, D_CONV), F32))
    return pl.pallas_call(
        functools.partial(_inproj_kernel, tm=tm, tiles_per_seq=tiles_per_seq,
                          keep_tiles=keep_tiles, hi=hi, given_prev=prev is not None),
        grid=(nt,), in_specs=in_specs, out_specs=out_specs, out_shape=out_shape,
        scratch_shapes=scratch, compiler_params=_cparams(("arbitrary",)),
        name="inproj_hi" if hi else "inproj",
    )(*args)


def _attn_kernel(q_ref, k_ref, v_ref, o_ref, num_ref, den_ref, m_ref, bias_ref,
                 qd_ref, kd_ref, vd_ref, numd_ref, dend_ref, md_ref, tmp_ref, *, seq_len):
    blk = ATT_BLOCK
    qi = lax.broadcasted_iota(jnp.int32, (blk, 2 * blk), 0)
    kj = lax.broadcasted_iota(jnp.int32, (blk, 2 * blk), 1)
    bias_ref[...] = jnp.where((kj >= qi) & (kj <= qi + blk), 0.0, NEG)
    head0 = lax.broadcasted_iota(jnp.int32, (1, LANES), 1) < HEAD_DIM
    dn = (((1,), (1,)), ((), ()))

    def rows(start, size, d):
        return pl.ds(start, size) if d == 1 else pl.ds(start, size, stride=d)

    natural = (lambda at: q_ref[0, at, :], lambda at: k_ref[0, at, :], lambda at: v_ref[0, at, :],
               (num_ref, den_ref, m_ref))
    regrouped = (lambda at: qd_ref[at, :], lambda at: kd_ref[at, :], lambda at: vd_ref[at, :],
                 (numd_ref, dend_ref, md_ref))

    def group(src, d, r, n0, u, first, mode):
        q_at, k_at, v_at, (num_acc, den_acc, m_acc) = src
        q_rows = rows(n0 * (blk * d) + r, u * blk, d)
        if first:
            k_rows = rows(r, u * blk, d)
        else:
            k_rows = rows((n0 - 1) * (blk * d) + r, (u + 1) * blk, d)
        qa = q_at(q_rows)
        ka = k_at(k_rows).astype(BF16)
        va = v_at(k_rows)
        qa0 = jnp.where(head0, qa, 0.0).astype(BF16)
        qa1 = jnp.where(head0, 0.0, qa).astype(BF16)
        va0 = jnp.where(head0, va, 1.0).astype(BF16)
        va1 = jnp.where(head0, 1.0, va).astype(BF16)
        if mode != "init":
            m_old = m_acc[q_rows, :]
            num_old = num_acc[q_rows, :]
            den_old = den_acc[q_rows, :]
        key_slices, scores = [], []
        for i in range(u):
            qs = slice(i * blk, (i + 1) * blk)
            if first and i == 0:
                ks = slice(0, blk)
                bias = bias_ref[:, blk:2 * blk]
            else:
                k0 = (i - 1) * blk if first else i * blk
                ks = slice(k0, k0 + 2 * blk)
                bias = bias_ref[...]
            key_slices.append(ks)
            scores.append(
                (lax.dot_general(qa0[qs], ka[ks], dn, preferred_element_type=F32) + bias,
                 lax.dot_general(qa1[qs], ka[ks], dn, preferred_element_type=F32) + bias))
        yield
        probs, ms = [], []
        for s0, s1 in scores:
            m0 = jnp.max(s0, axis=-1, keepdims=True)
            m1 = jnp.max(s1, axis=-1, keepdims=True)
            probs.append((jnp.exp(s0 - m0).astype(BF16), jnp.exp(s1 - m1).astype(BF16)))
            ms.append(jnp.where(head0, m0, m1))
        yield
        nums, dens = [], []
        for (p0, p1), ks in zip(probs, key_slices):
            r0 = jnp.dot(p0, va0[ks], preferred_element_type=F32)
            r1 = jnp.dot(p1, va1[ks], preferred_element_type=F32)
            nums.append(jnp.where(head0, r0, r1))
            dens.append(pltpu.roll(jnp.where(head0, r1, r0), HEAD_DIM, 1))
        num = jnp.concatenate(nums, axis=0)
        den = jnp.concatenate(dens, axis=0)
        m = jnp.concatenate(ms, axis=0)
        if mode == "init":
            return [(num_acc, q_rows, num), (den_acc, q_rows, den), (m_acc, q_rows, m)]
        m_new = jnp.maximum(m_old, m)
        a = jnp.exp(m_old - m_new)
        b = jnp.exp(m - m_new)
        num = num_old * a + num * b
        den = den_old * a + den * b
        if mode == "merge":
            return [(num_acc, q_rows, num), (den_acc, q_rows, den), (m_acc, q_rows, m_new)]
        return [(o_ref.at[0], q_rows, num / den)]

    def run(groups):
        stages = [group(*g) for g in groups]
        for _ in range(2):
            for stage in stages:
                next(stage)
        stores = []
        for stage in stages:
            try:
                next(stage)
            except StopIteration as done:
                stores += done.value
        for ref, at, val in stores:
            ref[at, :] = val

    def branch(src, d, n_seq, sub_len, seq_stride, mode):
        nb = sub_len // blk
        u = min(ATT_UNROLL, nb)
        if nb == u:
            per_body = ATT_UNROLL // u

            def several(rr, carry):
                run([(src, d, (rr * per_body + t) * seq_stride, 0, u, True, mode)
                     for t in range(per_body)])
                return carry

            lax.fori_loop(0, n_seq // per_body, several, 0)
        else:
            def one(r, carry):
                run([(src, d, r * seq_stride, 0, u, True, mode)])

                def later_blocks(g, c):
                    run([(src, d, r * seq_stride, g * u, u, False, mode)])
                    return c

                lax.fori_loop(1, nb // u, later_blocks, 0)
                return carry

            lax.fori_loop(0, n_seq, one, 0)

    step = REGROUP_STEP
    wide = step * step
    assert DILATIONS == (1, step, wide)
    part = seq_len // wide

    def regroup(load, dst_ref):
        for a in range(step):
            tmp_ref[a * step * part:(a + 1) * step * part, :] = load(rows(a, step * part, step))
        for a in range(step):
            for b in range(step):
                s = a * step + b
                dst_ref[s * part:(s + 1) * part, :] = tmp_ref[rows(a * step * part + b, part, step), :]

    def ungroup(src_ref, dst_ref):
        for a in range(step):
            for b in range(step):
                s = a * step + b
                tmp_ref[rows(a * step * part + b, part, step), :] = src_ref[s * part:(s + 1) * part, :]
        for a in range(step):
            dst_ref[rows(a, step * part, step), :] = tmp_ref[a * step * part:(a + 1) * step * part, :]

    for load, dst in zip(natural[:3], (qd_ref, kd_ref, vd_ref)):
        regroup(load, dst)
    branch(regrouped, 1, wide, part, part, "init")
    for src, dst in zip(regrouped[3], natural[3]):
        ungroup(src, dst)
    branch(natural, 1, 1, seq_len, 0, "merge")
    branch(natural, step, step, seq_len // step, 1, "final")


def _attention(q, k, v):
    b, s, _ = q.shape
    spec = pl.BlockSpec((1, s, LANES), lambda i, p: (i, 0, p))
    return pl.pallas_call(
        functools.partial(_attn_kernel, seq_len=s),
        grid=(b, D_ATTN // LANES), in_specs=[spec] * 3, out_specs=spec,
        out_shape=jax.ShapeDtypeStruct((b, s, D_ATTN), F32),
        scratch_shapes=[pltpu.VMEM((s, LANES), F32)] * 3 + [pltpu.VMEM((ATT_BLOCK, 2 * ATT_BLOCK), F32)]
        + [pltpu.VMEM((s, LANES), F32)] * 7,
        compiler_params=_cparams(("arbitrary", "arbitrary")), name="attn_prompt",
    )(q, k, v)


def _attn_sample_kernel(q_ref, kn_ref, vn_ref, kc_ref, vc_ref, o_ref, *, w_buf, heads):
    age = w_buf - lax.broadcasted_iota(jnp.int32, (1, w_buf), 1)
    valid = [(age % d == 0) & (age <= d * ATT_BLOCK) for d in DILATIONS]
    diag = (lax.broadcasted_iota(jnp.int32, (HEAD_DIM, HEAD_DIM), 0)
            == lax.broadcasted_iota(jnp.int32, (HEAD_DIM, HEAD_DIM), 1))
    to_col = lambda row: jnp.sum(jnp.where(diag, row, 0.0), axis=1, keepdims=True)
    to_row = lambda col: jnp.sum(jnp.where(diag, col, 0.0), axis=0, keepdims=True)
    q_rows, kn_rows, vn_rows = q_ref[0], kn_ref[0], vn_ref[0]
    s_self_all = jnp.sum(kn_rows * q_rows, axis=1, keepdims=True)
    for h in range(heads):
        q = to_col(q_rows[h:h + 1])
        vn = to_col(vn_rows[h:h + 1])
        s = jnp.sum(kc_ref[0, h] * q, axis=0, keepdims=True)
        s_self = s_self_all[h:h + 1]
        parts = []
        for ok in valid:
            sd = jnp.where(ok, s, NEG)
            m = jnp.maximum(jnp.max(sd, axis=1, keepdims=True), s_self)
            p = jnp.exp(sd - m)
            p_self = jnp.exp(s_self - m)
            parts.append((p, p_self, jnp.sum(p, axis=1, keepdims=True) + p_self, m))
        m_all = jnp.maximum(jnp.maximum(parts[0][3], parts[1][3]), parts[2][3])
        w = jnp.zeros((1, w_buf), F32)
        w_self = jnp.zeros((1, 1), F32)
        den_t = jnp.zeros((1, 1), F32)
        for p, p_self, den, m in parts:
            c = jnp.exp(m - m_all)
            w = w + p * c
            w_self = w_self + p_self * c
            den_t = den_t + den * c
        num_t = jnp.sum(vc_ref[0, h] * w, axis=1, keepdims=True) + vn * w_self
        o_ref[0, h:h + 1, :] = to_row(num_t / den_t)


def _attention_sample(q, kn, vn, cache_kt, cache_vt):
    db, heads, _, w_buf = cache_kt.shape
    tok = pl.BlockSpec((1, heads, HEAD_DIM), lambda i: (i, 0, 0))
    cache = pl.BlockSpec((1, heads, HEAD_DIM, w_buf), lambda i: (i, 0, 0, 0))
    return pl.pallas_call(
        functools.partial(_attn_sample_kernel, w_buf=w_buf, heads=heads),
        grid=(db,), in_specs=[tok, tok, tok, cache, cache], out_specs=tok,
        out_shape=jax.ShapeDtypeStruct((db, heads, HEAD_DIM), F32),
        compiler_params=_cparams(("arbitrary",)), name="attn_sample",
    )(q, kn, vn, cache_kt, cache_vt)


def _outproj_kernel(attn_ref, bc_ref, x_ref, ga_ref, gc_ref, wo_ref, gf_ref, wr_ref, *rest,
                    hi, n_tiles):
    hx_ref, route_ref = rest[-2:]
    a = _rms(attn_ref[...], ga_ref[...])
    c = _rms(bc_ref[...], gc_ref[...])
    mix = _mm(a, wo_ref[0:D_ATTN, :], hi) + _mm(c, wo_ref[D_ATTN:D_MODEL, :], hi)
    h = x_ref[...] + mix
    hn = _rms(h, gf_ref[...])

    if hi:
        lg = _mm(hn, wr_ref[...], True)
    else:
        hn_hi = hn.astype(BF16)
        hn_lo = (hn - hn_hi.astype(F32)).astype(BF16)
        both = jnp.dot(hn_hi, wr_ref[...], preferred_element_type=F32)
        lg = (both[:, :LANES] + both[:, LANES:]
              + jnp.dot(hn_lo, wr_ref[:, :LANES], preferred_element_type=F32))
    lane_i = lax.broadcasted_iota(jnp.int32, lg.shape, 1)
    lane = lane_i.astype(F32)
    lane_group = (lane_i // EXPERTS_PER_GROUP).astype(F32)
    is_e = lane_i < N_EXPERTS
    is_g = (lane_i >= N_EXPERTS) & (lane_i < N_EXPERTS + N_GROUPS)
    big = float(LANES)
    first_at = lambda t, v: jnp.min(jnp.where(t == v, lane, big), axis=-1, keepdims=True)

    gl = jnp.where(is_g, lg, NEG)
    mg = jnp.max(gl, axis=-1, keepdims=True)
    p_g = 1.0 / jnp.sum(jnp.exp(gl - mg), axis=-1, keepdims=True)
    g_idx = first_at(gl, mg) - float(N_EXPERTS)
    el = jnp.where(is_e & (lane_group == g_idx), lg, NEG)
    v1 = jnp.max(el, axis=-1, keepdims=True)
    i1 = first_at(el, v1)
    el2 = jnp.where(lane == i1, NEG, el)
    v2 = jnp.max(el2, axis=-1, keepdims=True)
    i2 = first_at(el2, v2)
    e2 = jnp.exp(v2 - v1)
    w1 = p_g / (1.0 + e2)
    w2 = p_g * e2 / (1.0 + e2)
    route = sum(jnp.where(lane_i == j, val, 0.0) for j, val in enumerate([i1, i2, w1, w2]))
    route_ref[...] = route
    hx_ref[...] = h.reshape(h.shape[0], TOKEN_ROWS, LANES)

    @pl.when(pl.program_id(0) >= n_tiles)
    def _():
        hx_ref[...] = jnp.zeros(hx_ref.shape, F32)
        route_ref[...] = jnp.zeros(route_ref.shape, F32)


def _outproj(attn, bc, x, ga, gc, w_out, gf, w_router, *, tm, hi, n_rows, row0, into=None):
    n = x.shape[0]
    n_tiles = n // tm
    n_fill = 0 if into is not None else -(-(n_rows - n) // tm)
    row = lambda w: pl.BlockSpec((tm, w), lambda i: (jnp.minimum(i, n_tiles - 1), 0))
    full = lambda a: pl.BlockSpec(a.shape, lambda i: (0,) * a.ndim)
    in_specs = [row(D_ATTN), row(D_CONV), row(D_MODEL), full(ga), full(gc), full(w_out),
                full(gf), full(w_router)]
    args = [attn, bc, x, ga, gc, w_out, gf, w_router]
    aliases = {}
    if into is not None:
        in_specs += [pl.BlockSpec(memory_space=pl.ANY)] * 2
        args += list(into)
        aliases = {len(args) - 2: 0, len(args) - 1: 1}
    return pl.pallas_call(
        functools.partial(_outproj_kernel, hi=hi, n_tiles=n_tiles),
        grid=(n_tiles + n_fill,), in_specs=in_specs,
        out_specs=[pl.BlockSpec((tm, TOKEN_ROWS, LANES), lambda i: (i + row0 // tm, 0, 0)),
                   pl.BlockSpec((tm, LANES), lambda i: (i + row0 // tm, 0))],
        out_shape=[jax.ShapeDtypeStruct((n_rows, TOKEN_ROWS, LANES), F32),
                   jax.ShapeDtypeStruct((n_rows, LANES), F32)],
        input_output_aliases=aliases,
        compiler_params=_cparams(("arbitrary",)), name="outproj_hi" if hi else "outproj",
    )(*args)


def _lane_pack(values, width=LANES):
    rows = values[0].shape[0]
    lane = lax.broadcasted_iota(jnp.int32, (rows, width), 1)
    return sum(jnp.where(lane == j, v, 0.0) for j, v in enumerate(values))


def _plan_rank_kernel(route_ref, cr_ref, meta_ref, tiles_ref, run_ref, *, n_steps, n_tokens):
    i = pl.program_id(0)
    rows = route_ref.shape[0]
    e = float(EXPERTS_PER_GROUP)

    @pl.when(i == 0)
    def _():
        run_ref[...] = jnp.zeros(run_ref.shape, F32)

    r = route_ref[...]
    i1, i2 = r[:, 0:1], r[:, 1:2]
    grp = jnp.floor(i1 * (1.0 / e))
    cls = grp * (e * e) + (jnp.minimum(i1, i2) - grp * e) * e + (jnp.maximum(i1, i2) - grp * e)
    token = i * rows + lax.broadcasted_iota(jnp.int32, (rows, 1), 0)
    lane = lax.broadcasted_iota(jnp.int32, (rows, N_CLASSES), 1).astype(F32)
    onehot = (lane == cls) & (token < n_tokens)
    earlier = (lax.broadcasted_iota(jnp.int32, (rows, rows), 0)
               > lax.broadcasted_iota(jnp.int32, (rows, rows), 1))
    before = jnp.dot(jnp.where(earlier, 1.0, 0.0).astype(BF16),
                     jnp.where(onehot, 1.0, 0.0).astype(BF16), preferred_element_type=F32)
    rank = jnp.sum(jnp.where(onehot, before + run_ref[...], 0.0), axis=-1, keepdims=True)
    run_ref[...] += jnp.sum(jnp.where(onehot, 1.0, 0.0), axis=0, keepdims=True)
    cr_ref[...] = _lane_pack([cls, rank])

    @pl.when(i == n_steps - 1)
    def _():
        counts = run_ref[...]
        tiles = jnp.floor((counts + (MOE_TILE - 1.0)) * (1.0 / MOE_TILE))
        upto = jnp.where(lax.broadcasted_iota(jnp.int32, (N_CLASSES, N_CLASSES), 0)
                         <= lax.broadcasted_iota(jnp.int32, (N_CLASSES, N_CLASSES), 1), 1.0, 0.0)
        both = jnp.concatenate([tiles, counts] + [jnp.zeros_like(counts)] * 6, axis=0)
        ends = _mm(both, upto, True)
        tile_end, tok_end = ends[0:1], ends[1:2]
        tile_off, tok_off = tile_end - tiles, tok_end - counts
        row8 = lax.broadcasted_iota(jnp.int32, (SUBLANES, N_CLASSES), 0)
        meta_ref[...] = sum(jnp.where(row8 == j, v, 0.0)
                            for j, v in enumerate([counts, tile_off, tok_off, tile_end]))
        n_tiles = jnp.max(tile_end, axis=-1, keepdims=True)
        t = lax.broadcasted_iota(jnp.int32, (N_CLASSES, 1), 0).astype(F32)
        t = jnp.minimum(t, n_tiles - 1.0)
        cls_t = jnp.sum(jnp.where(tile_end <= t, 1.0, 0.0), axis=-1, keepdims=True)
        mine = lax.broadcasted_iota(jnp.int32, (N_CLASSES, N_CLASSES), 1).astype(F32) == cls_t
        pick = lambda v: jnp.sum(jnp.where(mine, v, 0.0), axis=-1, keepdims=True)
        base = pick(tok_off) + (t - pick(tile_off)) * MOE_TILE
        tiles_ref[...] = _lane_pack([cls_t, base, n_tiles + jnp.zeros_like(t)])


def _plan_slot_kernel(cr_ref, meta_ref, route_ref, sp_ref):
    cr = cr_ref[...]
    cls, rank = cr[:, 0:1], cr[:, 1:2]
    lane = lax.broadcasted_iota(jnp.int32, (cr.shape[0], N_CLASSES), 1).astype(F32)
    mine = lane == cls
    meta = meta_ref[...]
    pick = lambda v: jnp.sum(jnp.where(mine, v, 0.0), axis=-1, keepdims=True)
    r = route_ref[...]
    i1, i2, w1, w2 = r[:, 0:1], r[:, 1:2], r[:, 2:3], r[:, 3:4]
    first_is_a = i1 < i2
    sp_ref[...] = _lane_pack([pick(meta[1:2]) * MOE_TILE + rank, pick(meta[2:3]) + rank,
                              jnp.where(first_is_a, w1, w2), jnp.where(first_is_a, w2, w1)])


def _plan_order_kernel(pos_ref, order_ref, *, n_tokens):
    def place(t, carry):
        order_ref[pos_ref[t]] = t
        return carry

    lax.fori_loop(0, n_tokens, place, 0, unroll=8)
    for j in range(MOE_TILE):
        order_ref[n_tokens + j] = 0


def _route_plan(route, n_tokens):
    n_rows = route.shape[0]
    rows = PLAN_ROWS
    n_steps = n_rows // rows
    blk = pl.BlockSpec((rows, LANES), lambda i: (i, 0))
    whole = lambda r, c: pl.BlockSpec((r, c), lambda i: (0, 0))
    cr, meta, tiles = pl.pallas_call(
        functools.partial(_plan_rank_kernel, n_steps=n_steps, n_tokens=n_tokens),
        grid=(n_steps,), in_specs=[blk],
        out_specs=[blk, whole(SUBLANES, N_CLASSES), whole(N_CLASSES, LANES)],
        out_shape=[jax.ShapeDtypeStruct((n_rows, LANES), F32),
                   jax.ShapeDtypeStruct((SUBLANES, N_CLASSES), F32),
                   jax.ShapeDtypeStruct((N_CLASSES, LANES), F32)],
        scratch_shapes=[pltpu.VMEM((1, N_CLASSES), F32)],
        compiler_params=_cparams(("arbitrary",)), name="plan_rank",
    )(route)
    sp = pl.pallas_call(
        _plan_slot_kernel, grid=(n_steps,), in_specs=[blk, whole(SUBLANES, N_CLASSES), blk],
        out_specs=blk, out_shape=jax.ShapeDtypeStruct((n_rows, LANES), F32),
        compiler_params=_cparams(("arbitrary",)), name="plan_slot",
    )(cr, meta, route)
    token_slot = sp[:n_tokens, 0].astype(jnp.int32)
    token_pos = sp[:n_tokens, 1].astype(jnp.int32)
    order = pl.pallas_call(
        functools.partial(_plan_order_kernel, n_tokens=n_tokens),
        in_specs=[pl.BlockSpec(memory_space=pltpu.SMEM)],
        out_specs=pl.BlockSpec(memory_space=pltpu.SMEM),
        out_shape=jax.ShapeDtypeStruct((n_tokens + MOE_TILE,), jnp.int32), name="plan_order",
    )(token_pos)
    tile_cls = tiles[:, 0].astype(jnp.int32)
    tile_base = tiles[:, 1].astype(jnp.int32)
    n_tiles = tiles[0:1, 2].astype(jnp.int32)
    return tile_cls, tile_base, n_tiles, order, sp[:n_tokens, 2], sp[:n_tokens, 3], token_slot


def _moe_kernel(tcls_ref, tbase_ref, nt_ref, order_ref, wa_ref, wb_ref,
                hx_hbm, gf_ref, wg_ref, wu_ref, wd_ref, y_ref,
                xbuf, wgu_s, wd_s, sem):
    t = pl.program_id(0)
    n_tiles = nt_ref[0]
    buf = t % 2
    pair = EXPERTS_PER_GROUP * EXPERTS_PER_GROUP

    def gather(tile, b):
        base = tbase_ref[tile]
        for j in range(MOE_TILE):
            pltpu.make_async_copy(hx_hbm.at[order_ref[base + j]], xbuf.at[b, j], sem.at[b]).start()

    @pl.when(t == 0)
    def _():
        gather(0, 0)

    @pl.when(t + 1 < n_tiles)
    def _():
        gather(t + 1, 1 - buf)

    @pl.when(t < n_tiles)
    def _():
        cls = tcls_ref[t]
        grp = cls // pair

        @pl.when((t == 0) | (grp != tcls_ref[jnp.maximum(t - 1, 0)] // pair))
        def _():
            for e in range(EXPERTS_PER_GROUP):
                wgu_s[e, :, 0:D_EXPERT] = wg_ref[0, e].astype(BF16)
                wgu_s[e, :, D_EXPERT:2 * D_EXPERT] = wu_ref[0, e].astype(BF16)
                wd_s[e] = wd_ref[0, e].astype(BF16)

        pltpu.make_async_copy(hx_hbm.at[pl.ds(0, MOE_TILE)], xbuf.at[buf], sem.at[buf]).wait()
        h = xbuf[buf].reshape(MOE_TILE, D_MODEL)
        xn = _rms(h, gf_ref[...]).astype(BF16)
        ea = (cls % pair) // EXPERTS_PER_GROUP
        eb = cls % EXPERTS_PER_GROUP

        def column(w_ref):
            base = tbase_ref[t]
            sub = lax.broadcasted_iota(jnp.int32, (SUBLANES, 1), 0)
            chunks = []
            for c in range(MOE_TILE // SUBLANES):
                acc = jnp.zeros((SUBLANES, 1), F32)
                for k in range(SUBLANES):
                    acc = jnp.where(sub == k, w_ref[order_ref[base + c * SUBLANES + k]], acc)
                chunks.append(acc)
            return jnp.concatenate(chunks, axis=0)

        gus = [jnp.dot(xn, wgu_s[e], preferred_element_type=F32) for e in (ea, eb)]
        acts = [(jax.nn.silu(gu[:, 0:D_EXPERT]) * gu[:, D_EXPERT:2 * D_EXPERT]).astype(BF16)
                for gu in gus]
        oa, ob = [jnp.dot(act, wd_s[e], preferred_element_type=F32)
                  for act, e in zip(acts, (ea, eb))]
        y = h + column(wa_ref) * oa + column(wb_ref) * ob
        y_ref[...] = y.reshape(MOE_TILE, D_MODEL // LANES, LANES)

    @pl.when(t >= n_tiles)
    def _():
        y_ref[...] = jnp.zeros(y_ref.shape, F32)


def _moe(hx, gf, w_gate, w_up, w_down, plan, n_slots):
    tile_cls, tile_base, n_tiles, order, wa, wb, _ = plan
    pair = EXPERTS_PER_GROUP * EXPERTS_PER_GROUP
    grouped = lambda w: w.reshape(N_GROUPS, EXPERTS_PER_GROUP, *w.shape[1:])
    wspec = lambda r, c: pl.BlockSpec((1, EXPERTS_PER_GROUP, r, c),
                                      lambda t, tc, *_: (tc[t] // pair, 0, 0, 0),
                                      pipeline_mode=pl.Buffered(1))
    grid_spec = pltpu.PrefetchScalarGridSpec(
        num_scalar_prefetch=6, grid=(n_slots // MOE_TILE,),
        in_specs=[pl.BlockSpec(memory_space=pl.ANY),
                  pl.BlockSpec(gf.shape, lambda t, *_: (0, 0)),
                  wspec(D_MODEL, D_EXPERT), wspec(D_MODEL, D_EXPERT), wspec(D_EXPERT, D_MODEL)],
        out_specs=pl.BlockSpec((MOE_TILE, D_MODEL // LANES, LANES), lambda t, *_: (t, 0, 0)),
        scratch_shapes=[pltpu.VMEM((2, MOE_TILE, TOKEN_ROWS, LANES), F32),
                        pltpu.VMEM((EXPERTS_PER_GROUP, D_MODEL, 2 * D_EXPERT), BF16),
                        pltpu.VMEM((EXPERTS_PER_GROUP, D_EXPERT, D_MODEL), BF16),
                        pltpu.SemaphoreType.DMA((2,))])
    return pl.pallas_call(
        _moe_kernel, grid_spec=grid_spec,
        out_shape=jax.ShapeDtypeStruct((n_slots, D_MODEL // LANES, LANES), F32),
        compiler_params=_cparams(("arbitrary",)), name="moe",
    )(tile_cls, tile_base, n_tiles, order, wa, wb, hx, gf,
      grouped(w_gate), grouped(w_up), grouped(w_down))


def _unsort_kernel(slot_ref, ys_hbm, y_ref, buf, sem, *, rows, row0):
    i = pl.program_id(0)
    b = i % 2

    def gather(step, dst):
        for j in range(rows):
            pltpu.make_async_copy(ys_hbm.at[slot_ref[row0 + step * rows + j]], buf.at[dst, j],
                                  sem.at[dst]).start()

    @pl.when(i == 0)
    def _():
        gather(0, 0)

    @pl.when(i + 1 < pl.num_programs(0))
    def _():
        gather(i + 1, 1 - b)

    pltpu.make_async_copy(ys_hbm.at[pl.ds(0, rows)], buf.at[b], sem.at[b]).wait()
    y_ref[...] = buf[b].reshape(rows, D_MODEL)


def _unsort(ys, token_slot, *, n, row0, rows):
    grid_spec = pltpu.PrefetchScalarGridSpec(
        num_scalar_prefetch=1, grid=(n // rows,),
        in_specs=[pl.BlockSpec(memory_space=pl.ANY)],
        out_specs=pl.BlockSpec((rows, D_MODEL), lambda i, *_: (i, 0)),
        scratch_shapes=[pltpu.VMEM((2, rows, D_MODEL // LANES, LANES), F32),
                        pltpu.SemaphoreType.DMA((2,))])
    return pl.pallas_call(
        functools.partial(_unsort_kernel, rows=rows, row0=row0), grid_spec=grid_spec,
        out_shape=jax.ShapeDtypeStruct((n, D_MODEL), F32),
        compiler_params=_cparams(("arbitrary",)), name="unsort",
    )(token_slot, ys)


def kernel(x_prompt, x_sample, cache_k, cache_v, state_conv, g_norm_mix, w_in, g_q, g_k, w_conv,
           g_attn_out, g_conv_out, w_out, g_norm_ffn, w_router_group, w_router_expert,
           w_gate, w_up, w_down):
    depth = g_norm_mix.shape[0]
    assert depth == 1
    nb, seq, _ = x_prompt.shape
    db, dec_seq, _ = x_sample.shape
    assert dec_seq == 1
    l = 0

    gmix = g_norm_mix[l][None, :]
    gq = jnp.tile(g_q[l], N_HEADS)[None, :]
    gk = jnp.tile(g_k[l], N_HEADS)[None, :]
    ga = g_attn_out[l][None, :]
    gc = g_conv_out[l][None, :]
    gf = g_norm_ffn[l][None, :]
    w_router = jnp.concatenate(
        [w_router_expert[l], w_router_group[l],
         jnp.zeros((D_MODEL, LANES - N_EXPERTS - N_GROUPS), F32)], axis=1)
    w_router_hi = w_router.astype(BF16)
    w_router_split = jnp.concatenate(
        [w_router_hi, (w_router - w_router_hi.astype(F32)).astype(BF16)], axis=1)

    cos_p, sin_p = _rope_tables(np.arange(seq))
    xp = x_prompt.reshape(nb * seq, D_MODEL)
    q, k, v, bc, tail, kt, vt = _inproj(xp, gmix, w_in[l].astype(BF16), gq, gk, cos_p, sin_p,
                                        w_conv[l], tm=256, seq_len=seq, hi=False)
    attn = _attention(q.reshape(nb, seq, D_ATTN), k.reshape(nb, seq, D_ATTN),
                      v.reshape(nb, seq, D_ATTN))
    n_prompt = nb * seq
    n_total = n_prompt + db
    n_rows = -(-n_total // TOKEN_PAD) * TOKEN_PAD
    hx, route = _outproj(attn.reshape(n_prompt, D_ATTN), bc, xp, ga, gc, w_out[l].astype(BF16),
                         gf, w_router_split, tm=256, hi=False, n_rows=n_rows, row0=0)
    k_prompt = jnp.transpose(kt, (0, 3, 1, 2))
    v_prompt = jnp.transpose(vt, (0, 3, 1, 2))
    conv_prompt = tail.reshape(nb, SUBLANES, D_CONV)[:, SUBLANES - (CONV_WIDTH - 1):]

    cos_s, sin_s = _rope_tables(np.full((db,), PAST_LEN))
    xs = x_sample.reshape(db, D_MODEL)
    st = state_conv[l]
    qs, ks, vs, bcs, us = _inproj(xs, gmix, w_in[l], gq, gk, cos_s, sin_s, w_conv[l],
                                  tm=db, seq_len=db, hi=True, prev=(st[:, 1], st[:, 0]))
    heads = lambda t: t.reshape(db, N_HEADS, HEAD_DIM)
    attn_s = _attention_sample(heads(qs), heads(ks), heads(vs),
                               jnp.transpose(cache_k[l], (0, 2, 3, 1)),
                               jnp.transpose(cache_v[l], (0, 2, 3, 1)))
    hx, route = _outproj(attn_s.reshape(db, D_ATTN), bcs, xs, ga, gc, w_out[l], gf, w_router,
                         tm=db, hi=True, n_rows=n_rows, row0=n_prompt, into=(hx, route))

    n_cls = N_GROUPS * EXPERTS_PER_GROUP * (EXPERTS_PER_GROUP - 1) // 2
    n_slots = -(-(n_total + n_cls * (MOE_TILE - 1)) // MOE_TILE) * MOE_TILE
    assert n_slots // MOE_TILE <= N_CLASSES and n_rows % PLAN_ROWS == 0
    plan = _route_plan(route, n_total)
    ys = _moe(hx, gf, w_gate[l], w_up[l], w_down[l], plan, n_slots)
    y_prompt = _unsort(ys, plan[-1], n=n_prompt, row0=0, rows=256).reshape(nb, seq, D_MODEL)
    y_sample = _unsort(ys, plan[-1], n=db, row0=n_prompt, rows=db).reshape(db, 1, D_MODEL)
    k_sample = ks.reshape(db, 1, N_HEADS, HEAD_DIM)
    v_sample = vs.reshape(db, 1, N_HEADS, HEAD_DIM)
    conv_sample = jnp.stack([st[:, 1], us], axis=1)

    return (y_prompt, y_sample, k_prompt[None], v_prompt[None], conv_prompt[None],
            k_sample[None], v_sample[None], conv_sample[None])
```

```python
import functools

import numpy as np
import jax
import jax.numpy as jnp
from jax import lax
from jax.experimental import pallas as pl
from jax.experimental.pallas import tpu as pltpu

F32 = jnp.float32
BF16 = jnp.bfloat16

D_MODEL = 1024
HEAD_DIM = 64
N_HEADS = 12
D_ATTN = N_HEADS * HEAD_DIM
D_CONV = D_MODEL - D_ATTN
D_IN_PROJ = 3 * D_ATTN + 3 * D_CONV
CONV_WIDTH = 3
DILATIONS = (1, 4, 16)
ATT_BLOCK = 128
ATT_UNROLL = 8
REGROUP_STEP = 4
MAX_WINDOW = 2048
PAST_LEN = 8192
ATTN_SCALE = HEAD_DIM ** -0.5
ROPE_THETA = 10000.0
EPS = 1e-6
N_GROUPS = 4
EXPERTS_PER_GROUP = 8
N_EXPERTS = N_GROUPS * EXPERTS_PER_GROUP
D_EXPERT = 256
MOE_TILE = 192
N_CLASSES = N_GROUPS * EXPERTS_PER_GROUP * EXPERTS_PER_GROUP
PLAN_ROWS = 768
TOKEN_PAD = 512

LANES = 128
SUBLANES = 8
MXU_DIM = 256
TOKEN_ROWS = D_MODEL // LANES
NEG = -1e30
VMEM_LIMIT = 48 * 1024 * 1024


def _cparams(sem):
    return pltpu.CompilerParams(dimension_semantics=sem, vmem_limit_bytes=VMEM_LIMIT)


def _mm(a, b, hi):
    if hi:
        return jnp.dot(a, b, preferred_element_type=F32, precision=lax.Precision.HIGHEST)
    return jnp.dot(a.astype(BF16), b.astype(BF16), preferred_element_type=F32)


def _rms(x, g):
    return x * lax.rsqrt(jnp.mean(x * x, axis=-1, keepdims=True) + EPS) * g


def _rope_tables(pos):
    half = HEAD_DIM // 2
    inv = ROPE_THETA ** (-np.arange(half, dtype=np.float64) / half)
    ang = np.asarray(pos, np.float64)[:, None] * inv[None, :]
    cos, sin = np.cos(ang), np.sin(ang)
    cos2 = np.concatenate([cos, cos, cos, cos], axis=-1)
    sin2 = np.concatenate([-sin, sin, -sin, sin], axis=-1)
    return jnp.asarray(cos2, F32), jnp.asarray(sin2, F32)


def _inproj_kernel(*refs, tm, tiles_per_seq, keep_tiles, hi, given_prev):
    if given_prev:
        (x_ref, gmix_ref, w_ref, gq_ref, gk_ref, cos_ref, sin_ref, wconv_ref, p1_ref, p2_ref,
         q_ref, k_ref, v_ref, bc_ref, u_ref) = refs
    else:
        (x_ref, gmix_ref, w_ref, gq_ref, gk_ref, cos_ref, sin_ref, wconv_ref,
         q_ref, k_ref, v_ref, bc_ref, tail_ref, kt_ref, vt_ref, ubuf) = refs

    xn = _rms(x_ref[...], gmix_ref[...])
    if not hi:
        xn = xn.astype(BF16)

    r_i = lax.broadcasted_iota(jnp.int32, (MXU_DIM, MXU_DIM), 0) // HEAD_DIM
    c_i = lax.broadcasted_iota(jnp.int32, (MXU_DIM, MXU_DIM), 1) // HEAD_DIM
    head_sum = jnp.where(r_i == c_i, 1.0, 0.0).astype(F32 if hi else BF16)

    reps = D_ATTN // LANES
    cos = jnp.concatenate([cos_ref[...]] * reps, axis=-1)
    sin = jnp.concatenate([sin_ref[...]] * reps, axis=-1)
    lane = lax.broadcasted_iota(jnp.int32, (1, D_ATTN), 1)
    first_half = (lane % HEAD_DIM) < (HEAD_DIM // 2)

    def head_ssq(z):
        sq = z * z
        return jnp.concatenate(
            [_mm(sq[:, c * MXU_DIM:(c + 1) * MXU_DIM], head_sum, hi)
             for c in range(D_ATTN // MXU_DIM)], axis=-1)

    def norm_rope(z, ssq, g_ref):
        zn = z * lax.rsqrt(ssq * (1.0 / HEAD_DIM) + EPS) * g_ref[...]
        partner = jnp.where(first_half,
                            pltpu.roll(zn, D_ATTN - HEAD_DIM // 2, 1),
                            pltpu.roll(zn, HEAD_DIM // 2, 1))
        return zn * cos + partner * sin

    zq = _mm(xn, w_ref[:, 0:D_ATTN], hi)
    zk = _mm(xn, w_ref[:, D_ATTN:2 * D_ATTN], hi)
    ssq_q = head_ssq(zq)
    v = _mm(xn, w_ref[:, 2 * D_ATTN:3 * D_ATTN], hi)
    ssq_k = head_ssq(zk)
    c0 = 3 * D_ATTN
    b_gate = _mm(xn, w_ref[:, c0:c0 + D_CONV], hi)
    c_gate = _mm(xn, w_ref[:, c0 + D_CONV:c0 + 2 * D_CONV], hi)
    u_in = _mm(xn, w_ref[:, c0 + 2 * D_CONV:c0 + 3 * D_CONV], hi)
    q_ref[...] = norm_rope(zq, ssq_q, gq_ref) * ATTN_SCALE
    k = norm_rope(zk, ssq_k, gk_ref)
    k_ref[...] = k
    v_ref[...] = v
    if not given_prev:
        @pl.when(pl.program_id(0) % tiles_per_seq >= tiles_per_seq - keep_tiles)
        def _():
            kt_ref[0] = k.T.reshape(N_HEADS, HEAD_DIM, tm)
            vt_ref[0] = v.T.reshape(N_HEADS, HEAD_DIM, tm)
    u = c_gate * u_in

    if given_prev:
        u1, u2 = p1_ref[...], p2_ref[...]
        u_ref[...] = u
    else:
        i = pl.program_id(0)

        @pl.when(i % tiles_per_seq == 0)
        def _():
            ubuf[0:SUBLANES, :] = jnp.zeros((SUBLANES, D_CONV), F32)

        @pl.when(i % tiles_per_seq != 0)
        def _():
            ubuf[0:SUBLANES, :] = ubuf[tm:tm + SUBLANES, :]

        ubuf[SUBLANES:tm + SUBLANES, :] = u
        u1 = ubuf[SUBLANES - 1:tm + SUBLANES - 1, :]
        u2 = ubuf[SUBLANES - 2:tm + SUBLANES - 2, :]
        tail_ref[...] = ubuf[tm:tm + SUBLANES, :]

    wc = wconv_ref[...]
    conv = wc[0:1, :] * u2 + wc[1:2, :] * u1 + wc[2:3, :] * u
    bc_ref[...] = b_gate * conv


def _inproj(x, gmix, w_in, gq, gk, cos, sin, w_conv, *, tm, seq_len, hi, prev=None):
    n = x.shape[0]
    nt = n // tm
    tiles_per_seq = seq_len // tm
    row = lambda w: pl.BlockSpec((tm, w), lambda i: (i, 0))
    full = lambda a: pl.BlockSpec(a.shape, lambda i: (0,) * a.ndim)
    tab = pl.BlockSpec((tm, LANES), lambda i: (i % tiles_per_seq, 0))
    in_specs = [row(D_MODEL), full(gmix), full(w_in), full(gq), full(gk), tab, tab, full(w_conv)]
    args = [x, gmix, w_in, gq, gk, cos, sin, w_conv]
    out_shape = [jax.ShapeDtypeStruct((n, D_ATTN), F32)] * 3 + [jax.ShapeDtypeStruct((n, D_CONV), F32)]
    out_specs = [row(D_ATTN)] * 3 + [row(D_CONV)]
    scratch = []
    keep_tiles = min(MAX_WINDOW, seq_len) // tm
    if prev is not None:
        in_specs += [row(D_CONV), row(D_CONV)]
        args += list(prev)
        out_shape.append(jax.ShapeDtypeStruct((n, D_CONV), F32))
        out_specs.append(row(D_CONV))
    else:
        n_seq = n // seq_len
        out_shape.append(jax.ShapeDtypeStruct((n_seq * SUBLANES, D_CONV), F32))
        out_specs.append(pl.BlockSpec((SUBLANES, D_CONV), lambda i: (i // tiles_per_seq, 0)))
        kept = pl.BlockSpec(
            (1, N_HEADS, HEAD_DIM, tm),
            lambda i: (i // tiles_per_seq, 0, 0,
                       jnp.maximum(i % tiles_per_seq - (tiles_per_seq - keep_tiles), 0)))
        out_shape += [jax.ShapeDtypeStruct((n_seq, N_HEADS, HEAD_DIM, keep_tiles * tm), F32)] * 2
        out_specs += [kept, kept]
        scratch.append(pltpu.VMEM((tm + SUBLANES, D_CONV), F32))
    return pl.pallas_call(
        functools.partial(_inproj_kernel, tm=tm, tiles_per_seq=tiles_per_seq,
                          keep_tiles=keep_tiles, hi=hi, given_prev=prev is not None),
        grid=(nt,), in_specs=in_specs, out_specs=out_specs, out_shape=out_shape,
        scratch_shapes=scratch, compiler_params=_cparams(("arbitrary",)),
        name="inproj_hi" if hi else "inproj",
    )(*args)


def _attn_kernel(q_ref, k_ref, v_ref, o_ref, num_ref, den_ref, m_ref, bias_ref,
                 qd_ref, kd_ref, vd_ref, numd_ref, dend_ref, md_ref, tmp_ref, *, seq_len):
    blk = ATT_BLOCK
    qi = lax.broadcasted_iota(jnp.int32, (blk, 2 * blk), 0)
    kj = lax.broadcasted_iota(jnp.int32, (blk, 2 * blk), 1)
    bias_ref[...] = jnp.where((kj >= qi) & (kj <= qi + blk), 0.0, NEG)
    head0 = lax.broadcasted_iota(jnp.int32, (1, LANES), 1) < HEAD_DIM
    dn = (((1,), (1,)), ((), ()))

    def rows(start, size, d):
        return pl.ds(start, size) if d == 1 else pl.ds(start, size, stride=d)

    natural = (lambda at: q_ref[0, at, :], lambda at: k_ref[0, at, :], lambda at: v_ref[0, at, :],
               (num_ref, den_ref, m_ref))
    regrouped = (lambda at: qd_ref[at, :], lambda at: kd_ref[at, :], lambda at: vd_ref[at, :],
                 (numd_ref, dend_ref, md_ref))

    def group(src, d, r, n0, u, first, mode):
        q_at, k_at, v_at, (num_acc, den_acc, m_acc) = src
        q_rows = rows(n0 * (blk * d) + r, u * blk, d)
        if first:
            k_rows = rows(r, u * blk, d)
        else:
            k_rows = rows((n0 - 1) * (blk * d) + r, (u + 1) * blk, d)
        qa = q_at(q_rows)
        ka = k_at(k_rows).astype(BF16)
        va = v_at(k_rows)
        qa0 = jnp.where(head0, qa, 0.0).astype(BF16)
        qa1 = jnp.where(head0, 0.0, qa).astype(BF16)
        va0 = jnp.where(head0, va, 1.0).astype(BF16)
        va1 = jnp.where(head0, 1.0, va).astype(BF16)
        if mode != "init":
            m_old = m_acc[q_rows, :]
            num_old = num_acc[q_rows, :]
            den_old = den_acc[q_rows, :]
        key_slices, scores = [], []
        for i in range(u):
            qs = slice(i * blk, (i + 1) * blk)
            if first and i == 0:
                ks = slice(0, blk)
                bias = bias_ref[:, blk:2 * blk]
            else:
                k0 = (i - 1) * blk if first else i * blk
                ks = slice(k0, k0 + 2 * blk)
                bias = bias_ref[...]
            key_slices.append(ks)
            scores.append(
                (lax.dot_general(qa0[qs], ka[ks], dn, preferred_element_type=F32) + bias,
                 lax.dot_general(qa1[qs], ka[ks], dn, preferred_element_type=F32) + bias))
        yield
        probs, ms = [], []
        for s0, s1 in scores:
            m0 = jnp.max(s0, axis=-1, keepdims=True)
            m1 = jnp.max(s1, axis=-1, keepdims=True)
            probs.append((jnp.exp(s0 - m0).astype(BF16), jnp.exp(s1 - m1).astype(BF16)))
            ms.append(jnp.where(head0, m0, m1))
        yield
        nums, dens = [], []
        for (p0, p1), ks in zip(probs, key_slices):
            r0 = jnp.dot(p0, va0[ks], preferred_element_type=F32)
            r1 = jnp.dot(p1, va1[ks], preferred_element_type=F32)
            nums.append(jnp.where(head0, r0, r1))
            dens.append(pltpu.roll(jnp.where(head0, r1, r0), HEAD_DIM, 1))
        num = jnp.concatenate(nums, axis=0)
        den = jnp.concatenate(dens, axis=0)
        m = jnp.concatenate(ms, axis=0)
        if mode == "init":
            return [(num_acc, q_rows, num), (den_acc, q_rows, den), (m_acc, q_rows, m)]
        m_new = jnp.maximum(m_old, m)
        a = jnp.exp(m_old - m_new)
        b = jnp.exp(m - m_new)
        num = num_old * a + num * b
        den = den_old * a + den * b
        if mode == "merge":
            return [(num_acc, q_rows, num), (den_acc, q_rows, den), (m_acc, q_rows, m_new)]
        return [(o_ref.at[0], q_rows, num / den)]

    def run(groups):
        stages = [group(*g) for g in groups]
        for _ in range(2):
            for stage in stages:
                next(stage)
        stores = []
        for stage in stages:
            try:
                next(stage)
            except StopIteration as done:
                stores += done.value
        for ref, at, val in stores:
            ref[at, :] = val

    def branch(src, d, n_seq, sub_len, seq_stride, mode):
        nb = sub_len // blk
        u = min(ATT_UNROLL, nb)
        if nb == u:
            per_body = ATT_UNROLL // u

            def several(rr, carry):
                run([(src, d, (rr * per_body + t) * seq_stride, 0, u, True, mode)
                     for t in range(per_body)])
                return carry

            lax.fori_loop(0, n_seq // per_body, several, 0)
        else:
            def one(r, carry):
                run([(src, d, r * seq_stride, 0, u, True, mode)])

                def later_blocks(g, c):
                    run([(src, d, r * seq_stride, g * u, u, False, mode)])
                    return c

                lax.fori_loop(1, nb // u, later_blocks, 0)
                return carry

            lax.fori_loop(0, n_seq, one, 0)

    step = REGROUP_STEP
    wide = step * step
    assert DILATIONS == (1, step, wide)
    part = seq_len // wide

    def regroup(load, dst_ref):
        for a in range(step):
            tmp_ref[a * step * part:(a + 1) * step * part, :] = load(rows(a, step * part, step))
        for a in range(step):
            for b in range(step):
                s = a * step + b
                dst_ref[s * part:(s + 1) * part, :] = tmp_ref[rows(a * step * part + b, part, step), :]

    def ungroup(src_ref, dst_ref):
        for a in range(step):
            for b in range(step):
                s = a * step + b
                tmp_ref[rows(a * step * part + b, part, step), :] = src_ref[s * part:(s + 1) * part, :]
        for a in range(step):
            dst_ref[rows(a, step * part, step), :] = tmp_ref[a * step * part:(a + 1) * step * part, :]

    for load, dst in zip(natural[:3], (qd_ref, kd_ref, vd_ref)):
        regroup(load, dst)
    branch(regrouped, 1, wide, part, part, "init")
    for src, dst in zip(regrouped[3], natural[3]):
        ungroup(src, dst)
    branch(natural, 1, 1, seq_len, 0, "merge")
    branch(natural, step, step, seq_len // step, 1, "final")


def _attention(q, k, v):
    b, s, _ = q.shape
    spec = pl.BlockSpec((1, s, LANES), lambda i, p: (i, 0, p))
    return pl.pallas_call(
        functools.partial(_attn_kernel, seq_len=s),
        grid=(b, D_ATTN // LANES), in_specs=[spec] * 3, out_specs=spec,
        out_shape=jax.ShapeDtypeStruct((b, s, D_ATTN), F32),
        scratch_shapes=[pltpu.VMEM((s, LANES), F32)] * 3 + [pltpu.VMEM((ATT_BLOCK, 2 * ATT_BLOCK), F32)]
        + [pltpu.VMEM((s, LANES), F32)] * 7,
        compiler_params=_cparams(("arbitrary", "arbitrary")), name="attn_prompt",
    )(q, k, v)


def _attn_sample_kernel(q_ref, kn_ref, vn_ref, kc_ref, vc_ref, o_ref, *, w_buf, heads):
    age = w_buf - lax.broadcasted_iota(jnp.int32, (1, w_buf), 1)
    valid = [(age % d == 0) & (age <= d * ATT_BLOCK) for d in DILATIONS]
    diag = (lax.broadcasted_iota(jnp.int32, (HEAD_DIM, HEAD_DIM), 0)
            == lax.broadcasted_iota(jnp.int32, (HEAD_DIM, HEAD_DIM), 1))
    to_col = lambda row: jnp.sum(jnp.where(diag, row, 0.0), axis=1, keepdims=True)
    to_row = lambda col: jnp.sum(jnp.where(diag, col, 0.0), axis=0, keepdims=True)
    q_rows, kn_rows, vn_rows = q_ref[0], kn_ref[0], vn_ref[0]
    s_self_all = jnp.sum(kn_rows * q_rows, axis=1, keepdims=True)
    for h in range(heads):
        q = to_col(q_rows[h:h + 1])
        vn = to_col(vn_rows[h:h + 1])
        s = jnp.sum(kc_ref[0, h] * q, axis=0, keepdims=True)
        s_self = s_self_all[h:h + 1]
        parts = []
        for ok in valid:
            sd = jnp.where(ok, s, NEG)
            m = jnp.maximum(jnp.max(sd, axis=1, keepdims=True), s_self)
            p = jnp.exp(sd - m)
            p_self = jnp.exp(s_self - m)
            parts.append((p, p_self, jnp.sum(p, axis=1, keepdims=True) + p_self, m))
        m_all = jnp.maximum(jnp.maximum(parts[0][3], parts[1][3]), parts[2][3])
        w = jnp.zeros((1, w_buf), F32)
        w_self = jnp.zeros((1, 1), F32)
        den_t = jnp.zeros((1, 1), F32)
        for p, p_self, den, m in parts:
            c = jnp.exp(m - m_all)
            w = w + p * c
            w_self = w_self + p_self * c
            den_t = den_t + den * c
        num_t = jnp.sum(vc_ref[0, h] * w, axis=1, keepdims=True) + vn * w_self
        o_ref[0, h:h + 1, :] = to_row(num_t / den_t)


def _attention_sample(q, kn, vn, cache_kt, cache_vt):
    db, heads, _, w_buf = cache_kt.shape
    tok = pl.BlockSpec((1, heads, HEAD_DIM), lambda i: (i, 0, 0))
    cache = pl.BlockSpec((1, heads, HEAD_DIM, w_buf), lambda i: (i, 0, 0, 0))
    return pl.pallas_call(
        functools.partial(_attn_sample_kernel, w_buf=w_buf, heads=heads),
        grid=(db,), in_specs=[tok, tok, tok, cache, cache], out_specs=tok,
        out_shape=jax.ShapeDtypeStruct((db, heads, HEAD_DIM), F32),
        compiler_params=_cparams(("arbitrary",)), name="attn_sample",
    )(q, kn, vn, cache_kt, cache_vt)


def _outproj_kernel(attn_ref, bc_ref, x_ref, ga_ref, gc_ref, wo_ref, gf_ref, wr_ref, *rest,
                    hi, n_tiles, n_parts):
    hx_ref, route_ref = rest[-2:]
    part_rows = x_ref.shape[0] // n_parts

    def part(k):
        rows = pl.ds(k * part_rows, part_rows)
        a = _rms(attn_ref[rows, :], ga_ref[...])
        c = _rms(bc_ref[rows, :], gc_ref[...])
        mix = _mm(a, wo_ref[0:D_ATTN, :], hi) + _mm(c, wo_ref[D_ATTN:D_MODEL, :], hi)
        yield
        h = x_ref[rows, :] + mix
        hn = _rms(h, gf_ref[...])
        if hi:
            lg = _mm(hn, wr_ref[...], True)
        else:
            hn_hi = hn.astype(BF16)
            hn_lo = (hn - hn_hi.astype(F32)).astype(BF16)
            both = jnp.dot(hn_hi, wr_ref[...], preferred_element_type=F32)
            lg = (both[:, :LANES] + both[:, LANES:]
                  + jnp.dot(hn_lo, wr_ref[:, :LANES], preferred_element_type=F32))
        yield
        lane_i = lax.broadcasted_iota(jnp.int32, lg.shape, 1)
        lane = lane_i.astype(F32)
        lane_group = (lane_i // EXPERTS_PER_GROUP).astype(F32)
        is_e = lane_i < N_EXPERTS
        is_g = (lane_i >= N_EXPERTS) & (lane_i < N_EXPERTS + N_GROUPS)
        big = float(LANES)
        first_at = lambda t, v: jnp.min(jnp.where(t == v, lane, big), axis=-1, keepdims=True)

        gl = jnp.where(is_g, lg, NEG)
        mg = jnp.max(gl, axis=-1, keepdims=True)
        p_g = 1.0 / jnp.sum(jnp.exp(gl - mg), axis=-1, keepdims=True)
        g_idx = first_at(gl, mg) - float(N_EXPERTS)
        el = jnp.where(is_e & (lane_group == g_idx), lg, NEG)
        v1 = jnp.max(el, axis=-1, keepdims=True)
        i1 = first_at(el, v1)
        el2 = jnp.where(lane == i1, NEG, el)
        v2 = jnp.max(el2, axis=-1, keepdims=True)
        i2 = first_at(el2, v2)
        e2 = jnp.exp(v2 - v1)
        w1 = p_g / (1.0 + e2)
        w2 = p_g * e2 / (1.0 + e2)
        route_ref[rows, :] = sum(jnp.where(lane_i == j, val, 0.0)
                                 for j, val in enumerate([i1, i2, w1, w2]))
        hx_ref[rows, :, :] = h.reshape(part_rows, TOKEN_ROWS, LANES)

    stages = [part(k) for k in range(n_parts)]
    for _ in range(2):
        for stage in stages:
            next(stage)
    for stage in stages:
        for _ in stage:
            pass

    @pl.when(pl.program_id(0) >= n_tiles)
    def _():
        hx_ref[...] = jnp.zeros(hx_ref.shape, F32)
        route_ref[...] = jnp.zeros(route_ref.shape, F32)


def _outproj(attn, bc, x, ga, gc, w_out, gf, w_router, *, tm, hi, n_rows, row0, n_parts=1,
             into=None):
    n = x.shape[0]
    n_tiles = n // tm
    n_fill = 0 if into is not None else -(-(n_rows - n) // tm)
    row = lambda w: pl.BlockSpec((tm, w), lambda i: (jnp.minimum(i, n_tiles - 1), 0))
    full = lambda a: pl.BlockSpec(a.shape, lambda i: (0,) * a.ndim)
    in_specs = [row(D_ATTN), row(D_CONV), row(D_MODEL), full(ga), full(gc), full(w_out),
                full(gf), full(w_router)]
    args = [attn, bc, x, ga, gc, w_out, gf, w_router]
    aliases = {}
    if into is not None:
        in_specs += [pl.BlockSpec(memory_space=pl.ANY)] * 2
        args += list(into)
        aliases = {len(args) - 2: 0, len(args) - 1: 1}
    return pl.pallas_call(
        functools.partial(_outproj_kernel, hi=hi, n_tiles=n_tiles, n_parts=n_parts),
        grid=(n_tiles + n_fill,), in_specs=in_specs,
        out_specs=[pl.BlockSpec((tm, TOKEN_ROWS, LANES), lambda i: (i + row0 // tm, 0, 0)),
                   pl.BlockSpec((tm, LANES), lambda i: (i + row0 // tm, 0))],
        out_shape=[jax.ShapeDtypeStruct((n_rows, TOKEN_ROWS, LANES), F32),
                   jax.ShapeDtypeStruct((n_rows, LANES), F32)],
        input_output_aliases=aliases,
        compiler_params=_cparams(("arbitrary",)), name="outproj_hi" if hi else "outproj",
    )(*args)


def _to_lanes(col):
    n = col.shape[0]
    diag = lax.broadcasted_iota(jnp.int32, (n, n), 0) == lax.broadcasted_iota(jnp.int32, (n, n), 1)
    return jnp.sum(jnp.where(diag, col, 0.0), axis=0, keepdims=True)


def _to_sublanes(row):
    n = row.shape[1]
    diag = lax.broadcasted_iota(jnp.int32, (n, n), 0) == lax.broadcasted_iota(jnp.int32, (n, n), 1)
    return jnp.sum(jnp.where(diag, row, 0.0), axis=1, keepdims=True)


def _plan_kernel(route_ref, slot_ref, pos_ref, wa_ref, wb_ref, tiles_ref, run_ref, cls_s, rank_s,
                 *, n_steps, n_tokens):
    i = pl.program_id(0)
    rows = route_ref.shape[0]
    e = float(EXPERTS_PER_GROUP)

    @pl.when(i == 0)
    def _():
        run_ref[...] = jnp.zeros(run_ref.shape, F32)

    r = route_ref[...]
    i1, i2, w1, w2 = r[:, 0:1], r[:, 1:2], r[:, 2:3], r[:, 3:4]
    grp = jnp.floor(i1 * (1.0 / e))
    cls = grp * (e * e) + (jnp.minimum(i1, i2) - grp * e) * e + (jnp.maximum(i1, i2) - grp * e)
    token = i * rows + lax.broadcasted_iota(jnp.int32, (rows, 1), 0)
    lane = lax.broadcasted_iota(jnp.int32, (rows, N_CLASSES), 1).astype(F32)
    onehot = (lane == cls) & (token < n_tokens)
    earlier = (lax.broadcasted_iota(jnp.int32, (rows, rows), 0)
               > lax.broadcasted_iota(jnp.int32, (rows, rows), 1))
    before = jnp.dot(jnp.where(earlier, 1.0, 0.0).astype(BF16),
                     jnp.where(onehot, 1.0, 0.0).astype(BF16), preferred_element_type=F32)
    rank = jnp.sum(jnp.where(onehot, before + run_ref[...], 0.0), axis=-1, keepdims=True)
    run_ref[...] += jnp.sum(jnp.where(onehot, 1.0, 0.0), axis=0, keepdims=True)
    first_is_a = i1 < i2
    columns = ((cls_s, cls), (rank_s, rank),
               (wa_ref, jnp.where(first_is_a, w1, w2)), (wb_ref, jnp.where(first_is_a, w2, w1)))
    for c in range(rows // LANES):
        for ref, col in columns:
            ref[pl.ds(i * (rows // LANES) + c, 1), :] = _to_lanes(col[c * LANES:(c + 1) * LANES])

    @pl.when(i == n_steps - 1)
    def _():
        counts = run_ref[...]
        tiles = jnp.floor((counts + (MOE_TILE - 1.0)) * (1.0 / MOE_TILE))
        upto = jnp.where(lax.broadcasted_iota(jnp.int32, (N_CLASSES, N_CLASSES), 0)
                         <= lax.broadcasted_iota(jnp.int32, (N_CLASSES, N_CLASSES), 1), 1.0, 0.0)
        both = jnp.concatenate([tiles, counts] + [jnp.zeros_like(counts)] * 6, axis=0)
        ends = _mm(both, upto, True)
        tile_end, tok_end = ends[0:1], ends[1:2]
        tile_off, tok_off = tile_end - tiles, tok_end - counts
        n_tiles = jnp.max(tile_end, axis=-1, keepdims=True)
        t = lax.broadcasted_iota(jnp.int32, (N_CLASSES, 1), 0).astype(F32)
        t = jnp.minimum(t, n_tiles - 1.0)
        cls_t = jnp.sum(jnp.where(tile_end <= t, 1.0, 0.0), axis=-1, keepdims=True)
        mine = lax.broadcasted_iota(jnp.int32, (N_CLASSES, N_CLASSES), 1).astype(F32) == cls_t
        pick = lambda v: jnp.sum(jnp.where(mine, v, 0.0), axis=-1, keepdims=True)
        base = pick(tok_off) + (t - pick(tile_off)) * MOE_TILE
        lane_t = lax.broadcasted_iota(jnp.int32, (N_CLASSES, LANES), 1)
        tiles_ref[...] = sum(jnp.where(lane_t == j, v, 0.0)
                             for j, v in enumerate([cls_t, base, n_tiles + jnp.zeros_like(t)]))

        tile_off_col, tok_off_col = _to_sublanes(tile_off), _to_sublanes(tok_off)
        class_id = lax.broadcasted_iota(jnp.int32, (N_CLASSES, LANES), 0).astype(F32)

        def per_block(j, carry):
            at = pl.ds(j, 1)
            own = class_id == cls_s[at, :]
            look = lambda col: jnp.sum(jnp.where(own, col, 0.0), axis=0, keepdims=True)
            rank_row = rank_s[at, :]
            slot_ref[at, :] = (look(tile_off_col) * MOE_TILE + rank_row).astype(jnp.int32)
            pos_ref[at, :] = (look(tok_off_col) + rank_row).astype(jnp.int32)
            return carry

        lax.fori_loop(0, slot_ref.shape[0], per_block, 0)


def _plan_order_kernel(pos_ref, order_ref, *, n_tokens):
    def place(t, carry):
        order_ref[pos_ref[t]] = t
        return carry

    lax.fori_loop(0, n_tokens, place, 0, unroll=32)
    for j in range(MOE_TILE):
        order_ref[n_tokens + j] = 0


def _route_plan(route, n_tokens):
    n_rows = route.shape[0]
    rows = PLAN_ROWS
    n_steps = n_rows // rows
    dense = lambda dtype: jax.ShapeDtypeStruct((n_rows // LANES, LANES), dtype)
    whole = lambda r, c: pl.BlockSpec((r, c), lambda i: (0, 0))
    slot, pos, wa, wb, tiles = pl.pallas_call(
        functools.partial(_plan_kernel, n_steps=n_steps, n_tokens=n_tokens),
        grid=(n_steps,), in_specs=[pl.BlockSpec((rows, LANES), lambda i: (i, 0))],
        out_specs=[whole(n_rows // LANES, LANES)] * 4 + [whole(N_CLASSES, LANES)],
        out_shape=[dense(jnp.int32), dense(jnp.int32), dense(F32), dense(F32),
                   jax.ShapeDtypeStruct((N_CLASSES, LANES), F32)],
        scratch_shapes=[pltpu.VMEM((1, N_CLASSES), F32),
                        pltpu.VMEM((n_rows // LANES, LANES), F32),
                        pltpu.VMEM((n_rows // LANES, LANES), F32)],
        compiler_params=_cparams(("arbitrary",)), name="plan",
    )(route)
    order = pl.pallas_call(
        functools.partial(_plan_order_kernel, n_tokens=n_tokens),
        in_specs=[pl.BlockSpec(memory_space=pltpu.SMEM)],
        out_specs=pl.BlockSpec(memory_space=pltpu.SMEM),
        out_shape=jax.ShapeDtypeStruct((n_tokens + MOE_TILE,), jnp.int32), name="plan_order",
    )(pos.reshape(-1))
    tile_cls = tiles[:, 0].astype(jnp.int32)
    tile_base = tiles[:, 1].astype(jnp.int32)
    n_tiles = tiles[0:1, 2].astype(jnp.int32)
    return tile_cls, tile_base, n_tiles, order, wa.reshape(-1), wb.reshape(-1), slot.reshape(-1)


def _moe_kernel(tcls_ref, tbase_ref, nt_ref, order_ref, wa_ref, wb_ref,
                hx_hbm, gf_ref, wg_ref, wu_ref, wd_ref, y_ref,
                xbuf, wgu_s, wd_s, sem):
    t = pl.program_id(0)
    n_tiles = nt_ref[0]
    buf = t % 2
    pair = EXPERTS_PER_GROUP * EXPERTS_PER_GROUP

    def gather(tile, b):
        base = tbase_ref[tile]
        for j in range(MOE_TILE):
            pltpu.make_async_copy(hx_hbm.at[order_ref[base + j]], xbuf.at[b, j], sem.at[b]).start()

    def wait(b):
        pltpu.make_async_copy(hx_hbm.at[pl.ds(0, MOE_TILE)], xbuf.at[b], sem.at[b]).wait()

    @pl.when(t == 0)
    def _():
        gather(0, 0)

    @pl.when(t == n_tiles)
    def _():
        wait(buf)

    @pl.when(t < n_tiles)
    def _():
        cls = tcls_ref[t]
        grp = cls // pair

        @pl.when((t == 0) | (grp != tcls_ref[jnp.maximum(t - 1, 0)] // pair))
        def _():
            for e in range(EXPERTS_PER_GROUP):
                wgu_s[e, :, 0:D_EXPERT] = wg_ref[0, e].astype(BF16)
                wgu_s[e, :, D_EXPERT:2 * D_EXPERT] = wu_ref[0, e].astype(BF16)
                wd_s[e] = wd_ref[0, e].astype(BF16)

        wait(buf)
        h = xbuf[buf].reshape(MOE_TILE, D_MODEL)
        gather(jnp.minimum(t + 1, n_tiles - 1), 1 - buf)
        xn = _rms(h, gf_ref[...]).astype(BF16)
        ea = (cls % pair) // EXPERTS_PER_GROUP
        eb = cls % EXPERTS_PER_GROUP

        def column(w_ref):
            base = tbase_ref[t]
            sub = lax.broadcasted_iota(jnp.int32, (SUBLANES, 1), 0)
            chunks = []
            for c in range(MOE_TILE // SUBLANES):
                acc = jnp.zeros((SUBLANES, 1), F32)
                for k in range(SUBLANES):
                    acc = jnp.where(sub == k, w_ref[order_ref[base + c * SUBLANES + k]], acc)
                chunks.append(acc)
            return jnp.concatenate(chunks, axis=0)

        gus = [jnp.dot(xn, wgu_s[e], preferred_element_type=F32) for e in (ea, eb)]
        acts = [(jax.nn.silu(gu[:, 0:D_EXPERT]) * gu[:, D_EXPERT:2 * D_EXPERT]).astype(BF16)
                for gu in gus]
        oa, ob = [jnp.dot(act, wd_s[e], preferred_element_type=F32)
                  for act, e in zip(acts, (ea, eb))]
        y = h + column(wa_ref) * oa + column(wb_ref) * ob
        y_ref[...] = y.reshape(MOE_TILE, D_MODEL // LANES, LANES)

    @pl.when((t >= n_tiles) & (t < pl.num_programs(0) - 1))
    def _():
        y_ref[...] = jnp.zeros(y_ref.shape, F32)


def _moe(hx, gf, w_gate, w_up, w_down, plan, n_slots):
    tile_cls, tile_base, n_tiles, order, wa, wb, _ = plan
    pair = EXPERTS_PER_GROUP * EXPERTS_PER_GROUP
    max_tiles = n_slots // MOE_TILE
    assert max_tiles < tile_cls.shape[0]
    grouped = lambda w: w.reshape(N_GROUPS, EXPERTS_PER_GROUP, *w.shape[1:])
    wspec = lambda r, c: pl.BlockSpec((1, EXPERTS_PER_GROUP, r, c),
                                      lambda t, tc, *_: (tc[t] // pair, 0, 0, 0),
                                      pipeline_mode=pl.Buffered(1))
    grid_spec = pltpu.PrefetchScalarGridSpec(
        num_scalar_prefetch=6, grid=(max_tiles + 1,),
        in_specs=[pl.BlockSpec(memory_space=pl.ANY),
                  pl.BlockSpec(gf.shape, lambda t, *_: (0, 0)),
                  wspec(D_MODEL, D_EXPERT), wspec(D_MODEL, D_EXPERT), wspec(D_EXPERT, D_MODEL)],
        out_specs=pl.BlockSpec((MOE_TILE, D_MODEL // LANES, LANES),
                               lambda t, *_: (jnp.minimum(t, max_tiles - 1), 0, 0)),
        scratch_shapes=[pltpu.VMEM((2, MOE_TILE, TOKEN_ROWS, LANES), F32),
                        pltpu.VMEM((EXPERTS_PER_GROUP, D_MODEL, 2 * D_EXPERT), BF16),
                        pltpu.VMEM((EXPERTS_PER_GROUP, D_EXPERT, D_MODEL), BF16),
                        pltpu.SemaphoreType.DMA((2,))])
    return pl.pallas_call(
        _moe_kernel, grid_spec=grid_spec,
        out_shape=jax.ShapeDtypeStruct((n_slots, D_MODEL // LANES, LANES), F32),
        compiler_params=_cparams(("arbitrary",)), name="moe",
    )(tile_cls, tile_base, n_tiles, order, wa, wb, hx, gf,
      grouped(w_gate), grouped(w_up), grouped(w_down))


def _unsort_kernel(slot_ref, ys_hbm, y_ref, buf, sem, *, rows, row0):
    i = pl.program_id(0)
    b = i % 2
    last = pl.num_programs(0) - 2

    def gather(step, dst):
        for j in range(rows):
            pltpu.make_async_copy(ys_hbm.at[slot_ref[row0 + step * rows + j]], buf.at[dst, j],
                                  sem.at[dst]).start()

    def wait(dst):
        pltpu.make_async_copy(ys_hbm.at[pl.ds(0, rows)], buf.at[dst], sem.at[dst]).wait()

    @pl.when(i == 0)
    def _():
        gather(0, 0)

    @pl.when(i > last)
    def _():
        wait(b)

    @pl.when(i <= last)
    def _():
        gather(jnp.minimum(i + 1, last), 1 - b)
        wait(b)
        y_ref[...] = buf[b].reshape(rows, D_MODEL)


def _unsort(ys, token_slot, *, n, row0, rows):
    grid_spec = pltpu.PrefetchScalarGridSpec(
        num_scalar_prefetch=1, grid=(n // rows + 1,),
        in_specs=[pl.BlockSpec(memory_space=pl.ANY)],
        out_specs=pl.BlockSpec((rows, D_MODEL), lambda i, *_: (jnp.minimum(i, n // rows - 1), 0)),
        scratch_shapes=[pltpu.VMEM((2, rows, D_MODEL // LANES, LANES), F32),
                        pltpu.SemaphoreType.DMA((2,))])
    return pl.pallas_call(
        functools.partial(_unsort_kernel, rows=rows, row0=row0), grid_spec=grid_spec,
        out_shape=jax.ShapeDtypeStruct((n, D_MODEL), F32),
        compiler_params=_cparams(("arbitrary",)), name="unsort",
    )(token_slot, ys)


def kernel(x_prompt, x_sample, cache_k, cache_v, state_conv, g_norm_mix, w_in, g_q, g_k, w_conv,
           g_attn_out, g_conv_out, w_out, g_norm_ffn, w_router_group, w_router_expert,
           w_gate, w_up, w_down):
    depth = g_norm_mix.shape[0]
    assert depth == 1
    nb, seq, _ = x_prompt.shape
    db, dec_seq, _ = x_sample.shape
    assert dec_seq == 1
    l = 0

    gmix = g_norm_mix[l][None, :]
    gq = jnp.tile(g_q[l], N_HEADS)[None, :]
    gk = jnp.tile(g_k[l], N_HEADS)[None, :]
    ga = g_attn_out[l][None, :]
    gc = g_conv_out[l][None, :]
    gf = g_norm_ffn[l][None, :]
    w_router = jnp.concatenate(
        [w_router_expert[l], w_router_group[l],
         jnp.zeros((D_MODEL, LANES - N_EXPERTS - N_GROUPS), F32)], axis=1)
    w_router_hi = w_router.astype(BF16)
    w_router_split = jnp.concatenate(
        [w_router_hi, (w_router - w_router_hi.astype(F32)).astype(BF16)], axis=1)

    cos_p, sin_p = _rope_tables(np.arange(seq))
    xp = x_prompt.reshape(nb * seq, D_MODEL)
    q, k, v, bc, tail, kt, vt = _inproj(xp, gmix, w_in[l].astype(BF16), gq, gk, cos_p, sin_p,
                                        w_conv[l], tm=256, seq_len=seq, hi=False)
    attn = _attention(q.reshape(nb, seq, D_ATTN), k.reshape(nb, seq, D_ATTN),
                      v.reshape(nb, seq, D_ATTN))
    n_prompt = nb * seq
    n_total = n_prompt + db
    n_rows = -(-n_total // TOKEN_PAD) * TOKEN_PAD
    hx, route = _outproj(attn.reshape(n_prompt, D_ATTN), bc, xp, ga, gc, w_out[l].astype(BF16),
                         gf, w_router_split, tm=512, hi=False, n_rows=n_rows, row0=0, n_parts=2)
    k_prompt = jnp.transpose(kt, (0, 3, 1, 2))
    v_prompt = jnp.transpose(vt, (0, 3, 1, 2))
    conv_prompt = tail.reshape(nb, SUBLANES, D_CONV)[:, SUBLANES - (CONV_WIDTH - 1):]

    cos_s, sin_s = _rope_tables(np.full((db,), PAST_LEN))
    xs = x_sample.reshape(db, D_MODEL)
    st = state_conv[l]
    qs, ks, vs, bcs, us = _inproj(xs, gmix, w_in[l], gq, gk, cos_s, sin_s, w_conv[l],
                                  tm=db, seq_len=db, hi=True, prev=(st[:, 1], st[:, 0]))
    heads = lambda t: t.reshape(db, N_HEADS, HEAD_DIM)
    attn_s = _attention_sample(heads(qs), heads(ks), heads(vs),
                               jnp.transpose(cache_k[l], (0, 2, 3, 1)),
                               jnp.transpose(cache_v[l], (0, 2, 3, 1)))
    hx, route = _outproj(attn_s.reshape(db, D_ATTN), bcs, xs, ga, gc, w_out[l], gf, w_router,
                         tm=db, hi=True, n_rows=n_rows, row0=n_prompt, into=(hx, route))

    n_cls = N_GROUPS * EXPERTS_PER_GROUP * (EXPERTS_PER_GROUP - 1) // 2
    n_slots = -(-(n_total + n_cls * (MOE_TILE - 1)) // MOE_TILE) * MOE_TILE
    assert n_slots // MOE_TILE <= N_CLASSES and n_rows % PLAN_ROWS == 0
    plan = _route_plan(route, n_total)
    ys = _moe(hx, gf, w_gate[l], w_up[l], w_down[l], plan, n_slots)
    y_prompt = _unsort(ys, plan[-1], n=n_prompt, row0=0, rows=256).reshape(nb, seq, D_MODEL)
    y_sample = _unsort(ys, plan[-1], n=db, row0=n_prompt, rows=db).reshape(db, 1, D_MODEL)
    k_sample = ks.reshape(db, 1, N_HEADS, HEAD_DIM)
    v_sample = vs.reshape(db, 1, N_HEADS, HEAD_DIM)
    conv_sample = jnp.stack([st[:, 1], us], axis=1)

    return (y_prompt, y_sample, k_prompt[None], v_prompt[None], conv_prompt[None],
            k_sample[None], v_sample[None], conv_sample[None])
```

```python
import functools

import numpy as np
import jax
import jax.numpy as jnp
from jax import lax
from jax.experimental import pallas as pl
from jax.experimental.pallas import tpu as pltpu

F32 = jnp.float32
BF16 = jnp.bfloat16

D_MODEL = 1024
HEAD_DIM = 64
N_HEADS = 12
D_ATTN = N_HEADS * HEAD_DIM
D_CONV = D_MODEL - D_ATTN
D_IN_PROJ = 3 * D_ATTN + 3 * D_CONV
CONV_WIDTH = 3
DILATIONS = (1, 4, 16)
ATT_BLOCK = 128
ATT_UNROLL = 16
REGROUP_STEP = 4
MAX_WINDOW = 2048
PAST_LEN = 8192
ATTN_SCALE = HEAD_DIM ** -0.5
ROPE_THETA = 10000.0
EPS = 1e-6
N_GROUPS = 4
EXPERTS_PER_GROUP = 8
N_EXPERTS = N_GROUPS * EXPERTS_PER_GROUP
D_EXPERT = 256
MOE_TILE = 192
N_CLASSES = N_GROUPS * EXPERTS_PER_GROUP * EXPERTS_PER_GROUP
PLAN_ROWS = 768
TOKEN_PAD = 512

LANES = 128
SUBLANES = 8
MXU_DIM = 256
TOKEN_ROWS = D_MODEL // LANES
NEG = -1e30
VMEM_LIMIT = 48 * 1024 * 1024


def _cparams(sem):
    return pltpu.CompilerParams(dimension_semantics=sem, vmem_limit_bytes=VMEM_LIMIT)


def _mm(a, b, hi):
    if hi:
        return jnp.dot(a, b, preferred_element_type=F32, precision=lax.Precision.HIGHEST)
    return jnp.dot(a.astype(BF16), b.astype(BF16), preferred_element_type=F32)


def _rms(x, g):
    return x * lax.rsqrt(jnp.mean(x * x, axis=-1, keepdims=True) + EPS) * g


def _rope_tables(pos):
    half = HEAD_DIM // 2
    inv = ROPE_THETA ** (-np.arange(half, dtype=np.float64) / half)
    ang = np.asarray(pos, np.float64)[:, None] * inv[None, :]
    cos, sin = np.cos(ang), np.sin(ang)
    cos2 = np.concatenate([cos, cos, cos, cos], axis=-1)
    sin2 = np.concatenate([-sin, sin, -sin, sin], axis=-1)
    return jnp.asarray(cos2, F32), jnp.asarray(sin2, F32)


def _inproj_kernel(*refs, tm, tiles_per_seq, keep_tiles, hi, given_prev):
    if given_prev:
        (x_ref, gmix_ref, w_ref, gq_ref, gk_ref, cos_ref, sin_ref, wconv_ref, p1_ref, p2_ref,
         q_ref, k_ref, v_ref, bc_ref, u_ref) = refs
    else:
        (x_ref, gmix_ref, w_ref, gq_ref, gk_ref, cos_ref, sin_ref, wconv_ref,
         q_ref, k_ref, v_ref, bc_ref, tail_ref, kt_ref, vt_ref, ubuf) = refs

    xn = _rms(x_ref[...], gmix_ref[...])
    if not hi:
        xn = xn.astype(BF16)

    r_i = lax.broadcasted_iota(jnp.int32, (MXU_DIM, MXU_DIM), 0) // HEAD_DIM
    c_i = lax.broadcasted_iota(jnp.int32, (MXU_DIM, MXU_DIM), 1) // HEAD_DIM
    head_sum = jnp.where(r_i == c_i, 1.0, 0.0).astype(F32 if hi else BF16)

    reps = D_ATTN // LANES
    cos = jnp.concatenate([cos_ref[...]] * reps, axis=-1)
    sin = jnp.concatenate([sin_ref[...]] * reps, axis=-1)
    lane = lax.broadcasted_iota(jnp.int32, (1, D_ATTN), 1)
    first_half = (lane % HEAD_DIM) < (HEAD_DIM // 2)

    def head_ssq(z):
        sq = z * z
        return jnp.concatenate(
            [_mm(sq[:, c * MXU_DIM:(c + 1) * MXU_DIM], head_sum, hi)
             for c in range(D_ATTN // MXU_DIM)], axis=-1)

    def norm_rope(z, ssq, g_ref):
        zn = z * lax.rsqrt(ssq * (1.0 / HEAD_DIM) + EPS) * g_ref[...]
        partner = jnp.where(first_half,
                            pltpu.roll(zn, D_ATTN - HEAD_DIM // 2, 1),
                            pltpu.roll(zn, HEAD_DIM // 2, 1))
        return zn * cos + partner * sin

    zq = _mm(xn, w_ref[:, 0:D_ATTN], hi)
    zk = _mm(xn, w_ref[:, D_ATTN:2 * D_ATTN], hi)
    ssq_q = head_ssq(zq)
    v = _mm(xn, w_ref[:, 2 * D_ATTN:3 * D_ATTN], hi)
    ssq_k = head_ssq(zk)
    c0 = 3 * D_ATTN
    b_gate = _mm(xn, w_ref[:, c0:c0 + D_CONV], hi)
    c_gate = _mm(xn, w_ref[:, c0 + D_CONV:c0 + 2 * D_CONV], hi)
    u_in = _mm(xn, w_ref[:, c0 + 2 * D_CONV:c0 + 3 * D_CONV], hi)
    q_ref[...] = norm_rope(zq, ssq_q, gq_ref) * ATTN_SCALE
    k = norm_rope(zk, ssq_k, gk_ref)
    k_ref[...] = k
    v_ref[...] = v
    if not given_prev:
        @pl.when(pl.program_id(0) % tiles_per_seq >= tiles_per_seq - keep_tiles)
        def _():
            kt_ref[0] = k.T.reshape(N_HEADS, HEAD_DIM, tm)
            vt_ref[0] = v.T.reshape(N_HEADS, HEAD_DIM, tm)
    u = c_gate * u_in

    if given_prev:
        u1, u2 = p1_ref[...], p2_ref[...]
        u_ref[...] = u
    else:
        i = pl.program_id(0)

        @pl.when(i % tiles_per_seq == 0)
        def _():
            ubuf[0:SUBLANES, :] = jnp.zeros((SUBLANES, D_CONV), F32)

        @pl.when(i % tiles_per_seq != 0)
        def _():
            ubuf[0:SUBLANES, :] = ubuf[tm:tm + SUBLANES, :]

        ubuf[SUBLANES:tm + SUBLANES, :] = u
        u1 = ubuf[SUBLANES - 1:tm + SUBLANES - 1, :]
        u2 = ubuf[SUBLANES - 2:tm + SUBLANES - 2, :]
        tail_ref[...] = ubuf[tm:tm + SUBLANES, :]

    wc = wconv_ref[...]
    conv = wc[0:1, :] * u2 + wc[1:2, :] * u1 + wc[2:3, :] * u
    bc_ref[...] = b_gate * conv


def _inproj(x, gmix, w_in, gq, gk, cos, sin, w_conv, *, tm, seq_len, hi, prev=None):
    n = x.shape[0]
    nt = n // tm
    tiles_per_seq = seq_len // tm
    row = lambda w: pl.BlockSpec((tm, w), lambda i: (i, 0))
    full = lambda a: pl.BlockSpec(a.shape, lambda i: (0,) * a.ndim)
    tab = pl.BlockSpec((tm, LANES), lambda i: (i % tiles_per_seq, 0))
    in_specs = [row(D_MODEL), full(gmix), full(w_in), full(gq), full(gk), tab, tab, full(w_conv)]
    args = [x, gmix, w_in, gq, gk, cos, sin, w_conv]
    out_shape = [jax.ShapeDtypeStruct((n, D_ATTN), F32)] * 3 + [jax.ShapeDtypeStruct((n, D_CONV), F32)]
    out_specs = [row(D_ATTN)] * 3 + [row(D_CONV)]
    scratch = []
    keep_tiles = min(MAX_WINDOW, seq_len) // tm
    if prev is not None:
        in_specs += [row(D_CONV), row(D_CONV)]
        args += list(prev)
        out_shape.append(jax.ShapeDtypeStruct((n, D_CONV), F32))
        out_specs.append(row(D_CONV))
    else:
        n_seq = n // seq_len
        out_shape.append(jax.ShapeDtypeStruct((n_seq * SUBLANES, D_CONV), F32))
        out_specs.append(pl.BlockSpec((SUBLANES, D_CONV), lambda i: (i // tiles_per_seq, 0)))
        kept = pl.BlockSpec(
            (1, N_HEADS, HEAD_DIM, tm),
            lambda i: (i // tiles_per_seq, 0, 0,
                       jnp.maximum(i % tiles_per_seq - (tiles_per_seq - keep_tiles), 0)))
        out_shape += [jax.ShapeDtypeStruct((n_seq, N_HEADS, HEAD_DIM, keep_tiles * tm), F32)] * 2
        out_specs += [kept, kept]
        scratch.append(pltpu.VMEM((tm + SUBLANES, D_CONV), F32))
    return pl.pallas_call(
        functools.partial(_inproj_kernel, tm=tm, tiles_per_seq=tiles_per_seq,
                          keep_tiles=keep_tiles, hi=hi, given_prev=prev is not None),
        grid=(nt,), in_specs=in_specs, out_specs=out_specs, out_shape=out_shape,
        scratch_shapes=scratch, compiler_params=_cparams(("arbitrary",)),
        name="inproj_hi" if hi else "inproj",
    )(*args)


def _attn_kernel(q_ref, k_ref, v_ref, o_ref, num_ref, den_ref, m_ref, bias_ref,
                 qd_ref, kd_ref, vd_ref, numd_ref, dend_ref, md_ref, tmp_ref, *, seq_len):
    blk = ATT_BLOCK
    qi = lax.broadcasted_iota(jnp.int32, (blk, 2 * blk), 0)
    kj = lax.broadcasted_iota(jnp.int32, (blk, 2 * blk), 1)
    bias_ref[...] = jnp.where((kj >= qi) & (kj <= qi + blk), 0.0, NEG)
    head0 = lax.broadcasted_iota(jnp.int32, (1, LANES), 1) < HEAD_DIM
    dn = (((1,), (1,)), ((), ()))

    def rows(start, size, d):
        return pl.ds(start, size) if d == 1 else pl.ds(start, size, stride=d)

    natural = (lambda at: q_ref[0, at, :], lambda at: k_ref[0, at, :], lambda at: v_ref[0, at, :],
               (num_ref, den_ref, m_ref))
    regrouped = (lambda at: qd_ref[at, :], lambda at: kd_ref[at, :], lambda at: vd_ref[at, :],
                 (numd_ref, dend_ref, md_ref))

    def group(src, d, r, n0, u, first, mode):
        q_at, k_at, v_at, (num_acc, den_acc, m_acc) = src
        q_rows = rows(n0 * (blk * d) + r, u * blk, d)
        if first:
            k_rows = rows(r, u * blk, d)
        else:
            k_rows = rows((n0 - 1) * (blk * d) + r, (u + 1) * blk, d)
        qa = q_at(q_rows)
        ka = k_at(k_rows).astype(BF16)
        va = v_at(k_rows)
        qa0 = jnp.where(head0, qa, 0.0).astype(BF16)
        qa1 = jnp.where(head0, 0.0, qa).astype(BF16)
        va0 = jnp.where(head0, va, 1.0).astype(BF16)
        va1 = jnp.where(head0, 1.0, va).astype(BF16)
        if mode != "init":
            m_old = m_acc[q_rows, :]
            num_old = num_acc[q_rows, :]
            den_old = den_acc[q_rows, :]
        key_slices, scores = [], []
        for i in range(u):
            qs = slice(i * blk, (i + 1) * blk)
            if first and i == 0:
                ks = slice(0, blk)
                bias = bias_ref[:, blk:2 * blk]
            else:
                k0 = (i - 1) * blk if first else i * blk
                ks = slice(k0, k0 + 2 * blk)
                bias = bias_ref[...]
            key_slices.append(ks)
            scores.append(
                (lax.dot_general(qa0[qs], ka[ks], dn, preferred_element_type=F32) + bias,
                 lax.dot_general(qa1[qs], ka[ks], dn, preferred_element_type=F32) + bias))
        yield
        probs, ms = [], []
        for s0, s1 in scores:
            m0 = jnp.max(s0, axis=-1, keepdims=True)
            m1 = jnp.max(s1, axis=-1, keepdims=True)
            probs.append((jnp.exp(s0 - m0).astype(BF16), jnp.exp(s1 - m1).astype(BF16)))
            ms.append(jnp.where(head0, m0, m1))
        yield
        nums, dens = [], []
        for (p0, p1), ks in zip(probs, key_slices):
            r0 = jnp.dot(p0, va0[ks], preferred_element_type=F32)
            r1 = jnp.dot(p1, va1[ks], preferred_element_type=F32)
            nums.append(jnp.where(head0, r0, r1))
            dens.append(pltpu.roll(jnp.where(head0, r1, r0), HEAD_DIM, 1))
        num = jnp.concatenate(nums, axis=0)
        den = jnp.concatenate(dens, axis=0)
        m = jnp.concatenate(ms, axis=0)
        if mode == "init":
            return [(num_acc, q_rows, num), (den_acc, q_rows, den), (m_acc, q_rows, m)]
        m_new = jnp.maximum(m_old, m)
        a = jnp.exp(m_old - m_new)
        b = jnp.exp(m - m_new)
        num = num_old * a + num * b
        den = den_old * a + den * b
        if mode == "merge":
            return [(num_acc, q_rows, num), (den_acc, q_rows, den), (m_acc, q_rows, m_new)]
        return [(o_ref.at[0], q_rows, num / den)]

    def run(groups):
        stages = [group(*g) for g in groups]
        for _ in range(2):
            for stage in stages:
                next(stage)
        stores = []
        for stage in stages:
            try:
                next(stage)
            except StopIteration as done:
                stores += done.value
        for ref, at, val in stores:
            ref[at, :] = val

    def branch(src, d, n_seq, sub_len, seq_stride, mode):
        nb = sub_len // blk
        u = min(ATT_UNROLL, nb)
        if nb == u:
            per_body = ATT_UNROLL // u

            def several(rr, carry):
                run([(src, d, (rr * per_body + t) * seq_stride, 0, u, True, mode)
                     for t in range(per_body)])
                return carry

            lax.fori_loop(0, n_seq // per_body, several, 0)
        else:
            def one(r, carry):
                run([(src, d, r * seq_stride, 0, u, True, mode)])

                def later_blocks(g, c):
                    run([(src, d, r * seq_stride, g * u, u, False, mode)])
                    return c

                lax.fori_loop(1, nb // u, later_blocks, 0)
                return carry

            lax.fori_loop(0, n_seq, one, 0)

    step = REGROUP_STEP
    wide = step * step
    assert DILATIONS == (1, step, wide)
    part = seq_len // wide

    def regroup(load, dst_ref):
        for a in range(step):
            tmp_ref[a * step * part:(a + 1) * step * part, :] = load(rows(a, step * part, step))
        for a in range(step):
            for b in range(step):
                s = a * step + b
                dst_ref[s * part:(s + 1) * part, :] = tmp_ref[rows(a * step * part + b, part, step), :]

    def ungroup(src_ref, dst_ref):
        for a in range(step):
            for b in range(step):
                s = a * step + b
                tmp_ref[rows(a * step * part + b, part, step), :] = src_ref[s * part:(s + 1) * part, :]
        for a in range(step):
            dst_ref[rows(a, step * part, step), :] = tmp_ref[a * step * part:(a + 1) * step * part, :]

    for load, dst in zip(natural[:3], (qd_ref, kd_ref, vd_ref)):
        regroup(load, dst)
    branch(regrouped, 1, wide, part, part, "init")
    for src, dst in zip(regrouped[3], natural[3]):
        ungroup(src, dst)
    branch(natural, 1, 1, seq_len, 0, "merge")
    branch(natural, step, step, seq_len // step, 1, "final")


def _attention(q, k, v):
    b, s, _ = q.shape
    spec = pl.BlockSpec((1, s, LANES), lambda i, p: (i, 0, p))
    return pl.pallas_call(
        functools.partial(_attn_kernel, seq_len=s),
        grid=(b, D_ATTN // LANES), in_specs=[spec] * 3, out_specs=spec,
        out_shape=jax.ShapeDtypeStruct((b, s, D_ATTN), F32),
        scratch_shapes=[pltpu.VMEM((s, LANES), F32)] * 3 + [pltpu.VMEM((ATT_BLOCK, 2 * ATT_BLOCK), F32)]
        + [pltpu.VMEM((s, LANES), F32)] * 7,
        compiler_params=_cparams(("arbitrary", "arbitrary")), name="attn_prompt",
    )(q, k, v)


def _attn_sample_kernel(q_ref, kn_ref, vn_ref, kc_ref, vc_ref, o_ref, *, w_buf, heads):
    age = w_buf - lax.broadcasted_iota(jnp.int32, (1, w_buf), 1)
    valid = [(age % d == 0) & (age <= d * ATT_BLOCK) for d in DILATIONS]
    diag = (lax.broadcasted_iota(jnp.int32, (HEAD_DIM, HEAD_DIM), 0)
            == lax.broadcasted_iota(jnp.int32, (HEAD_DIM, HEAD_DIM), 1))
    to_col = lambda row: jnp.sum(jnp.where(diag, row, 0.0), axis=1, keepdims=True)
    to_row = lambda col: jnp.sum(jnp.where(diag, col, 0.0), axis=0, keepdims=True)
    q_rows, kn_rows, vn_rows = q_ref[0], kn_ref[0], vn_ref[0]
    s_self_all = jnp.sum(kn_rows * q_rows, axis=1, keepdims=True)
    for h in range(heads):
        q = to_col(q_rows[h:h + 1])
        vn = to_col(vn_rows[h:h + 1])
        s = jnp.sum(kc_ref[0, h] * q, axis=0, keepdims=True)
        s_self = s_self_all[h:h + 1]
        parts = []
        for ok in valid:
            sd = jnp.where(ok, s, NEG)
            m = jnp.maximum(jnp.max(sd, axis=1, keepdims=True), s_self)
            p = jnp.exp(sd - m)
            p_self = jnp.exp(s_self - m)
            parts.append((p, p_self, jnp.sum(p, axis=1, keepdims=True) + p_self, m))
        m_all = jnp.maximum(jnp.maximum(parts[0][3], parts[1][3]), parts[2][3])
        w = jnp.zeros((1, w_buf), F32)
        w_self = jnp.zeros((1, 1), F32)
        den_t = jnp.zeros((1, 1), F32)
        for p, p_self, den, m in parts:
            c = jnp.exp(m - m_all)
            w = w + p * c
            w_self = w_self + p_self * c
            den_t = den_t + den * c
        num_t = jnp.sum(vc_ref[0, h] * w, axis=1, keepdims=True) + vn * w_self
        o_ref[0, h:h + 1, :] = to_row(num_t / den_t)


def _attention_sample(q, kn, vn, cache_kt, cache_vt):
    db, heads, _, w_buf = cache_kt.shape
    tok = pl.BlockSpec((1, heads, HEAD_DIM), lambda i: (i, 0, 0))
    cache = pl.BlockSpec((1, heads, HEAD_DIM, w_buf), lambda i: (i, 0, 0, 0))
    return pl.pallas_call(
        functools.partial(_attn_sample_kernel, w_buf=w_buf, heads=heads),
        grid=(db,), in_specs=[tok, tok, tok, cache, cache], out_specs=tok,
        out_shape=jax.ShapeDtypeStruct((db, heads, HEAD_DIM), F32),
        compiler_params=_cparams(("arbitrary",)), name="attn_sample",
    )(q, kn, vn, cache_kt, cache_vt)


def _outproj_kernel(attn_ref, bc_ref, x_ref, ga_ref, gc_ref, wo_ref, gf_ref, wr_ref, *rest,
                    hi, n_tiles, n_parts):
    hx_ref, route_ref = rest[-2:]
    part_rows = x_ref.shape[0] // n_parts

    def part(k):
        rows = pl.ds(k * part_rows, part_rows)
        a = _rms(attn_ref[rows, :], ga_ref[...])
        c = _rms(bc_ref[rows, :], gc_ref[...])
        mix = _mm(a, wo_ref[0:D_ATTN, :], hi) + _mm(c, wo_ref[D_ATTN:D_MODEL, :], hi)
        yield
        h = x_ref[rows, :] + mix
        hn = _rms(h, gf_ref[...])
        if hi:
            lg = _mm(hn, wr_ref[...], True)
        else:
            hn_hi = hn.astype(BF16)
            hn_lo = (hn - hn_hi.astype(F32)).astype(BF16)
            both = jnp.dot(hn_hi, wr_ref[...], preferred_element_type=F32)
            lg = (both[:, :LANES] + both[:, LANES:]
                  + jnp.dot(hn_lo, wr_ref[:, :LANES], preferred_element_type=F32))
        yield
        lane_i = lax.broadcasted_iota(jnp.int32, lg.shape, 1)
        lane = lane_i.astype(F32)
        lane_group = (lane_i // EXPERTS_PER_GROUP).astype(F32)
        is_e = lane_i < N_EXPERTS
        is_g = (lane_i >= N_EXPERTS) & (lane_i < N_EXPERTS + N_GROUPS)
        big = float(LANES)
        first_at = lambda t, v: jnp.min(jnp.where(t == v, lane, big), axis=-1, keepdims=True)

        gl = jnp.where(is_g, lg, NEG)
        mg = jnp.max(gl, axis=-1, keepdims=True)
        p_g = 1.0 / jnp.sum(jnp.exp(gl - mg), axis=-1, keepdims=True)
        g_idx = first_at(gl, mg) - float(N_EXPERTS)
        el = jnp.where(is_e & (lane_group == g_idx), lg, NEG)
        v1 = jnp.max(el, axis=-1, keepdims=True)
        i1 = first_at(el, v1)
        el2 = jnp.where(lane == i1, NEG, el)
        v2 = jnp.max(el2, axis=-1, keepdims=True)
        i2 = first_at(el2, v2)
        e2 = jnp.exp(v2 - v1)
        w1 = p_g / (1.0 + e2)
        w2 = p_g * e2 / (1.0 + e2)
        route_ref[rows, :] = sum(jnp.where(lane_i == j, val, 0.0)
                                 for j, val in enumerate([i1, i2, w1, w2]))
        hx_ref[rows, :, :] = h.reshape(part_rows, TOKEN_ROWS, LANES)

    stages = [part(k) for k in range(n_parts)]
    for _ in range(2):
        for stage in stages:
            next(stage)
    for stage in stages:
        for _ in stage:
            pass

    @pl.when(pl.program_id(0) >= n_tiles)
    def _():
        hx_ref[...] = jnp.zeros(hx_ref.shape, F32)
        route_ref[...] = jnp.zeros(route_ref.shape, F32)


def _outproj(attn, bc, x, ga, gc, w_out, gf, w_router, *, tm, hi, n_rows, row0, n_parts=1,
             into=None):
    n = x.shape[0]
    n_tiles = n // tm
    n_fill = 0 if into is not None else -(-(n_rows - n) // tm)
    row = lambda w: pl.BlockSpec((tm, w), lambda i: (jnp.minimum(i, n_tiles - 1), 0))
    full = lambda a: pl.BlockSpec(a.shape, lambda i: (0,) * a.ndim)
    in_specs = [row(D_ATTN), row(D_CONV), row(D_MODEL), full(ga), full(gc), full(w_out),
                full(gf), full(w_router)]
    args = [attn, bc, x, ga, gc, w_out, gf, w_router]
    aliases = {}
    if into is not None:
        in_specs += [pl.BlockSpec(memory_space=pl.ANY)] * 2
        args += list(into)
        aliases = {len(args) - 2: 0, len(args) - 1: 1}
    return pl.pallas_call(
        functools.partial(_outproj_kernel, hi=hi, n_tiles=n_tiles, n_parts=n_parts),
        grid=(n_tiles + n_fill,), in_specs=in_specs,
        out_specs=[pl.BlockSpec((tm, TOKEN_ROWS, LANES), lambda i: (i + row0 // tm, 0, 0)),
                   pl.BlockSpec((tm, LANES), lambda i: (i + row0 // tm, 0))],
        out_shape=[jax.ShapeDtypeStruct((n_rows, TOKEN_ROWS, LANES), F32),
                   jax.ShapeDtypeStruct((n_rows, LANES), F32)],
        input_output_aliases=aliases,
        compiler_params=_cparams(("arbitrary",)), name="outproj_hi" if hi else "outproj",
    )(*args)


def _to_lanes(col):
    n = col.shape[0]
    diag = lax.broadcasted_iota(jnp.int32, (n, n), 0) == lax.broadcasted_iota(jnp.int32, (n, n), 1)
    return jnp.sum(jnp.where(diag, col, 0.0), axis=0, keepdims=True)


def _to_sublanes(row):
    n = row.shape[1]
    diag = lax.broadcasted_iota(jnp.int32, (n, n), 0) == lax.broadcasted_iota(jnp.int32, (n, n), 1)
    return jnp.sum(jnp.where(diag, row, 0.0), axis=1, keepdims=True)


def _plan_kernel(route_ref, slot_ref, pos_ref, wa_ref, wb_ref, tiles_ref, run_ref, cls_s, rank_s,
                 *, n_steps, n_tokens):
    i = pl.program_id(0)
    rows = route_ref.shape[0]
    e = float(EXPERTS_PER_GROUP)

    @pl.when(i == 0)
    def _():
        run_ref[...] = jnp.zeros(run_ref.shape, F32)

    r = route_ref[...]
    i1, i2, w1, w2 = r[:, 0:1], r[:, 1:2], r[:, 2:3], r[:, 3:4]
    grp = jnp.floor(i1 * (1.0 / e))
    cls = grp * (e * e) + (jnp.minimum(i1, i2) - grp * e) * e + (jnp.maximum(i1, i2) - grp * e)
    token = i * rows + lax.broadcasted_iota(jnp.int32, (rows, 1), 0)
    lane = lax.broadcasted_iota(jnp.int32, (rows, N_CLASSES), 1).astype(F32)
    onehot = (lane == cls) & (token < n_tokens)
    earlier = (lax.broadcasted_iota(jnp.int32, (rows, rows), 0)
               > lax.broadcasted_iota(jnp.int32, (rows, rows), 1))
    before = jnp.dot(jnp.where(earlier, 1.0, 0.0).astype(BF16),
                     jnp.where(onehot, 1.0, 0.0).astype(BF16), preferred_element_type=F32)
    rank = jnp.sum(jnp.where(onehot, before + run_ref[...], 0.0), axis=-1, keepdims=True)
    run_ref[...] += jnp.sum(jnp.where(onehot, 1.0, 0.0), axis=0, keepdims=True)
    first_is_a = i1 < i2
    columns = ((cls_s, cls), (rank_s, rank),
               (wa_ref, jnp.where(first_is_a, w1, w2)), (wb_ref, jnp.where(first_is_a, w2, w1)))
    for c in range(rows // LANES):
        for ref, col in columns:
            ref[pl.ds(i * (rows // LANES) + c, 1), :] = _to_lanes(col[c * LANES:(c + 1) * LANES])

    @pl.when(i == n_steps - 1)
    def _():
        counts = run_ref[...]
        tiles = jnp.floor((counts + (MOE_TILE - 1.0)) * (1.0 / MOE_TILE))
        upto = jnp.where(lax.broadcasted_iota(jnp.int32, (N_CLASSES, N_CLASSES), 0)
                         <= lax.broadcasted_iota(jnp.int32, (N_CLASSES, N_CLASSES), 1), 1.0, 0.0)
        both = jnp.concatenate([tiles, counts] + [jnp.zeros_like(counts)] * 6, axis=0)
        ends = _mm(both, upto, True)
        tile_end, tok_end = ends[0:1], ends[1:2]
        tile_off, tok_off = tile_end - tiles, tok_end - counts
        n_tiles = jnp.max(tile_end, axis=-1, keepdims=True)
        t = lax.broadcasted_iota(jnp.int32, (N_CLASSES, 1), 0).astype(F32)
        t = jnp.minimum(t, n_tiles - 1.0)
        cls_t = jnp.sum(jnp.where(tile_end <= t, 1.0, 0.0), axis=-1, keepdims=True)
        mine = lax.broadcasted_iota(jnp.int32, (N_CLASSES, N_CLASSES), 1).astype(F32) == cls_t
        pick = lambda v: jnp.sum(jnp.where(mine, v, 0.0), axis=-1, keepdims=True)
        base = pick(tok_off) + (t - pick(tile_off)) * MOE_TILE
        lane_t = lax.broadcasted_iota(jnp.int32, (N_CLASSES, LANES), 1)
        tiles_ref[...] = sum(jnp.where(lane_t == j, v, 0.0)
                             for j, v in enumerate([cls_t, base, n_tiles + jnp.zeros_like(t)]))

        tile_off_col, tok_off_col = _to_sublanes(tile_off), _to_sublanes(tok_off)
        class_id = lax.broadcasted_iota(jnp.int32, (N_CLASSES, LANES), 0).astype(F32)

        def per_block(j, carry):
            at = pl.ds(j, 1)
            own = class_id == cls_s[at, :]
            look = lambda col: jnp.sum(jnp.where(own, col, 0.0), axis=0, keepdims=True)
            rank_row = rank_s[at, :]
            slot_ref[at, :] = (look(tile_off_col) * MOE_TILE + rank_row).astype(jnp.int32)
            pos_ref[at, :] = (look(tok_off_col) + rank_row).astype(jnp.int32)
            return carry

        lax.fori_loop(0, slot_ref.shape[0], per_block, 0)


def _plan_order_kernel(pos_ref, order_ref, *, n_tokens):
    def place(t, carry):
        order_ref[pos_ref[t]] = t
        return carry

    lax.fori_loop(0, n_tokens, place, 0, unroll=32)
    for j in range(MOE_TILE):
        order_ref[n_tokens + j] = 0


def _route_plan(route, n_tokens):
    n_rows = route.shape[0]
    rows = PLAN_ROWS
    n_steps = n_rows // rows
    dense = lambda dtype: jax.ShapeDtypeStruct((n_rows // LANES, LANES), dtype)
    whole = lambda r, c: pl.BlockSpec((r, c), lambda i: (0, 0))
    slot, pos, wa, wb, tiles = pl.pallas_call(
        functools.partial(_plan_kernel, n_steps=n_steps, n_tokens=n_tokens),
        grid=(n_steps,), in_specs=[pl.BlockSpec((rows, LANES), lambda i: (i, 0))],
        out_specs=[whole(n_rows // LANES, LANES)] * 4 + [whole(N_CLASSES, LANES)],
        out_shape=[dense(jnp.int32), dense(jnp.int32), dense(F32), dense(F32),
                   jax.ShapeDtypeStruct((N_CLASSES, LANES), F32)],
        scratch_shapes=[pltpu.VMEM((1, N_CLASSES), F32),
                        pltpu.VMEM((n_rows // LANES, LANES), F32),
                        pltpu.VMEM((n_rows // LANES, LANES), F32)],
        compiler_params=_cparams(("arbitrary",)), name="plan",
    )(route)
    order = pl.pallas_call(
        functools.partial(_plan_order_kernel, n_tokens=n_tokens),
        in_specs=[pl.BlockSpec(memory_space=pltpu.SMEM)],
        out_specs=pl.BlockSpec(memory_space=pltpu.SMEM),
        out_shape=jax.ShapeDtypeStruct((n_tokens + MOE_TILE,), jnp.int32), name="plan_order",
    )(pos.reshape(-1))
    tile_cls = tiles[:, 0].astype(jnp.int32)
    tile_base = tiles[:, 1].astype(jnp.int32)
    n_tiles = tiles[0:1, 2].astype(jnp.int32)
    return tile_cls, tile_base, n_tiles, order, wa.reshape(-1), wb.reshape(-1), slot.reshape(-1)


def _moe_kernel(tcls_ref, tbase_ref, nt_ref, order_ref, wa_ref, wb_ref,
                hx_hbm, gf_ref, wg_ref, wu_ref, wd_ref, y_ref,
                xbuf, wgu_s, wd_s, sem):
    t = pl.program_id(0)
    n_tiles = nt_ref[0]
    buf = t % 2
    pair = EXPERTS_PER_GROUP * EXPERTS_PER_GROUP

    def gather(tile, b):
        base = tbase_ref[tile]
        for j in range(MOE_TILE):
            pltpu.make_async_copy(hx_hbm.at[order_ref[base + j]], xbuf.at[b, j], sem.at[b]).start()

    @pl.when(t == 0)
    def _():
        gather(0, 0)

    @pl.when(t + 1 < n_tiles)
    def _():
        gather(t + 1, 1 - buf)

    @pl.when(t < n_tiles)
    def _():
        cls = tcls_ref[t]
        grp = cls // pair

        @pl.when((t == 0) | (grp != tcls_ref[jnp.maximum(t - 1, 0)] // pair))
        def _():
            for e in range(EXPERTS_PER_GROUP):
                wgu_s[e, :, 0:D_EXPERT] = wg_ref[0, e].astype(BF16)
                wgu_s[e, :, D_EXPERT:2 * D_EXPERT] = wu_ref[0, e].astype(BF16)
                wd_s[e] = wd_ref[0, e].astype(BF16)

        pltpu.make_async_copy(hx_hbm.at[pl.ds(0, MOE_TILE)], xbuf.at[buf], sem.at[buf]).wait()
        h = xbuf[buf].reshape(MOE_TILE, D_MODEL)
        xn = _rms(h, gf_ref[...]).astype(BF16)
        ea = (cls % pair) // EXPERTS_PER_GROUP
        eb = cls % EXPERTS_PER_GROUP

        def column(w_ref):
            base = tbase_ref[t]
            sub = lax.broadcasted_iota(jnp.int32, (SUBLANES, 1), 0)
            chunks = []
            for c in range(MOE_TILE // SUBLANES):
                acc = jnp.zeros((SUBLANES, 1), F32)
                for k in range(SUBLANES):
                    acc = jnp.where(sub == k, w_ref[order_ref[base + c * SUBLANES + k]], acc)
                chunks.append(acc)
            return jnp.concatenate(chunks, axis=0)

        gus = [jnp.dot(xn, wgu_s[e], preferred_element_type=F32) for e in (ea, eb)]
        acts = [(jax.nn.silu(gu[:, 0:D_EXPERT]) * gu[:, D_EXPERT:2 * D_EXPERT]).astype(BF16)
                for gu in gus]
        oa, ob = [jnp.dot(act, wd_s[e], preferred_element_type=F32)
                  for act, e in zip(acts, (ea, eb))]
        y = h + column(wa_ref) * oa + column(wb_ref) * ob
        y_ref[...] = y.reshape(MOE_TILE, D_MODEL // LANES, LANES)

    @pl.when(t >= n_tiles)
    def _():
        y_ref[...] = jnp.zeros(y_ref.shape, F32)


def _moe(hx, gf, w_gate, w_up, w_down, plan, n_slots):
    tile_cls, tile_base, n_tiles, order, wa, wb, _ = plan
    pair = EXPERTS_PER_GROUP * EXPERTS_PER_GROUP
    assert n_slots // MOE_TILE <= tile_cls.shape[0]
    grouped = lambda w: w.reshape(N_GROUPS, EXPERTS_PER_GROUP, *w.shape[1:])
    wspec = lambda r, c: pl.BlockSpec((1, EXPERTS_PER_GROUP, r, c),
                                      lambda t, tc, *_: (tc[t] // pair, 0, 0, 0),
                                      pipeline_mode=pl.Buffered(1))
    grid_spec = pltpu.PrefetchScalarGridSpec(
        num_scalar_prefetch=6, grid=(n_slots // MOE_TILE,),
        in_specs=[pl.BlockSpec(memory_space=pl.ANY),
                  pl.BlockSpec(gf.shape, lambda t, *_: (0, 0)),
                  wspec(D_MODEL, D_EXPERT), wspec(D_MODEL, D_EXPERT), wspec(D_EXPERT, D_MODEL)],
        out_specs=pl.BlockSpec((MOE_TILE, D_MODEL // LANES, LANES), lambda t, *_: (t, 0, 0)),
        scratch_shapes=[pltpu.VMEM((2, MOE_TILE, TOKEN_ROWS, LANES), F32),
                        pltpu.VMEM((EXPERTS_PER_GROUP, D_MODEL, 2 * D_EXPERT), BF16),
                        pltpu.VMEM((EXPERTS_PER_GROUP, D_EXPERT, D_MODEL), BF16),
                        pltpu.SemaphoreType.DMA((2,))])
    return pl.pallas_call(
        _moe_kernel, grid_spec=grid_spec,
        out_shape=jax.ShapeDtypeStruct((n_slots, D_MODEL // LANES, LANES), F32),
        compiler_params=_cparams(("arbitrary",)), name="moe",
    )(tile_cls, tile_base, n_tiles, order, wa, wb, hx, gf,
      grouped(w_gate), grouped(w_up), grouped(w_down))


def _unsort_kernel(slot_ref, ys_hbm, y_ref, buf, sem, *, rows, row0):
    i = pl.program_id(0)
    b = i % 2

    def gather(step, dst):
        for j in range(rows):
            pltpu.make_async_copy(ys_hbm.at[slot_ref[row0 + step * rows + j]], buf.at[dst, j],
                                  sem.at[dst]).start()

    @pl.when(i == 0)
    def _():
        gather(0, 0)

    @pl.when(i + 1 < pl.num_programs(0))
    def _():
        gather(i + 1, 1 - b)

    pltpu.make_async_copy(ys_hbm.at[pl.ds(0, rows)], buf.at[b], sem.at[b]).wait()
    y_ref[...] = buf[b].reshape(rows, D_MODEL)


def _unsort(ys, token_slot, *, n, row0, rows):
    grid_spec = pltpu.PrefetchScalarGridSpec(
        num_scalar_prefetch=1, grid=(n // rows,),
        in_specs=[pl.BlockSpec(memory_space=pl.ANY)],
        out_specs=pl.BlockSpec((rows, D_MODEL), lambda i, *_: (i, 0)),
        scratch_shapes=[pltpu.VMEM((2, rows, D_MODEL // LANES, LANES), F32),
                        pltpu.SemaphoreType.DMA((2,))])
    return pl.pallas_call(
        functools.partial(_unsort_kernel, rows=rows, row0=row0), grid_spec=grid_spec,
        out_shape=jax.ShapeDtypeStruct((n, D_MODEL), F32),
        compiler_params=_cparams(("arbitrary",)), name="unsort",
    )(token_slot, ys)


def kernel(x_prompt, x_sample, cache_k, cache_v, state_conv, g_norm_mix, w_in, g_q, g_k, w_conv,
           g_attn_out, g_conv_out, w_out, g_norm_ffn, w_router_group, w_router_expert,
           w_gate, w_up, w_down):
    depth = g_norm_mix.shape[0]
    assert depth == 1
    nb, seq, _ = x_prompt.shape
    db, dec_seq, _ = x_sample.shape
    assert dec_seq == 1
    l = 0

    gmix = g_norm_mix[l][None, :]
    gq = jnp.tile(g_q[l], N_HEADS)[None, :]
    gk = jnp.tile(g_k[l], N_HEADS)[None, :]
    ga = g_attn_out[l][None, :]
    gc = g_conv_out[l][None, :]
    gf = g_norm_ffn[l][None, :]
    w_router = jnp.concatenate(
        [w_router_expert[l], w_router_group[l],
         jnp.zeros((D_MODEL, LANES - N_EXPERTS - N_GROUPS), F32)], axis=1)
    w_router_hi = w_router.astype(BF16)
    w_router_split = jnp.concatenate(
        [w_router_hi, (w_router - w_router_hi.astype(F32)).astype(BF16)], axis=1)

    cos_p, sin_p = _rope_tables(np.arange(seq))
    xp = x_prompt.reshape(nb * seq, D_MODEL)
    q, k, v, bc, tail, kt, vt = _inproj(xp, gmix, w_in[l].astype(BF16), gq, gk, cos_p, sin_p,
                                        w_conv[l], tm=256, seq_len=seq, hi=False)
    attn = _attention(q.reshape(nb, seq, D_ATTN), k.reshape(nb, seq, D_ATTN),
                      v.reshape(nb, seq, D_ATTN))
    n_prompt = nb * seq
    n_total = n_prompt + db
    n_rows = -(-n_total // TOKEN_PAD) * TOKEN_PAD
    hx, route = _outproj(attn.reshape(n_prompt, D_ATTN), bc, xp, ga, gc, w_out[l].astype(BF16),
                         gf, w_router_split, tm=512, hi=False, n_rows=n_rows, row0=0, n_parts=2)
    k_prompt = jnp.transpose(kt, (0, 3, 1, 2))
    v_prompt = jnp.transpose(vt, (0, 3, 1, 2))
    conv_prompt = tail.reshape(nb, SUBLANES, D_CONV)[:, SUBLANES - (CONV_WIDTH - 1):]

    cos_s, sin_s = _rope_tables(np.full((db,), PAST_LEN))
    xs = x_sample.reshape(db, D_MODEL)
    st = state_conv[l]
    qs, ks, vs, bcs, us = _inproj(xs, gmix, w_in[l], gq, gk, cos_s, sin_s, w_conv[l],
                                  tm=db, seq_len=db, hi=True, prev=(st[:, 1], st[:, 0]))
    heads = lambda t: t.reshape(db, N_HEADS, HEAD_DIM)
    attn_s = _attention_sample(heads(qs), heads(ks), heads(vs),
                               jnp.transpose(cache_k[l], (0, 2, 3, 1)),
                               jnp.transpose(cache_v[l], (0, 2, 3, 1)))
    hx, route = _outproj(attn_s.reshape(db, D_ATTN), bcs, xs, ga, gc, w_out[l], gf, w_router,
                         tm=db, hi=True, n_rows=n_rows, row0=n_prompt, into=(hx, route))

    n_cls = N_GROUPS * EXPERTS_PER_GROUP * (EXPERTS_PER_GROUP - 1) // 2
    n_slots = -(-(n_total + n_cls * (MOE_TILE - 1)) // MOE_TILE) * MOE_TILE
    assert n_slots // MOE_TILE <= N_CLASSES and n_rows % PLAN_ROWS == 0
    plan = _route_plan(route, n_total)
    ys = _moe(hx, gf, w_gate[l], w_up[l], w_down[l], plan, n_slots)
    y_prompt = _unsort(ys, plan[-1], n=n_prompt, row0=0, rows=512).reshape(nb, seq, D_MODEL)
    y_sample = _unsort(ys, plan[-1], n=db, row0=n_prompt, rows=db).reshape(db, 1, D_MODEL)
    k_sample = ks.reshape(db, 1, N_HEADS, HEAD_DIM)
    v_sample = vs.reshape(db, 1, N_HEADS, HEAD_DIM)
    conv_sample = jnp.stack([st[:, 1], us], axis=1)

    return (y_prompt, y_sample, k_prompt[None], v_prompt[None], conv_prompt[None],
            k_sample[None], v_sample[None], conv_sample[None])
```

```python
import functools

import numpy as np
import jax
import jax.numpy as jnp
from jax import lax
from jax.experimental import pallas as pl
from jax.experimental.pallas import tpu as pltpu

F32 = jnp.float32
BF16 = jnp.bfloat16

D_MODEL = 1024
HEAD_DIM = 64
N_HEADS = 12
D_ATTN = N_HEADS * HEAD_DIM
D_CONV = D_MODEL - D_ATTN
D_IN_PROJ = 3 * D_ATTN + 3 * D_CONV
CONV_WIDTH = 3
DILATIONS = (1, 4, 16)
ATT_BLOCK = 128
ATT_UNROLL = 16
REGROUP_STEP = 4
MAX_WINDOW = 2048
PAST_LEN = 8192
ATTN_SCALE = HEAD_DIM ** -0.5
LOG2_E = float(np.log2(np.e))
ROPE_THETA = 10000.0
EPS = 1e-6
N_GROUPS = 4
EXPERTS_PER_GROUP = 8
N_EXPERTS = N_GROUPS * EXPERTS_PER_GROUP
D_EXPERT = 256
MOE_TILE = 192
N_CLASSES = N_GROUPS * EXPERTS_PER_GROUP * EXPERTS_PER_GROUP
PLAN_ROWS = 768
TOKEN_PAD = 512

LANES = 128
SUBLANES = 8
MXU_DIM = 256
TOKEN_ROWS = D_MODEL // LANES
NEG = -1e30
VMEM_LIMIT = 48 * 1024 * 1024


def _cparams(sem):
    return pltpu.CompilerParams(dimension_semantics=sem, vmem_limit_bytes=VMEM_LIMIT)


def _mm(a, b, hi):
    if hi:
        return jnp.dot(a, b, preferred_element_type=F32, precision=lax.Precision.HIGHEST)
    return jnp.dot(a.astype(BF16), b.astype(BF16), preferred_element_type=F32)


def _rms(x, g):
    return x * lax.rsqrt(jnp.mean(x * x, axis=-1, keepdims=True) + EPS) * g


def _rope_tables(pos):
    half = HEAD_DIM // 2
    inv = ROPE_THETA ** (-np.arange(half, dtype=np.float64) / half)
    ang = np.asarray(pos, np.float64)[:, None] * inv[None, :]
    cos, sin = np.cos(ang), np.sin(ang)
    cos2 = np.concatenate([cos, cos, cos, cos], axis=-1)
    sin2 = np.concatenate([-sin, sin, -sin, sin], axis=-1)
    return jnp.asarray(cos2, F32), jnp.asarray(sin2, F32)


def _inproj_kernel(*refs, tm, tiles_per_seq, keep_tiles, hi, given_prev, n_parts, q_scale):
    if given_prev:
        (x_ref, gmix_ref, w_ref, gq_ref, gk_ref, cos_ref, sin_ref, wconv_ref, p1_ref, p2_ref,
         q_ref, k_ref, v_ref, bc_ref, u_ref) = refs
    else:
        (x_ref, gmix_ref, w_ref, gq_ref, gk_ref, cos_ref, sin_ref, wconv_ref,
         q_ref, k_ref, v_ref, bc_ref, tail_ref, kt_ref, vt_ref, ubuf) = refs
    part_rows = tm // n_parts

    r_i = lax.broadcasted_iota(jnp.int32, (MXU_DIM, MXU_DIM), 0) // HEAD_DIM
    c_i = lax.broadcasted_iota(jnp.int32, (MXU_DIM, MXU_DIM), 1) // HEAD_DIM
    head_sum = jnp.where(r_i == c_i, 1.0, 0.0).astype(F32 if hi else BF16)
    lane = lax.broadcasted_iota(jnp.int32, (1, D_ATTN), 1)
    first_half = (lane % HEAD_DIM) < (HEAD_DIM // 2)
    wc = wconv_ref[...]

    def head_ssq(z):
        sq = z * z
        return jnp.concatenate(
            [_mm(sq[:, c * MXU_DIM:(c + 1) * MXU_DIM], head_sum, hi)
             for c in range(D_ATTN // MXU_DIM)], axis=-1)

    def norm_rope(z, ssq, g_ref, cos, sin):
        zn = z * lax.rsqrt(ssq * (1.0 / HEAD_DIM) + EPS) * g_ref[...]
        partner = jnp.where(first_half,
                            pltpu.roll(zn, D_ATTN - HEAD_DIM // 2, 1),
                            pltpu.roll(zn, HEAD_DIM // 2, 1))
        return zn * cos + partner * sin

    def conv_gate(b_gate, u2, u1, u):
        return b_gate * (wc[0:1, :] * u2 + wc[1:2, :] * u1 + wc[2:3, :] * u)

    def part(p, transposed):
        rows = pl.ds(p * part_rows, part_rows)
        xn = _rms(x_ref[rows, :], gmix_ref[...])
        if not hi:
            xn = xn.astype(BF16)
        zq = _mm(xn, w_ref[:, 0:D_ATTN], hi)
        zk = _mm(xn, w_ref[:, D_ATTN:2 * D_ATTN], hi)
        ssq_q = head_ssq(zq)
        v = _mm(xn, w_ref[:, 2 * D_ATTN:3 * D_ATTN], hi)
        ssq_k = head_ssq(zk)
        c0 = 3 * D_ATTN
        b_gate = _mm(xn, w_ref[:, c0:c0 + D_CONV], hi)
        c_gate = _mm(xn, w_ref[:, c0 + D_CONV:c0 + 2 * D_CONV], hi)
        u_in = _mm(xn, w_ref[:, c0 + 2 * D_CONV:c0 + 3 * D_CONV], hi)
        yield
        reps = D_ATTN // LANES
        cos = jnp.concatenate([cos_ref[rows, :]] * reps, axis=-1)
        sin = jnp.concatenate([sin_ref[rows, :]] * reps, axis=-1)
        q_ref[rows, :] = norm_rope(zq, ssq_q, gq_ref, cos, sin) * q_scale
        k = norm_rope(zk, ssq_k, gk_ref, cos, sin)
        k_ref[rows, :] = k
        v_ref[rows, :] = v
        if transposed:
            cols = slice(p * part_rows, (p + 1) * part_rows)
            kt_ref[0, :, :, cols] = k.T.reshape(N_HEADS, HEAD_DIM, part_rows)
            vt_ref[0, :, :, cols] = v.T.reshape(N_HEADS, HEAD_DIM, part_rows)
        u = c_gate * u_in
        if given_prev:
            u_ref[rows, :] = u
            bc_ref[rows, :] = conv_gate(b_gate, p2_ref[rows, :], p1_ref[rows, :], u)
        else:
            ubuf[pl.ds(SUBLANES + p * part_rows, part_rows), :] = u
        return b_gate

    def body(transposed):
        stages = [part(p, transposed) for p in range(n_parts)]
        for stage in stages:
            next(stage)
        gates = []
        for stage in stages:
            try:
                next(stage)
            except StopIteration as done:
                gates.append(done.value)
        if not given_prev:
            u = ubuf[SUBLANES:tm + SUBLANES, :]
            u1 = ubuf[SUBLANES - 1:tm + SUBLANES - 1, :]
            u2 = ubuf[SUBLANES - 2:tm + SUBLANES - 2, :]
            bc_ref[...] = conv_gate(jnp.concatenate(gates, axis=0), u2, u1, u)
            tail_ref[...] = ubuf[tm:tm + SUBLANES, :]

    if given_prev:
        body(False)
        return

    i = pl.program_id(0)

    @pl.when(i % tiles_per_seq == 0)
    def _():
        ubuf[0:SUBLANES, :] = jnp.zeros((SUBLANES, D_CONV), F32)

    @pl.when(i % tiles_per_seq != 0)
    def _():
        ubuf[0:SUBLANES, :] = ubuf[tm:tm + SUBLANES, :]

    kept = i % tiles_per_seq >= tiles_per_seq - keep_tiles
    pl.when(kept)(functools.partial(body, True))
    pl.when(jnp.logical_not(kept))(functools.partial(body, False))


def _inproj(x, gmix, w_in, gq, gk, cos, sin, w_conv, *, tm, seq_len, hi, q_scale, n_parts=1,
            prev=None):
    n = x.shape[0]
    nt = n // tm
    tiles_per_seq = seq_len // tm
    row = lambda w: pl.BlockSpec((tm, w), lambda i: (i, 0))
    full = lambda a: pl.BlockSpec(a.shape, lambda i: (0,) * a.ndim)
    tab = pl.BlockSpec((tm, LANES), lambda i: (i % tiles_per_seq, 0))
    once = pl.BlockSpec(w_in.shape, lambda i: (0, 0), pipeline_mode=pl.Buffered(1))
    in_specs = [row(D_MODEL), full(gmix), once, full(gq), full(gk), tab, tab, full(w_conv)]
    args = [x, gmix, w_in, gq, gk, cos, sin, w_conv]
    out_shape = [jax.ShapeDtypeStruct((n, D_ATTN), F32)] * 3 + [jax.ShapeDtypeStruct((n, D_CONV), F32)]
    out_specs = [row(D_ATTN)] * 3 + [row(D_CONV)]
    scratch = []
    keep_tiles = min(MAX_WINDOW, seq_len) // tm
    if prev is not None:
        in_specs += [row(D_CONV), row(D_CONV)]
        args += list(prev)
        out_shape.append(jax.ShapeDtypeStruct((n, D_CONV), F32))
        out_specs.append(row(D_CONV))
    else:
        n_seq = n // seq_len
        out_shape.append(jax.ShapeDtypeStruct((n_seq * SUBLANES, D_CONV), F32))
        out_specs.append(pl.BlockSpec((SUBLANES, D_CONV), lambda i: (i // tiles_per_seq, 0)))
        kept = pl.BlockSpec(
            (1, N_HEADS, HEAD_DIM, tm),
            lambda i: (i // tiles_per_seq, 0, 0,
                       jnp.maximum(i % tiles_per_seq - (tiles_per_seq - keep_tiles), 0)))
        out_shape += [jax.ShapeDtypeStruct((n_seq, N_HEADS, HEAD_DIM, keep_tiles * tm), F32)] * 2
        out_specs += [kept, kept]
        scratch.append(pltpu.VMEM((tm + SUBLANES, D_CONV), F32))
    return pl.pallas_call(
        functools.partial(_inproj_kernel, tm=tm, tiles_per_seq=tiles_per_seq,
                          keep_tiles=keep_tiles, hi=hi, given_prev=prev is not None,
                          n_parts=n_parts, q_scale=q_scale),
        grid=(nt,), in_specs=in_specs, out_specs=out_specs, out_shape=out_shape,
        scratch_shapes=scratch, compiler_params=_cparams(("arbitrary",)),
        name="inproj_hi" if hi else "inproj",
    )(*args)


def _attn_kernel(q_ref, k_ref, v_ref, o_ref, num_ref, den_ref, m_ref, bias_ref,
                 qd_ref, kd_ref, vd_ref, numd_ref, dend_ref, md_ref, tmp_ref, *, seq_len):
    blk = ATT_BLOCK
    qi = lax.broadcasted_iota(jnp.int32, (blk, 2 * blk), 0)
    kj = lax.broadcasted_iota(jnp.int32, (blk, 2 * blk), 1)
    bias_ref[...] = jnp.where((kj >= qi) & (kj <= qi + blk), 0.0, NEG)
    head0 = lax.broadcasted_iota(jnp.int32, (1, LANES), 1) < HEAD_DIM
    dn = (((1,), (1,)), ((), ()))

    def rows(start, size, d):
        return pl.ds(start, size) if d == 1 else pl.ds(start, size, stride=d)

    natural = (lambda at: q_ref[0, at, :], lambda at: k_ref[0, at, :], lambda at: v_ref[0, at, :],
               (num_ref, den_ref, m_ref))
    regrouped = (lambda at: qd_ref[at, :], lambda at: kd_ref[at, :], lambda at: vd_ref[at, :],
                 (numd_ref, dend_ref, md_ref))

    def group(src, d, r, n0, u, first, mode):
        q_at, k_at, v_at, (num_acc, den_acc, m_acc) = src
        q_rows = rows(n0 * (blk * d) + r, u * blk, d)
        if first:
            k_rows = rows(r, u * blk, d)
        else:
            k_rows = rows((n0 - 1) * (blk * d) + r, (u + 1) * blk, d)
        qa = q_at(q_rows)
        ka = k_at(k_rows).astype(BF16)
        va = v_at(k_rows)
        qa0 = jnp.where(head0, qa, 0.0).astype(BF16)
        qa1 = jnp.where(head0, 0.0, qa).astype(BF16)
        va0 = jnp.where(head0, va, 1.0).astype(BF16)
        va1 = jnp.where(head0, 1.0, va).astype(BF16)
        if mode != "init":
            m_old = m_acc[q_rows, :]
            num_old = num_acc[q_rows, :]
            den_old = den_acc[q_rows, :]
        key_slices, scores = [], []
        for i in range(u):
            qs = slice(i * blk, (i + 1) * blk)
            if first and i == 0:
                ks = slice(0, blk)
                bias = bias_ref[:, blk:2 * blk]
            else:
                k0 = (i - 1) * blk if first else i * blk
                ks = slice(k0, k0 + 2 * blk)
                bias = bias_ref[...]
            key_slices.append(ks)
            scores.append(
                (lax.dot_general(qa0[qs], ka[ks], dn, preferred_element_type=F32) + bias,
                 lax.dot_general(qa1[qs], ka[ks], dn, preferred_element_type=F32) + bias))
        yield
        probs, ms = [], []
        for s0, s1 in scores:
            m0 = jnp.max(s0, axis=-1, keepdims=True)
            m1 = jnp.max(s1, axis=-1, keepdims=True)
            probs.append((jnp.exp2(s0 - m0).astype(BF16), jnp.exp2(s1 - m1).astype(BF16)))
            ms.append(jnp.where(head0, m0, m1))
        yield
        nums, dens = [], []
        for (p0, p1), ks in zip(probs, key_slices):
            r0 = jnp.dot(p0, va0[ks], preferred_element_type=F32)
            r1 = jnp.dot(p1, va1[ks], preferred_element_type=F32)
            nums.append(jnp.where(head0, r0, r1))
            dens.append(pltpu.roll(jnp.where(head0, r1, r0), HEAD_DIM, 1))
        num = jnp.concatenate(nums, axis=0)
        den = jnp.concatenate(dens, axis=0)
        m = jnp.concatenate(ms, axis=0)
        if mode == "init":
            return [(num_acc, q_rows, num), (den_acc, q_rows, den), (m_acc, q_rows, m)]
        m_new = jnp.maximum(m_old, m)
        a = jnp.exp2(m_old - m_new)
        b = jnp.exp2(m - m_new)
        num = num_old * a + num * b
        den = den_old * a + den * b
        if mode == "merge":
            return [(num_acc, q_rows, num), (den_acc, q_rows, den), (m_acc, q_rows, m_new)]
        return [(o_ref.at[0], q_rows, num / den)]

    def run(groups):
        stages = [group(*g) for g in groups]
        for _ in range(2):
            for stage in stages:
                next(stage)
        stores = []
        for stage in stages:
            try:
                next(stage)
            except StopIteration as done:
                stores += done.value
        for ref, at, val in stores:
            ref[at, :] = val

    def branch(src, d, n_seq, sub_len, seq_stride, mode):
        nb = sub_len // blk
        u = min(ATT_UNROLL, nb)
        if nb == u:
            per_body = ATT_UNROLL // u

            def several(rr, carry):
                run([(src, d, (rr * per_body + t) * seq_stride, 0, u, True, mode)
                     for t in range(per_body)])
                return carry

            lax.fori_loop(0, n_seq // per_body, several, 0)
        else:
            def one(r, carry):
                run([(src, d, r * seq_stride, 0, u, True, mode)])

                def later_blocks(g, c):
                    run([(src, d, r * seq_stride, g * u, u, False, mode)])
                    return c

                lax.fori_loop(1, nb // u, later_blocks, 0)
                return carry

            lax.fori_loop(0, n_seq, one, 0)

    step = REGROUP_STEP
    wide = step * step
    assert DILATIONS == (1, step, wide)
    part = seq_len // wide

    def regroup(load, dst_ref):
        for a in range(step):
            tmp_ref[a * step * part:(a + 1) * step * part, :] = load(rows(a, step * part, step))
        for a in range(step):
            for b in range(step):
                s = a * step + b
                dst_ref[s * part:(s + 1) * part, :] = tmp_ref[rows(a * step * part + b, part, step), :]

    def ungroup(src_ref, dst_ref):
        for a in range(step):
            for b in range(step):
                s = a * step + b
                tmp_ref[rows(a * step * part + b, part, step), :] = src_ref[s * part:(s + 1) * part, :]
        for a in range(step):
            dst_ref[rows(a, step * part, step), :] = tmp_ref[a * step * part:(a + 1) * step * part, :]

    for load, dst in zip(natural[:3], (qd_ref, kd_ref, vd_ref)):
        regroup(load, dst)
    branch(regrouped, 1, wide, part, part, "init")
    for src, dst in zip(regrouped[3], natural[3]):
        ungroup(src, dst)
    branch(natural, 1, 1, seq_len, 0, "merge")
    branch(natural, step, step, seq_len // step, 1, "final")


def _attention(q, k, v):
    b, s, _ = q.shape
    spec = pl.BlockSpec((1, s, LANES), lambda i, p: (i, 0, p))
    return pl.pallas_call(
        functools.partial(_attn_kernel, seq_len=s),
        grid=(b, D_ATTN // LANES), in_specs=[spec] * 3, out_specs=spec,
        out_shape=jax.ShapeDtypeStruct((b, s, D_ATTN), F32),
        scratch_shapes=[pltpu.VMEM((s, LANES), F32)] * 3 + [pltpu.VMEM((ATT_BLOCK, 2 * ATT_BLOCK), F32)]
        + [pltpu.VMEM((s, LANES), F32)] * 7,
        compiler_params=_cparams(("arbitrary", "arbitrary")), name="attn_prompt",
    )(q, k, v)


def _attn_sample_kernel(q_ref, kn_ref, vn_ref, kc_ref, vc_ref, o_ref, *, w_buf, heads):
    age = w_buf - lax.broadcasted_iota(jnp.int32, (1, w_buf), 1)
    valid = [(age % d == 0) & (age <= d * ATT_BLOCK) for d in DILATIONS]
    diag = (lax.broadcasted_iota(jnp.int32, (HEAD_DIM, HEAD_DIM), 0)
            == lax.broadcasted_iota(jnp.int32, (HEAD_DIM, HEAD_DIM), 1))
    to_col = lambda row: jnp.sum(jnp.where(diag, row, 0.0), axis=1, keepdims=True)
    to_row = lambda col: jnp.sum(jnp.where(diag, col, 0.0), axis=0, keepdims=True)
    q_rows, kn_rows, vn_rows = q_ref[0], kn_ref[0], vn_ref[0]
    s_self_all = jnp.sum(kn_rows * q_rows, axis=1, keepdims=True)
    for h in range(heads):
        q = to_col(q_rows[h:h + 1])
        vn = to_col(vn_rows[h:h + 1])
        s = jnp.sum(kc_ref[0, h] * q, axis=0, keepdims=True)
        s_self = s_self_all[h:h + 1]
        parts = []
        for ok in valid:
            sd = jnp.where(ok, s, NEG)
            m = jnp.maximum(jnp.max(sd, axis=1, keepdims=True), s_self)
            p = jnp.exp(sd - m)
            p_self = jnp.exp(s_self - m)
            parts.append((p, p_self, jnp.sum(p, axis=1, keepdims=True) + p_self, m))
        m_all = jnp.maximum(jnp.maximum(parts[0][3], parts[1][3]), parts[2][3])
        w = jnp.zeros((1, w_buf), F32)
        w_self = jnp.zeros((1, 1), F32)
        den_t = jnp.zeros((1, 1), F32)
        for p, p_self, den, m in parts:
            c = jnp.exp(m - m_all)
            w = w + p * c
            w_self = w_self + p_self * c
            den_t = den_t + den * c
        num_t = jnp.sum(vc_ref[0, h] * w, axis=1, keepdims=True) + vn * w_self
        o_ref[0, h:h + 1, :] = to_row(num_t / den_t)


def _attention_sample(q, kn, vn, cache_kt, cache_vt):
    db, heads, _, w_buf = cache_kt.shape
    tok = pl.BlockSpec((1, heads, HEAD_DIM), lambda i: (i, 0, 0))
    cache = pl.BlockSpec((1, heads, HEAD_DIM, w_buf), lambda i: (i, 0, 0, 0))
    return pl.pallas_call(
        functools.partial(_attn_sample_kernel, w_buf=w_buf, heads=heads),
        grid=(db,), in_specs=[tok, tok, tok, cache, cache], out_specs=tok,
        out_shape=jax.ShapeDtypeStruct((db, heads, HEAD_DIM), F32),
        compiler_params=_cparams(("arbitrary",)), name="attn_sample",
    )(q, kn, vn, cache_kt, cache_vt)


def _outproj_kernel(attn_ref, bc_ref, x_ref, ga_ref, gc_ref, wo_ref, gf_ref, wr_ref, *rest,
                    hi, n_tiles, n_parts):
    hx_ref, route_ref = rest[-2:]
    part_rows = x_ref.shape[0] // n_parts

    def part(k):
        rows = pl.ds(k * part_rows, part_rows)
        a = _rms(attn_ref[rows, :], ga_ref[...])
        c = _rms(bc_ref[rows, :], gc_ref[...])
        mix = _mm(a, wo_ref[0:D_ATTN, :], hi) + _mm(c, wo_ref[D_ATTN:D_MODEL, :], hi)
        yield
        h = x_ref[rows, :] + mix
        hn = _rms(h, gf_ref[...])
        if hi:
            lg = _mm(hn, wr_ref[...], True)
        else:
            hn_hi = hn.astype(BF16)
            hn_lo = (hn - hn_hi.astype(F32)).astype(BF16)
            both = jnp.dot(hn_hi, wr_ref[...], preferred_element_type=F32)
            lg = (both[:, :LANES] + both[:, LANES:]
                  + jnp.dot(hn_lo, wr_ref[:, :LANES], preferred_element_type=F32))
        yield
        lane_i = lax.broadcasted_iota(jnp.int32, lg.shape, 1)
        lane = lane_i.astype(F32)
        lane_group = (lane_i // EXPERTS_PER_GROUP).astype(F32)
        is_e = lane_i < N_EXPERTS
        is_g = (lane_i >= N_EXPERTS) & (lane_i < N_EXPERTS + N_GROUPS)
        big = float(LANES)
        first_at = lambda t, v: jnp.min(jnp.where(t == v, lane, big), axis=-1, keepdims=True)

        gl = jnp.where(is_g, lg, NEG)
        mg = jnp.max(gl, axis=-1, keepdims=True)
        p_g = 1.0 / jnp.sum(jnp.exp(gl - mg), axis=-1, keepdims=True)
        g_idx = first_at(gl, mg) - float(N_EXPERTS)
        el = jnp.where(is_e & (lane_group == g_idx), lg, NEG)
        v1 = jnp.max(el, axis=-1, keepdims=True)
        i1 = first_at(el, v1)
        el2 = jnp.where(lane == i1, NEG, el)
        v2 = jnp.max(el2, axis=-1, keepdims=True)
        i2 = first_at(el2, v2)
        e2 = jnp.exp(v2 - v1)
        w1 = p_g / (1.0 + e2)
        w2 = p_g * e2 / (1.0 + e2)
        route_ref[rows, :] = sum(jnp.where(lane_i == j, val, 0.0)
                                 for j, val in enumerate([i1, i2, w1, w2]))
        hx_ref[rows, :, :] = h.reshape(part_rows, TOKEN_ROWS, LANES)

    stages = [part(k) for k in range(n_parts)]
    for _ in range(2):
        for stage in stages:
            next(stage)
    for stage in stages:
        for _ in stage:
            pass

    @pl.when(pl.program_id(0) >= n_tiles)
    def _():
        hx_ref[...] = jnp.zeros(hx_ref.shape, F32)
        route_ref[...] = jnp.zeros(route_ref.shape, F32)


def _outproj(attn, bc, x, ga, gc, w_out, gf, w_router, *, tm, hi, n_rows, row0, n_parts=1,
             into=None):
    n = x.shape[0]
    n_tiles = n // tm
    n_fill = 0 if into is not None else -(-(n_rows - n) // tm)
    row = lambda w: pl.BlockSpec((tm, w), lambda i: (jnp.minimum(i, n_tiles - 1), 0))
    full = lambda a: pl.BlockSpec(a.shape, lambda i: (0,) * a.ndim)
    in_specs = [row(D_ATTN), row(D_CONV), row(D_MODEL), full(ga), full(gc), full(w_out),
                full(gf), full(w_router)]
    args = [attn, bc, x, ga, gc, w_out, gf, w_router]
    aliases = {}
    if into is not None:
        in_specs += [pl.BlockSpec(memory_space=pl.ANY)] * 2
        args += list(into)
        aliases = {len(args) - 2: 0, len(args) - 1: 1}
    return pl.pallas_call(
        functools.partial(_outproj_kernel, hi=hi, n_tiles=n_tiles, n_parts=n_parts),
        grid=(n_tiles + n_fill,), in_specs=in_specs,
        out_specs=[pl.BlockSpec((tm, TOKEN_ROWS, LANES), lambda i: (i + row0 // tm, 0, 0)),
                   pl.BlockSpec((tm, LANES), lambda i: (i + row0 // tm, 0))],
        out_shape=[jax.ShapeDtypeStruct((n_rows, TOKEN_ROWS, LANES), F32),
                   jax.ShapeDtypeStruct((n_rows, LANES), F32)],
        input_output_aliases=aliases,
        compiler_params=_cparams(("arbitrary",)), name="outproj_hi" if hi else "outproj",
    )(*args)


def _to_lanes(col):
    n = col.shape[0]
    diag = lax.broadcasted_iota(jnp.int32, (n, n), 0) == lax.broadcasted_iota(jnp.int32, (n, n), 1)
    return jnp.sum(jnp.where(diag, col, 0.0), axis=0, keepdims=True)


def _to_sublanes(row):
    n = row.shape[1]
    diag = lax.broadcasted_iota(jnp.int32, (n, n), 0) == lax.broadcasted_iota(jnp.int32, (n, n), 1)
    return jnp.sum(jnp.where(diag, row, 0.0), axis=1, keepdims=True)


def _plan_kernel(route_ref, slot_ref, pos_ref, wa_ref, wb_ref, tiles_ref, run_ref, cls_s, rank_s,
                 *, n_steps, n_tokens):
    i = pl.program_id(0)
    rows = route_ref.shape[0]
    e = float(EXPERTS_PER_GROUP)

    @pl.when(i == 0)
    def _():
        run_ref[...] = jnp.zeros(run_ref.shape, F32)

    r = route_ref[...]
    i1, i2, w1, w2 = r[:, 0:1], r[:, 1:2], r[:, 2:3], r[:, 3:4]
    grp = jnp.floor(i1 * (1.0 / e))
    cls = grp * (e * e) + (jnp.minimum(i1, i2) - grp * e) * e + (jnp.maximum(i1, i2) - grp * e)
    token = i * rows + lax.broadcasted_iota(jnp.int32, (rows, 1), 0)
    lane = lax.broadcasted_iota(jnp.int32, (rows, N_CLASSES), 1).astype(F32)
    onehot = (lane == cls) & (token < n_tokens)
    earlier = (lax.broadcasted_iota(jnp.int32, (rows, rows), 0)
               > lax.broadcasted_iota(jnp.int32, (rows, rows), 1))
    before = jnp.dot(jnp.where(earlier, 1.0, 0.0).astype(BF16),
                     jnp.where(onehot, 1.0, 0.0).astype(BF16), preferred_element_type=F32)
    rank = jnp.sum(jnp.where(onehot, before + run_ref[...], 0.0), axis=-1, keepdims=True)
    run_ref[...] += jnp.sum(jnp.where(onehot, 1.0, 0.0), axis=0, keepdims=True)
    first_is_a = i1 < i2
    columns = ((cls_s, cls), (rank_s, rank),
               (wa_ref, jnp.where(first_is_a, w1, w2)), (wb_ref, jnp.where(first_is_a, w2, w1)))
    for c in range(rows // LANES):
        for ref, col in columns:
            ref[pl.ds(i * (rows // LANES) + c, 1), :] = _to_lanes(col[c * LANES:(c + 1) * LANES])

    @pl.when(i == n_steps - 1)
    def _():
        counts = run_ref[...]
        tiles = jnp.floor((counts + (MOE_TILE - 1.0)) * (1.0 / MOE_TILE))
        upto = jnp.where(lax.broadcasted_iota(jnp.int32, (N_CLASSES, N_CLASSES), 0)
                         <= lax.broadcasted_iota(jnp.int32, (N_CLASSES, N_CLASSES), 1), 1.0, 0.0)
        both = jnp.concatenate([tiles, counts] + [jnp.zeros_like(counts)] * 6, axis=0)
        ends = _mm(both, upto, True)
        tile_end, tok_end = ends[0:1], ends[1:2]
        tile_off, tok_off = tile_end - tiles, tok_end - counts
        n_tiles = jnp.max(tile_end, axis=-1, keepdims=True)
        t = lax.broadcasted_iota(jnp.int32, (N_CLASSES, 1), 0).astype(F32)
        t = jnp.minimum(t, n_tiles - 1.0)
        cls_t = jnp.sum(jnp.where(tile_end <= t, 1.0, 0.0), axis=-1, keepdims=True)
        mine = lax.broadcasted_iota(jnp.int32, (N_CLASSES, N_CLASSES), 1).astype(F32) == cls_t
        pick = lambda v: jnp.sum(jnp.where(mine, v, 0.0), axis=-1, keepdims=True)
        base = pick(tok_off) + (t - pick(tile_off)) * MOE_TILE
        lane_t = lax.broadcasted_iota(jnp.int32, (N_CLASSES, LANES), 1)
        tiles_ref[...] = sum(jnp.where(lane_t == j, v, 0.0)
                             for j, v in enumerate([cls_t, base, n_tiles + jnp.zeros_like(t)]))

        tile_off_col, tok_off_col = _to_sublanes(tile_off), _to_sublanes(tok_off)
        class_id = lax.broadcasted_iota(jnp.int32, (N_CLASSES, LANES), 0).astype(F32)

        def per_block(j, carry):
            at = pl.ds(j, 1)
            own = class_id == cls_s[at, :]
            look = lambda col: jnp.sum(jnp.where(own, col, 0.0), axis=0, keepdims=True)
            rank_row = rank_s[at, :]
            slot_ref[at, :] = (look(tile_off_col) * MOE_TILE + rank_row).astype(jnp.int32)
            pos_ref[at, :] = (look(tok_off_col) + rank_row).astype(jnp.int32)
            return carry

        lax.fori_loop(0, slot_ref.shape[0], per_block, 0)


def _plan_order_kernel(pos_ref, order_ref, *, n_tokens):
    def place(t, carry):
        order_ref[pos_ref[t]] = t
        return carry

    lax.fori_loop(0, n_tokens, place, 0, unroll=32)
    for j in range(MOE_TILE):
        order_ref[n_tokens + j] = 0


def _route_plan(route, n_tokens):
    n_rows = route.shape[0]
    rows = PLAN_ROWS
    n_steps = n_rows // rows
    dense = lambda dtype: jax.ShapeDtypeStruct((n_rows // LANES, LANES), dtype)
    whole = lambda r, c: pl.BlockSpec((r, c), lambda i: (0, 0))
    slot, pos, wa, wb, tiles = pl.pallas_call(
        functools.partial(_plan_kernel, n_steps=n_steps, n_tokens=n_tokens),
        grid=(n_steps,), in_specs=[pl.BlockSpec((rows, LANES), lambda i: (i, 0))],
        out_specs=[whole(n_rows // LANES, LANES)] * 4 + [whole(N_CLASSES, LANES)],
        out_shape=[dense(jnp.int32), dense(jnp.int32), dense(F32), dense(F32),
                   jax.ShapeDtypeStruct((N_CLASSES, LANES), F32)],
        scratch_shapes=[pltpu.VMEM((1, N_CLASSES), F32),
                        pltpu.VMEM((n_rows // LANES, LANES), F32),
                        pltpu.VMEM((n_rows // LANES, LANES), F32)],
        compiler_params=_cparams(("arbitrary",)), name="plan",
    )(route)
    order = pl.pallas_call(
        functools.partial(_plan_order_kernel, n_tokens=n_tokens),
        in_specs=[pl.BlockSpec(memory_space=pltpu.SMEM)],
        out_specs=pl.BlockSpec(memory_space=pltpu.SMEM),
        out_shape=jax.ShapeDtypeStruct((n_tokens + MOE_TILE,), jnp.int32), name="plan_order",
    )(pos.reshape(-1))
    tile_cls = tiles[:, 0].astype(jnp.int32)
    tile_base = tiles[:, 1].astype(jnp.int32)
    n_tiles = tiles[0:1, 2].astype(jnp.int32)
    return tile_cls, tile_base, n_tiles, order, wa.reshape(-1), wb.reshape(-1), slot.reshape(-1)


def _moe_kernel(tcls_ref, tbase_ref, nt_ref, order_ref, wa_ref, wb_ref,
                hx_hbm, gf_ref, wg_ref, wu_ref, wd_ref, y_ref,
                xbuf, wgu_s, wd_s, sem):
    t = pl.program_id(0)
    n_tiles = nt_ref[0]
    buf = t % 2
    pair = EXPERTS_PER_GROUP * EXPERTS_PER_GROUP

    def gather(tile, b):
        base = tbase_ref[tile]
        for j in range(MOE_TILE):
            pltpu.make_async_copy(hx_hbm.at[order_ref[base + j]], xbuf.at[b, j], sem.at[b]).start()

    @pl.when(t == 0)
    def _():
        gather(0, 0)

    @pl.when(t + 1 < n_tiles)
    def _():
        gather(t + 1, 1 - buf)

    @pl.when(t < n_tiles)
    def _():
        cls = tcls_ref[t]
        grp = cls // pair

        @pl.when((t == 0) | (grp != tcls_ref[jnp.maximum(t - 1, 0)] // pair))
        def _():
            for e in range(EXPERTS_PER_GROUP):
                wgu_s[e, :, 0:D_EXPERT] = wg_ref[0, e].astype(BF16)
                wgu_s[e, :, D_EXPERT:2 * D_EXPERT] = wu_ref[0, e].astype(BF16)
                wd_s[e] = wd_ref[0, e].astype(BF16)

        pltpu.make_async_copy(hx_hbm.at[pl.ds(0, MOE_TILE)], xbuf.at[buf], sem.at[buf]).wait()
        h = xbuf[buf].reshape(MOE_TILE, D_MODEL)
        xn = _rms(h, gf_ref[...]).astype(BF16)
        ea = (cls % pair) // EXPERTS_PER_GROUP
        eb = cls % EXPERTS_PER_GROUP

        def column(w_ref):
            base = tbase_ref[t]
            sub = lax.broadcasted_iota(jnp.int32, (SUBLANES, 1), 0)
            chunks = []
            for c in range(MOE_TILE // SUBLANES):
                acc = jnp.zeros((SUBLANES, 1), F32)
                for k in range(SUBLANES):
                    acc = jnp.where(sub == k, w_ref[order_ref[base + c * SUBLANES + k]], acc)
                chunks.append(acc)
            return jnp.concatenate(chunks, axis=0)

        gus = [jnp.dot(xn, wgu_s[e], preferred_element_type=F32) for e in (ea, eb)]
        acts = [(jax.nn.silu(gu[:, 0:D_EXPERT]) * gu[:, D_EXPERT:2 * D_EXPERT]).astype(BF16)
                for gu in gus]
        oa, ob = [jnp.dot(act, wd_s[e], preferred_element_type=F32)
                  for act, e in zip(acts, (ea, eb))]
        y = h + column(wa_ref) * oa + column(wb_ref) * ob
        y_ref[...] = y.reshape(MOE_TILE, D_MODEL // LANES, LANES)

    @pl.when(t >= n_tiles)
    def _():
        y_ref[...] = jnp.zeros(y_ref.shape, F32)


def _moe(hx, gf, w_gate, w_up, w_down, plan, n_slots):
    tile_cls, tile_base, n_tiles, order, wa, wb, _ = plan
    pair = EXPERTS_PER_GROUP * EXPERTS_PER_GROUP
    assert n_slots // MOE_TILE <= tile_cls.shape[0]
    grouped = lambda w: w.reshape(N_GROUPS, EXPERTS_PER_GROUP, *w.shape[1:])
    wspec = lambda r, c: pl.BlockSpec((1, EXPERTS_PER_GROUP, r, c),
                                      lambda t, tc, *_: (tc[t] // pair, 0, 0, 0),
                                      pipeline_mode=pl.Buffered(1))
    grid_spec = pltpu.PrefetchScalarGridSpec(
        num_scalar_prefetch=6, grid=(n_slots // MOE_TILE,),
        in_specs=[pl.BlockSpec(memory_space=pl.ANY),
                  pl.BlockSpec(gf.shape, lambda t, *_: (0, 0)),
                  wspec(D_MODEL, D_EXPERT), wspec(D_MODEL, D_EXPERT), wspec(D_EXPERT, D_MODEL)],
        out_specs=pl.BlockSpec((MOE_TILE, D_MODEL // LANES, LANES), lambda t, *_: (t, 0, 0)),
        scratch_shapes=[pltpu.VMEM((2, MOE_TILE, TOKEN_ROWS, LANES), F32),
                        pltpu.VMEM((EXPERTS_PER_GROUP, D_MODEL, 2 * D_EXPERT), BF16),
                        pltpu.VMEM((EXPERTS_PER_GROUP, D_EXPERT, D_MODEL), BF16),
                        pltpu.SemaphoreType.DMA((2,))])
    return pl.pallas_call(
        _moe_kernel, grid_spec=grid_spec,
        out_shape=jax.ShapeDtypeStruct((n_slots, D_MODEL // LANES, LANES), F32),
        compiler_params=_cparams(("arbitrary",)), name="moe",
    )(tile_cls, tile_base, n_tiles, order, wa, wb, hx, gf,
      grouped(w_gate), grouped(w_up), grouped(w_down))


def _unsort_kernel(slot_ref, ys_hbm, y_ref, buf, sem, *, rows, row0):
    i = pl.program_id(0)
    b = i % 2

    def gather(step, dst):
        for j in range(rows):
            pltpu.make_async_copy(ys_hbm.at[slot_ref[row0 + step * rows + j]], buf.at[dst, j],
                                  sem.at[dst]).start()

    @pl.when(i == 0)
    def _():
        gather(0, 0)

    @pl.when(i + 1 < pl.num_programs(0))
    def _():
        gather(i + 1, 1 - b)

    pltpu.make_async_copy(ys_hbm.at[pl.ds(0, rows)], buf.at[b], sem.at[b]).wait()
    y_ref[...] = buf[b].reshape(rows, D_MODEL)


def _unsort(ys, token_slot, *, n, row0, rows):
    grid_spec = pltpu.PrefetchScalarGridSpec(
        num_scalar_prefetch=1, grid=(n // rows,),
        in_specs=[pl.BlockSpec(memory_space=pl.ANY)],
        out_specs=pl.BlockSpec((rows, D_MODEL), lambda i, *_: (i, 0)),
        scratch_shapes=[pltpu.VMEM((2, rows, D_MODEL // LANES, LANES), F32),
                        pltpu.SemaphoreType.DMA((2,))])
    return pl.pallas_call(
        functools.partial(_unsort_kernel, rows=rows, row0=row0), grid_spec=grid_spec,
        out_shape=jax.ShapeDtypeStruct((n, D_MODEL), F32),
        compiler_params=_cparams(("arbitrary",)), name="unsort",
    )(token_slot, ys)


def kernel(x_prompt, x_sample, cache_k, cache_v, state_conv, g_norm_mix, w_in, g_q, g_k, w_conv,
           g_attn_out, g_conv_out, w_out, g_norm_ffn, w_router_group, w_router_expert,
           w_gate, w_up, w_down):
    depth = g_norm_mix.shape[0]
    assert depth == 1
    nb, seq, _ = x_prompt.shape
    db, dec_seq, _ = x_sample.shape
    assert dec_seq == 1
    l = 0

    gmix = g_norm_mix[l][None, :]
    gq = jnp.tile(g_q[l], N_HEADS)[None, :]
    gk = jnp.tile(g_k[l], N_HEADS)[None, :]
    ga = g_attn_out[l][None, :]
    gc = g_conv_out[l][None, :]
    gf = g_norm_ffn[l][None, :]
    w_router = jnp.concatenate(
        [w_router_expert[l], w_router_group[l],
         jnp.zeros((D_MODEL, LANES - N_EXPERTS - N_GROUPS), F32)], axis=1)
    w_router_hi = w_router.astype(BF16)
    w_router_split = jnp.concatenate(
        [w_router_hi, (w_router - w_router_hi.astype(F32)).astype(BF16)], axis=1)

    cos_p, sin_p = _rope_tables(np.arange(seq))
    xp = x_prompt.reshape(nb * seq, D_MODEL)
    q, k, v, bc, tail, kt, vt = _inproj(xp, gmix, w_in[l].astype(BF16), gq, gk, cos_p, sin_p,
                                        w_conv[l], tm=512, seq_len=seq, hi=False,
                                        q_scale=ATTN_SCALE * LOG2_E, n_parts=2)
    attn = _attention(q.reshape(nb, seq, D_ATTN), k.reshape(nb, seq, D_ATTN),
                      v.reshape(nb, seq, D_ATTN))
    n_prompt = nb * seq
    n_total = n_prompt + db
    n_rows = -(-n_total // TOKEN_PAD) * TOKEN_PAD
    hx, route = _outproj(attn.reshape(n_prompt, D_ATTN), bc, xp, ga, gc, w_out[l].astype(BF16),
                         gf, w_router_split, tm=512, hi=False, n_rows=n_rows, row0=0, n_parts=2)
    k_prompt = jnp.transpose(kt, (0, 3, 1, 2))
    v_prompt = jnp.transpose(vt, (0, 3, 1, 2))
    conv_prompt = tail.reshape(nb, SUBLANES, D_CONV)[:, SUBLANES - (CONV_WIDTH - 1):]

    cos_s, sin_s = _rope_tables(np.full((db,), PAST_LEN))
    xs = x_sample.reshape(db, D_MODEL)
    st = state_conv[l]
    qs, ks, vs, bcs, us = _inproj(xs, gmix, w_in[l], gq, gk, cos_s, sin_s, w_conv[l],
                                  tm=db, seq_len=db, hi=True, q_scale=ATTN_SCALE,
                                  prev=(st[:, 1], st[:, 0]))
    heads = lambda t: t.reshape(db, N_HEADS, HEAD_DIM)
    attn_s = _attention_sample(heads(qs), heads(ks), heads(vs),
                               jnp.transpose(cache_k[l], (0, 2, 3, 1)),
                               jnp.transpose(cache_v[l], (0, 2, 3, 1)))
    hx, route = _outproj(attn_s.reshape(db, D_ATTN), bcs, xs, ga, gc, w_out[l], gf, w_router,
                         tm=db, hi=True, n_rows=n_rows, row0=n_prompt, into=(hx, route))

    n_cls = N_GROUPS * EXPERTS_PER_GROUP * (EXPERTS_PER_GROUP - 1) // 2
    n_slots = -(-(n_total + n_cls * (MOE_TILE - 1)) // MOE_TILE) * MOE_TILE
    assert n_slots // MOE_TILE <= N_CLASSES and n_rows % PLAN_ROWS == 0
    plan = _route_plan(route, n_total)
    ys = _moe(hx, gf, w_gate[l], w_up[l], w_down[l], plan, n_slots)
    y_prompt = _unsort(ys, plan[-1], n=n_prompt, row0=0, rows=512).reshape(nb, seq, D_MODEL)
    y_sample = _unsort(ys, plan[-1], n=db, row0=n_prompt, rows=db).reshape(db, 1, D_MODEL)
    k_sample = ks.reshape(db, 1, N_HEADS, HEAD_DIM)
    v_sample = vs.reshape(db, 1, N_HEADS, HEAD_DIM)
    conv_sample = jnp.stack([st[:, 1], us], axis=1)

    return (y_prompt, y_sample, k_prompt[None], v_prompt[None], conv_prompt[None],
            k_sample[None], v_sample[None], conv_sample[None])
```

```python
import functools

import numpy as np
import jax
import jax.numpy as jnp
from jax import lax
from jax.experimental import pallas as pl
from jax.experimental.pallas import tpu as pltpu

F32 = jnp.float32
BF16 = jnp.bfloat16

D_MODEL = 1024
HEAD_DIM = 64
N_HEADS = 12
D_ATTN = N_HEADS * HEAD_DIM
D_CONV = D_MODEL - D_ATTN
D_IN_PROJ = 3 * D_ATTN + 3 * D_CONV
CONV_WIDTH = 3
DILATIONS = (1, 4, 16)
ATT_BLOCK = 128
ATT_UNROLL = 16
REGROUP_STEP = 4
MAX_WINDOW = 2048
PAST_LEN = 8192
ATTN_SCALE = HEAD_DIM ** -0.5
LOG2_E = float(np.log2(np.e))
ROPE_THETA = 10000.0
EPS = 1e-6
N_GROUPS = 4
EXPERTS_PER_GROUP = 8
N_EXPERTS = N_GROUPS * EXPERTS_PER_GROUP
D_EXPERT = 256
MOE_TILE = 192
N_CLASSES = N_GROUPS * EXPERTS_PER_GROUP * EXPERTS_PER_GROUP
PLAN_ROWS = 768
TOKEN_PAD = 512

LANES = 128
SUBLANES = 8
MXU_DIM = 256
TOKEN_ROWS = D_MODEL // LANES
NEG = -1e30
VMEM_LIMIT = 48 * 1024 * 1024


def _cparams(sem):
    return pltpu.CompilerParams(dimension_semantics=sem, vmem_limit_bytes=VMEM_LIMIT)


def _mm(a, b, hi):
    if hi:
        return jnp.dot(a, b, preferred_element_type=F32, precision=lax.Precision.HIGHEST)
    return jnp.dot(a.astype(BF16), b.astype(BF16), preferred_element_type=F32)


def _rms(x, g):
    return x * lax.rsqrt(jnp.mean(x * x, axis=-1, keepdims=True) + EPS) * g


def _rope_tables(pos):
    half = HEAD_DIM // 2
    inv = ROPE_THETA ** (-np.arange(half, dtype=np.float64) / half)
    ang = np.asarray(pos, np.float64)[:, None] * inv[None, :]
    cos, sin = np.cos(ang), np.sin(ang)
    cos2 = np.concatenate([cos, cos, cos, cos], axis=-1)
    sin2 = np.concatenate([-sin, sin, -sin, sin], axis=-1)
    return jnp.asarray(cos2, F32), jnp.asarray(sin2, F32)


def _inproj_kernel(*refs, tm, tiles_per_seq, keep_tiles, hi, given_prev, n_parts, q_scale):
    if given_prev:
        (x_ref, gmix_ref, w_ref, gq_ref, gk_ref, cos_ref, sin_ref, wconv_ref, p1_ref, p2_ref,
         q_ref, k_ref, v_ref, bc_ref, u_ref) = refs
    else:
        (x_ref, gmix_ref, w_ref, gq_ref, gk_ref, cos_ref, sin_ref, wconv_ref,
         q_ref, k_ref, v_ref, bc_ref, tail_ref, kt_ref, vt_ref, ubuf) = refs
    part_rows = tm // n_parts

    r_i = lax.broadcasted_iota(jnp.int32, (MXU_DIM, MXU_DIM), 0) // HEAD_DIM
    c_i = lax.broadcasted_iota(jnp.int32, (MXU_DIM, MXU_DIM), 1) // HEAD_DIM
    head_sum = jnp.where(r_i == c_i, 1.0, 0.0).astype(F32 if hi else BF16)
    lane = lax.broadcasted_iota(jnp.int32, (1, D_ATTN), 1)
    first_half = (lane % HEAD_DIM) < (HEAD_DIM // 2)
    wc = wconv_ref[...]

    def head_ssq(z):
        sq = z * z
        return jnp.concatenate(
            [_mm(sq[:, c * MXU_DIM:(c + 1) * MXU_DIM], head_sum, hi)
             for c in range(D_ATTN // MXU_DIM)], axis=-1)

    def norm_rope(z, ssq, g_ref, cos, sin):
        zn = z * lax.rsqrt(ssq * (1.0 / HEAD_DIM) + EPS) * g_ref[...]
        partner = jnp.where(first_half,
                            pltpu.roll(zn, D_ATTN - HEAD_DIM // 2, 1),
                            pltpu.roll(zn, HEAD_DIM // 2, 1))
        return zn * cos + partner * sin

    def conv_gate(b_gate, u2, u1, u):
        return b_gate * (wc[0:1, :] * u2 + wc[1:2, :] * u1 + wc[2:3, :] * u)

    def part(p, transposed):
        rows = pl.ds(p * part_rows, part_rows)
        xn = _rms(x_ref[rows, :], gmix_ref[...])
        if not hi:
            xn = xn.astype(BF16)
        zq = _mm(xn, w_ref[:, 0:D_ATTN], hi)
        zk = _mm(xn, w_ref[:, D_ATTN:2 * D_ATTN], hi)
        ssq_q = head_ssq(zq)
        v = _mm(xn, w_ref[:, 2 * D_ATTN:3 * D_ATTN], hi)
        ssq_k = head_ssq(zk)
        c0 = 3 * D_ATTN
        b_gate = _mm(xn, w_ref[:, c0:c0 + D_CONV], hi)
        c_gate = _mm(xn, w_ref[:, c0 + D_CONV:c0 + 2 * D_CONV], hi)
        u_in = _mm(xn, w_ref[:, c0 + 2 * D_CONV:c0 + 3 * D_CONV], hi)
        yield
        reps = D_ATTN // LANES
        cos = jnp.concatenate([cos_ref[rows, :]] * reps, axis=-1)
        sin = jnp.concatenate([sin_ref[rows, :]] * reps, axis=-1)
        q_ref[rows, :] = norm_rope(zq, ssq_q, gq_ref, cos, sin) * q_scale
        k = norm_rope(zk, ssq_k, gk_ref, cos, sin)
        k_ref[rows, :] = k
        v_ref[rows, :] = v
        if transposed:
            cols = slice(p * part_rows, (p + 1) * part_rows)
            kt_ref[0, :, :, cols] = k.T.reshape(N_HEADS, HEAD_DIM, part_rows)
            vt_ref[0, :, :, cols] = v.T.reshape(N_HEADS, HEAD_DIM, part_rows)
        u = c_gate * u_in
        if given_prev:
            u_ref[rows, :] = u
            bc_ref[rows, :] = conv_gate(b_gate, p2_ref[rows, :], p1_ref[rows, :], u)
        else:
            ubuf[pl.ds(SUBLANES + p * part_rows, part_rows), :] = u
        return b_gate

    def body(transposed):
        stages = [part(p, transposed) for p in range(n_parts)]
        for stage in stages:
            next(stage)
        gates = []
        for stage in stages:
            try:
                next(stage)
            except StopIteration as done:
                gates.append(done.value)
        if not given_prev:
            u = ubuf[SUBLANES:tm + SUBLANES, :]
            u1 = ubuf[SUBLANES - 1:tm + SUBLANES - 1, :]
            u2 = ubuf[SUBLANES - 2:tm + SUBLANES - 2, :]
            bc_ref[...] = conv_gate(jnp.concatenate(gates, axis=0), u2, u1, u)
            tail_ref[...] = ubuf[tm:tm + SUBLANES, :]

    if given_prev:
        body(False)
        return

    i = pl.program_id(0)

    @pl.when(i % tiles_per_seq == 0)
    def _():
        ubuf[0:SUBLANES, :] = jnp.zeros((SUBLANES, D_CONV), F32)

    @pl.when(i % tiles_per_seq != 0)
    def _():
        ubuf[0:SUBLANES, :] = ubuf[tm:tm + SUBLANES, :]

    kept = i % tiles_per_seq >= tiles_per_seq - keep_tiles
    pl.when(kept)(functools.partial(body, True))
    pl.when(jnp.logical_not(kept))(functools.partial(body, False))


def _inproj(x, gmix, w_in, gq, gk, cos, sin, w_conv, *, tm, seq_len, hi, q_scale, n_parts=1,
            prev=None):
    n = x.shape[0]
    nt = n // tm
    tiles_per_seq = seq_len // tm
    row = lambda w: pl.BlockSpec((tm, w), lambda i: (i, 0))
    full = lambda a: pl.BlockSpec(a.shape, lambda i: (0,) * a.ndim)
    tab = pl.BlockSpec((tm, LANES), lambda i: (i % tiles_per_seq, 0))
    once = pl.BlockSpec(w_in.shape, lambda i: (0, 0), pipeline_mode=pl.Buffered(1))
    in_specs = [row(D_MODEL), full(gmix), once, full(gq), full(gk), tab, tab, full(w_conv)]
    args = [x, gmix, w_in, gq, gk, cos, sin, w_conv]
    out_shape = [jax.ShapeDtypeStruct((n, D_ATTN), F32)] * 3 + [jax.ShapeDtypeStruct((n, D_CONV), F32)]
    out_specs = [row(D_ATTN)] * 3 + [row(D_CONV)]
    scratch = []
    keep_tiles = min(MAX_WINDOW, seq_len) // tm
    if prev is not None:
        in_specs += [row(D_CONV), row(D_CONV)]
        args += list(prev)
        out_shape.append(jax.ShapeDtypeStruct((n, D_CONV), F32))
        out_specs.append(row(D_CONV))
    else:
        n_seq = n // seq_len
        out_shape.append(jax.ShapeDtypeStruct((n_seq * SUBLANES, D_CONV), F32))
        out_specs.append(pl.BlockSpec((SUBLANES, D_CONV), lambda i: (i // tiles_per_seq, 0)))
        kept = pl.BlockSpec(
            (1, N_HEADS, HEAD_DIM, tm),
            lambda i: (i // tiles_per_seq, 0, 0,
                       jnp.maximum(i % tiles_per_seq - (tiles_per_seq - keep_tiles), 0)))
        out_shape += [jax.ShapeDtypeStruct((n_seq, N_HEADS, HEAD_DIM, keep_tiles * tm), F32)] * 2
        out_specs += [kept, kept]
        scratch.append(pltpu.VMEM((tm + SUBLANES, D_CONV), F32))
    return pl.pallas_call(
        functools.partial(_inproj_kernel, tm=tm, tiles_per_seq=tiles_per_seq,
                          keep_tiles=keep_tiles, hi=hi, given_prev=prev is not None,
                          n_parts=n_parts, q_scale=q_scale),
        grid=(nt,), in_specs=in_specs, out_specs=out_specs, out_shape=out_shape,
        scratch_shapes=scratch, compiler_params=_cparams(("arbitrary",)),
        name="inproj_hi" if hi else "inproj",
    )(*args)


def _attn_kernel(q_ref, k_ref, v_ref, o_ref, num_ref, den_ref, m_ref, bias_ref,
                 qd_ref, kd_ref, vd_ref, numd_ref, dend_ref, md_ref, tmp_ref, *, seq_len):
    blk = ATT_BLOCK
    qi = lax.broadcasted_iota(jnp.int32, (blk, 2 * blk), 0)
    kj = lax.broadcasted_iota(jnp.int32, (blk, 2 * blk), 1)
    bias_ref[...] = jnp.where((kj >= qi) & (kj <= qi + blk), 0.0, NEG)
    head0 = lax.broadcasted_iota(jnp.int32, (1, LANES), 1) < HEAD_DIM
    dn = (((1,), (1,)), ((), ()))

    def rows(start, size, d):
        return pl.ds(start, size) if d == 1 else pl.ds(start, size, stride=d)

    natural = (lambda at: q_ref[0, at, :], lambda at: k_ref[0, at, :], lambda at: v_ref[0, at, :],
               (num_ref, den_ref, m_ref))
    regrouped = (lambda at: qd_ref[at, :], lambda at: kd_ref[at, :], lambda at: vd_ref[at, :],
                 (numd_ref, dend_ref, md_ref))

    def group(src, d, r, n0, u, first, mode):
        q_at, k_at, v_at, (num_acc, den_acc, m_acc) = src
        q_rows = rows(n0 * (blk * d) + r, u * blk, d)
        if first:
            k_rows = rows(r, u * blk, d)
        else:
            k_rows = rows((n0 - 1) * (blk * d) + r, (u + 1) * blk, d)
        qa = q_at(q_rows)
        ka = k_at(k_rows).astype(BF16)
        va = v_at(k_rows)
        qa0 = jnp.where(head0, qa, 0.0).astype(BF16)
        qa1 = jnp.where(head0, 0.0, qa).astype(BF16)
        va0 = jnp.where(head0, va, 1.0).astype(BF16)
        va1 = jnp.where(head0, 1.0, va).astype(BF16)
        if mode != "init":
            m_old = m_acc[q_rows, :]
            num_old = num_acc[q_rows, :]
            den_old = den_acc[q_rows, :]
        key_slices, scores = [], []
        for i in range(u):
            qs = slice(i * blk, (i + 1) * blk)
            if first and i == 0:
                ks = slice(0, blk)
                bias = bias_ref[:, blk:2 * blk]
            else:
                k0 = (i - 1) * blk if first else i * blk
                ks = slice(k0, k0 + 2 * blk)
                bias = bias_ref[...]
            key_slices.append(ks)
            scores.append(
                (lax.dot_general(qa0[qs], ka[ks], dn, preferred_element_type=F32) + bias,
                 lax.dot_general(qa1[qs], ka[ks], dn, preferred_element_type=F32) + bias))
        yield
        probs, ms = [], []
        for s0, s1 in scores:
            m0 = jnp.max(s0, axis=-1, keepdims=True)
            m1 = jnp.max(s1, axis=-1, keepdims=True)
            probs.append((jnp.exp2(s0 - m0).astype(BF16), jnp.exp2(s1 - m1).astype(BF16)))
            ms.append(jnp.where(head0, m0, m1))
        yield
        nums, dens = [], []
        for (p0, p1), ks in zip(probs, key_slices):
            r0 = jnp.dot(p0, va0[ks], preferred_element_type=F32)
            r1 = jnp.dot(p1, va1[ks], preferred_element_type=F32)
            nums.append(jnp.where(head0, r0, r1))
            dens.append(pltpu.roll(jnp.where(head0, r1, r0), HEAD_DIM, 1))
        num = jnp.concatenate(nums, axis=0)
        den = jnp.concatenate(dens, axis=0)
        m = jnp.concatenate(ms, axis=0)
        if mode == "init":
            return [(num_acc, q_rows, num), (den_acc, q_rows, den), (m_acc, q_rows, m)]
        m_new = jnp.maximum(m_old, m)
        a = jnp.exp2(m_old - m_new)
        b = jnp.exp2(m - m_new)
        num = num_old * a + num * b
        den = den_old * a + den * b
        if mode == "merge":
            return [(num_acc, q_rows, num), (den_acc, q_rows, den), (m_acc, q_rows, m_new)]
        return [(o_ref.at[0], q_rows, num / den)]

    def run(groups):
        stages = [group(*g) for g in groups]
        for _ in range(2):
            for stage in stages:
                next(stage)
        stores = []
        for stage in stages:
            try:
                next(stage)
            except StopIteration as done:
                stores += done.value
        for ref, at, val in stores:
            ref[at, :] = val

    def branch(src, d, n_seq, sub_len, seq_stride, mode):
        nb = sub_len // blk
        u = min(ATT_UNROLL, nb)
        if nb == u:
            per_body = ATT_UNROLL // u

            def several(rr, carry):
                run([(src, d, (rr * per_body + t) * seq_stride, 0, u, True, mode)
                     for t in range(per_body)])
                return carry

            lax.fori_loop(0, n_seq // per_body, several, 0)
        else:
            def one(r, carry):
                run([(src, d, r * seq_stride, 0, u, True, mode)])

                def later_blocks(g, c):
                    run([(src, d, r * seq_stride, g * u, u, False, mode)])
                    return c

                lax.fori_loop(1, nb // u, later_blocks, 0)
                return carry

            lax.fori_loop(0, n_seq, one, 0)

    step = REGROUP_STEP
    wide = step * step
    assert DILATIONS == (1, step, wide)
    part = seq_len // wide

    def regroup(load, dst_ref):
        for a in range(step):
            tmp_ref[a * step * part:(a + 1) * step * part, :] = load(rows(a, step * part, step))
        for a in range(step):
            for b in range(step):
                s = a * step + b
                dst_ref[s * part:(s + 1) * part, :] = tmp_ref[rows(a * step * part + b, part, step), :]

    def ungroup(src_ref, dst_ref):
        for a in range(step):
            for b in range(step):
                s = a * step + b
                tmp_ref[rows(a * step * part + b, part, step), :] = src_ref[s * part:(s + 1) * part, :]
        for a in range(step):
            dst_ref[rows(a, step * part, step), :] = tmp_ref[a * step * part:(a + 1) * step * part, :]

    for load, dst in zip(natural[:3], (qd_ref, kd_ref, vd_ref)):
        regroup(load, dst)
    branch(regrouped, 1, wide, part, part, "init")
    for src, dst in zip(regrouped[3], natural[3]):
        ungroup(src, dst)
    branch(natural, 1, 1, seq_len, 0, "merge")
    branch(natural, step, step, seq_len // step, 1, "final")


def _attention(q, k, v):
    b, s, _ = q.shape
    spec = pl.BlockSpec((1, s, LANES), lambda i, p: (i, 0, p))
    return pl.pallas_call(
        functools.partial(_attn_kernel, seq_len=s),
        grid=(b, D_ATTN // LANES), in_specs=[spec] * 3, out_specs=spec,
        out_shape=jax.ShapeDtypeStruct((b, s, D_ATTN), F32),
        scratch_shapes=[pltpu.VMEM((s, LANES), F32)] * 3 + [pltpu.VMEM((ATT_BLOCK, 2 * ATT_BLOCK), F32)]
        + [pltpu.VMEM((s, LANES), F32)] * 7,
        compiler_params=_cparams(("arbitrary", "arbitrary")), name="attn_prompt",
    )(q, k, v)


def _attn_sample_kernel(q_ref, kn_ref, vn_ref, kc_ref, vc_ref, o_ref, *, w_buf, heads):
    age = w_buf - lax.broadcasted_iota(jnp.int32, (1, w_buf), 1)
    valid = [(age % d == 0) & (age <= d * ATT_BLOCK) for d in DILATIONS]
    diag = (lax.broadcasted_iota(jnp.int32, (HEAD_DIM, HEAD_DIM), 0)
            == lax.broadcasted_iota(jnp.int32, (HEAD_DIM, HEAD_DIM), 1))
    to_col = lambda row: jnp.sum(jnp.where(diag, row, 0.0), axis=1, keepdims=True)
    to_row = lambda col: jnp.sum(jnp.where(diag, col, 0.0), axis=0, keepdims=True)
    q_rows, kn_rows, vn_rows = q_ref[0], kn_ref[0], vn_ref[0]
    s_self_all = jnp.sum(kn_rows * q_rows, axis=1, keepdims=True)
    for h in range(heads):
        q = to_col(q_rows[h:h + 1])
        vn = to_col(vn_rows[h:h + 1])
        s = jnp.sum(kc_ref[0, h] * q, axis=0, keepdims=True)
        s_self = s_self_all[h:h + 1]
        parts = []
        for ok in valid:
            sd = jnp.where(ok, s, NEG)
            m = jnp.maximum(jnp.max(sd, axis=1, keepdims=True), s_self)
            p = jnp.exp(sd - m)
            p_self = jnp.exp(s_self - m)
            parts.append((p, p_self, jnp.sum(p, axis=1, keepdims=True) + p_self, m))
        m_all = jnp.maximum(jnp.maximum(parts[0][3], parts[1][3]), parts[2][3])
        w = jnp.zeros((1, w_buf), F32)
        w_self = jnp.zeros((1, 1), F32)
        den_t = jnp.zeros((1, 1), F32)
        for p, p_self, den, m in parts:
            c = jnp.exp(m - m_all)
            w = w + p * c
            w_self = w_self + p_self * c
            den_t = den_t + den * c
        num_t = jnp.sum(vc_ref[0, h] * w, axis=1, keepdims=True) + vn * w_self
        o_ref[0, h:h + 1, :] = to_row(num_t / den_t)


def _attention_sample(q, kn, vn, cache_kt, cache_vt):
    db, heads, _, w_buf = cache_kt.shape
    tok = pl.BlockSpec((1, heads, HEAD_DIM), lambda i: (i, 0, 0))
    cache = pl.BlockSpec((1, heads, HEAD_DIM, w_buf), lambda i: (i, 0, 0, 0))
    return pl.pallas_call(
        functools.partial(_attn_sample_kernel, w_buf=w_buf, heads=heads),
        grid=(db,), in_specs=[tok, tok, tok, cache, cache], out_specs=tok,
        out_shape=jax.ShapeDtypeStruct((db, heads, HEAD_DIM), F32),
        compiler_params=_cparams(("arbitrary",)), name="attn_sample",
    )(q, kn, vn, cache_kt, cache_vt)


def _outproj_kernel(attn_ref, bc_ref, x_ref, ga_ref, gc_ref, wo_ref, gf_ref, wr_ref, *rest,
                    hi, n_tiles, n_parts):
    hx_ref, route_ref = rest[-2:]
    part_rows = x_ref.shape[0] // n_parts

    def part(k):
        rows = pl.ds(k * part_rows, part_rows)
        a = _rms(attn_ref[rows, :], ga_ref[...])
        c = _rms(bc_ref[rows, :], gc_ref[...])
        mix = _mm(a, wo_ref[0:D_ATTN, :], hi) + _mm(c, wo_ref[D_ATTN:D_MODEL, :], hi)
        yield
        h = x_ref[rows, :] + mix
        hn = _rms(h, gf_ref[...])
        if hi:
            lg = _mm(hn, wr_ref[...], True)
        else:
            hn_hi = hn.astype(BF16)
            hn_lo = (hn - hn_hi.astype(F32)).astype(BF16)
            both = jnp.dot(hn_hi, wr_ref[...], preferred_element_type=F32)
            lg = (both[:, :LANES] + both[:, LANES:]
                  + jnp.dot(hn_lo, wr_ref[:, :LANES], preferred_element_type=F32))
        yield
        lane_i = lax.broadcasted_iota(jnp.int32, lg.shape, 1)
        lane = lane_i.astype(F32)
        lane_group = (lane_i // EXPERTS_PER_GROUP).astype(F32)
        is_e = lane_i < N_EXPERTS
        is_g = (lane_i >= N_EXPERTS) & (lane_i < N_EXPERTS + N_GROUPS)
        big = float(LANES)
        first_at = lambda t, v: jnp.min(jnp.where(t == v, lane, big), axis=-1, keepdims=True)

        gl = jnp.where(is_g, lg, NEG)
        mg = jnp.max(gl, axis=-1, keepdims=True)
        p_g = 1.0 / jnp.sum(jnp.exp(gl - mg), axis=-1, keepdims=True)
        g_idx = first_at(gl, mg) - float(N_EXPERTS)
        el = jnp.where(is_e & (lane_group == g_idx), lg, NEG)
        v1 = jnp.max(el, axis=-1, keepdims=True)
        i1 = first_at(el, v1)
        el2 = jnp.where(lane == i1, NEG, el)
        v2 = jnp.max(el2, axis=-1, keepdims=True)
        i2 = first_at(el2, v2)
        e2 = jnp.exp(v2 - v1)
        w1 = p_g / (1.0 + e2)
        w2 = p_g * e2 / (1.0 + e2)
        route_ref[rows, :] = sum(jnp.where(lane_i == j, val, 0.0)
                                 for j, val in enumerate([i1, i2, w1, w2]))
        hx_ref[rows, :, :] = h.reshape(part_rows, TOKEN_ROWS, LANES)

    stages = [part(k) for k in range(n_parts)]
    for _ in range(2):
        for stage in stages:
            next(stage)
    for stage in stages:
        for _ in stage:
            pass

    @pl.when(pl.program_id(0) >= n_tiles)
    def _():
        hx_ref[...] = jnp.zeros(hx_ref.shape, F32)
        route_ref[...] = jnp.zeros(route_ref.shape, F32)


def _outproj(attn, bc, x, ga, gc, w_out, gf, w_router, *, tm, hi, n_rows, row0, n_parts=1,
             into=None):
    n = x.shape[0]
    n_tiles = n // tm
    n_fill = 0 if into is not None else -(-(n_rows - n) // tm)
    row = lambda w: pl.BlockSpec((tm, w), lambda i: (jnp.minimum(i, n_tiles - 1), 0))
    full = lambda a: pl.BlockSpec(a.shape, lambda i: (0,) * a.ndim)
    in_specs = [row(D_ATTN), row(D_CONV), row(D_MODEL), full(ga), full(gc), full(w_out),
                full(gf), full(w_router)]
    args = [attn, bc, x, ga, gc, w_out, gf, w_router]
    aliases = {}
    if into is not None:
        in_specs += [pl.BlockSpec(memory_space=pl.ANY)] * 2
        args += list(into)
        aliases = {len(args) - 2: 0, len(args) - 1: 1}
    return pl.pallas_call(
        functools.partial(_outproj_kernel, hi=hi, n_tiles=n_tiles, n_parts=n_parts),
        grid=(n_tiles + n_fill,), in_specs=in_specs,
        out_specs=[pl.BlockSpec((tm, TOKEN_ROWS, LANES), lambda i: (i + row0 // tm, 0, 0)),
                   pl.BlockSpec((tm, LANES), lambda i: (i + row0 // tm, 0))],
        out_shape=[jax.ShapeDtypeStruct((n_rows, TOKEN_ROWS, LANES), F32),
                   jax.ShapeDtypeStruct((n_rows, LANES), F32)],
        input_output_aliases=aliases,
        compiler_params=_cparams(("arbitrary",)), name="outproj_hi" if hi else "outproj",
    )(*args)


def _plan_kernel(route_ref, slot_ref, pos_ref, wa_ref, wb_ref, tiles_ref,
                 run_ref, cls_s, rank_s, later_s, *, n_steps, n_tokens):
    i = pl.program_id(0)
    rows = route_ref.shape[0]
    e = float(EXPERTS_PER_GROUP)

    @pl.when(i == 0)
    def _():
        run_ref[...] = jnp.zeros(run_ref.shape, F32)
        later_s[...] = jnp.where(lax.broadcasted_iota(jnp.int32, (rows, rows), 0)
                                 < lax.broadcasted_iota(jnp.int32, (rows, rows), 1),
                                 1.0, 0.0).astype(BF16)

    r = route_ref[...].T
    i1, i2, w1, w2 = r[0:1], r[1:2], r[2:3], r[3:4]
    grp = jnp.floor(i1 * (1.0 / e))
    cls = grp * (e * e) + (jnp.minimum(i1, i2) - grp * e) * e + (jnp.maximum(i1, i2) - grp * e)
    token = i * rows + lax.broadcasted_iota(jnp.int32, (1, rows), 1)
    class_id = lax.broadcasted_iota(jnp.int32, (N_CLASSES, rows), 0).astype(F32)
    own = (class_id == cls) & (token < n_tokens)
    before = jnp.dot(jnp.where(own, 1.0, 0.0).astype(BF16), later_s[...],
                     preferred_element_type=F32)
    rank = jnp.sum(jnp.where(own, before + run_ref[...], 0.0), axis=0, keepdims=True)
    run_ref[...] += jnp.sum(jnp.where(own, 1.0, 0.0), axis=1, keepdims=True)
    first_is_a = i1 < i2
    lines = ((cls_s, cls), (rank_s, rank),
             (wa_ref, jnp.where(first_is_a, w1, w2)), (wb_ref, jnp.where(first_is_a, w2, w1)))
    for c in range(rows // LANES):
        for ref, line in lines:
            ref[pl.ds(i * (rows // LANES) + c, 1), :] = line[:, c * LANES:(c + 1) * LANES]

    @pl.when(i == n_steps - 1)
    def _():
        counts = run_ref[...]
        tiles = jnp.floor((counts + (MOE_TILE - 1.0)) * (1.0 / MOE_TILE))
        upto = jnp.where(lax.broadcasted_iota(jnp.int32, (N_CLASSES, N_CLASSES), 0)
                         >= lax.broadcasted_iota(jnp.int32, (N_CLASSES, N_CLASSES), 1), 1.0, 0.0)
        lane = lax.broadcasted_iota(jnp.int32, (N_CLASSES, LANES), 1)
        both = jnp.where(lane == 0, tiles, 0.0) + jnp.where(lane == 1, counts, 0.0)
        ends = _mm(upto, both, True)
        tile_end, tok_end = ends[:, 0:1], ends[:, 1:2]
        tile_off, tok_off = tile_end - tiles, tok_end - counts
        n_tiles = jnp.max(tile_end, axis=0, keepdims=True)
        t = lax.broadcasted_iota(jnp.int32, (1, N_CLASSES), 1).astype(F32)
        t = jnp.minimum(t, n_tiles - 1.0)
        cls_t = jnp.sum(jnp.where(tile_end <= t, 1.0, 0.0), axis=0, keepdims=True)
        mine = lax.broadcasted_iota(jnp.int32, (N_CLASSES, N_CLASSES), 0).astype(F32) == cls_t
        pick = lambda v: jnp.sum(jnp.where(mine, v, 0.0), axis=0, keepdims=True)
        base = pick(tok_off) + (t - pick(tile_off)) * MOE_TILE
        row8 = lax.broadcasted_iota(jnp.int32, (SUBLANES, N_CLASSES), 0)
        tiles_ref[...] = sum(jnp.where(row8 == j, v, 0.0)
                             for j, v in enumerate([cls_t, base, n_tiles + jnp.zeros_like(t)]))

        token_class = lax.broadcasted_iota(jnp.int32, (N_CLASSES, LANES), 0).astype(F32)

        def per_block(j, carry):
            at = pl.ds(j, 1)
            hit = token_class == cls_s[at, :]
            look = lambda col: jnp.sum(jnp.where(hit, col, 0.0), axis=0, keepdims=True)
            rank_row = rank_s[at, :]
            slot_ref[at, :] = (look(tile_off) * MOE_TILE + rank_row).astype(jnp.int32)
            pos_ref[at, :] = (look(tok_off) + rank_row).astype(jnp.int32)
            return carry

        lax.fori_loop(0, slot_ref.shape[0], per_block, 0)


def _plan_order_kernel(pos_ref, order_ref, *, n_tokens):
    def place(t, carry):
        order_ref[pos_ref[t]] = t
        return carry

    lax.fori_loop(0, n_tokens, place, 0, unroll=32)
    for j in range(MOE_TILE):
        order_ref[n_tokens + j] = 0


def _route_plan(route, n_tokens):
    n_rows = route.shape[0]
    rows = PLAN_ROWS
    n_steps = n_rows // rows
    dense = lambda dtype: jax.ShapeDtypeStruct((n_rows // LANES, LANES), dtype)
    whole = lambda r, c: pl.BlockSpec((r, c), lambda i: (0, 0))
    slot, pos, wa, wb, tiles = pl.pallas_call(
        functools.partial(_plan_kernel, n_steps=n_steps, n_tokens=n_tokens),
        grid=(n_steps,), in_specs=[pl.BlockSpec((rows, LANES), lambda i: (i, 0))],
        out_specs=[whole(n_rows // LANES, LANES)] * 4 + [whole(SUBLANES, N_CLASSES)],
        out_shape=[dense(jnp.int32), dense(jnp.int32), dense(F32), dense(F32),
                   jax.ShapeDtypeStruct((SUBLANES, N_CLASSES), F32)],
        scratch_shapes=[pltpu.VMEM((N_CLASSES, 1), F32),
                        pltpu.VMEM((n_rows // LANES, LANES), F32),
                        pltpu.VMEM((n_rows // LANES, LANES), F32),
                        pltpu.VMEM((rows, rows), BF16)],
        compiler_params=_cparams(("arbitrary",)), name="plan",
    )(route)
    order = pl.pallas_call(
        functools.partial(_plan_order_kernel, n_tokens=n_tokens),
        in_specs=[pl.BlockSpec(memory_space=pltpu.SMEM)],
        out_specs=pl.BlockSpec(memory_space=pltpu.SMEM),
        out_shape=jax.ShapeDtypeStruct((n_tokens + MOE_TILE,), jnp.int32), name="plan_order",
    )(pos.reshape(-1))
    tile_cls = tiles[0].astype(jnp.int32)
    tile_base = tiles[1].astype(jnp.int32)
    n_tiles = tiles[2, 0:1].astype(jnp.int32)
    return tile_cls, tile_base, n_tiles, order, wa.reshape(-1), wb.reshape(-1), slot.reshape(-1)


def _moe_kernel(tcls_ref, tbase_ref, nt_ref, order_ref, wa_ref, wb_ref,
                hx_hbm, gf_ref, wg_ref, wu_ref, wd_ref, y_ref,
                xbuf, wgu_s, wd_s, sem):
    t = pl.program_id(0)
    n_tiles = nt_ref[0]
    buf = t % 2
    pair = EXPERTS_PER_GROUP * EXPERTS_PER_GROUP

    def gather(tile, b):
        base = tbase_ref[tile]
        for j in range(MOE_TILE):
            pltpu.make_async_copy(hx_hbm.at[order_ref[base + j]], xbuf.at[b, j], sem.at[b]).start()

    @pl.when(t == 0)
    def _():
        gather(0, 0)

    @pl.when(t + 1 < n_tiles)
    def _():
        gather(t + 1, 1 - buf)

    @pl.when(t < n_tiles)
    def _():
        cls = tcls_ref[t]
        grp = cls // pair

        @pl.when((t == 0) | (grp != tcls_ref[jnp.maximum(t - 1, 0)] // pair))
        def _():
            for e in range(EXPERTS_PER_GROUP):
                wgu_s[e, :, 0:D_EXPERT] = wg_ref[0, e].astype(BF16)
                wgu_s[e, :, D_EXPERT:2 * D_EXPERT] = wu_ref[0, e].astype(BF16)
                wd_s[e] = wd_ref[0, e].astype(BF16)

        pltpu.make_async_copy(hx_hbm.at[pl.ds(0, MOE_TILE)], xbuf.at[buf], sem.at[buf]).wait()
        h = xbuf[buf].reshape(MOE_TILE, D_MODEL)
        xn = _rms(h, gf_ref[...]).astype(BF16)
        ea = (cls % pair) // EXPERTS_PER_GROUP
        eb = cls % EXPERTS_PER_GROUP

        def column(w_ref):
            base = tbase_ref[t]
            sub = lax.broadcasted_iota(jnp.int32, (SUBLANES, 1), 0)
            chunks = []
            for c in range(MOE_TILE // SUBLANES):
                acc = jnp.zeros((SUBLANES, 1), F32)
                for k in range(SUBLANES):
                    acc = jnp.where(sub == k, w_ref[order_ref[base + c * SUBLANES + k]], acc)
                chunks.append(acc)
            return jnp.concatenate(chunks, axis=0)

        gus = [jnp.dot(xn, wgu_s[e], preferred_element_type=F32) for e in (ea, eb)]
        acts = [(jax.nn.silu(gu[:, 0:D_EXPERT]) * gu[:, D_EXPERT:2 * D_EXPERT]).astype(BF16)
                for gu in gus]
        oa, ob = [jnp.dot(act, wd_s[e], preferred_element_type=F32)
                  for act, e in zip(acts, (ea, eb))]
        y = h + column(wa_ref) * oa + column(wb_ref) * ob
        y_ref[...] = y.reshape(MOE_TILE, D_MODEL // LANES, LANES)

    @pl.when(t >= n_tiles)
    def _():
        y_ref[...] = jnp.zeros(y_ref.shape, F32)


def _moe(hx, gf, w_gate, w_up, w_down, plan, n_slots):
    tile_cls, tile_base, n_tiles, order, wa, wb, _ = plan
    pair = EXPERTS_PER_GROUP * EXPERTS_PER_GROUP
    assert n_slots // MOE_TILE <= tile_cls.shape[0]
    grouped = lambda w: w.reshape(N_GROUPS, EXPERTS_PER_GROUP, *w.shape[1:])
    wspec = lambda r, c: pl.BlockSpec((1, EXPERTS_PER_GROUP, r, c),
                                      lambda t, tc, *_: (tc[t] // pair, 0, 0, 0),
                                      pipeline_mode=pl.Buffered(1))
    grid_spec = pltpu.PrefetchScalarGridSpec(
        num_scalar_prefetch=6, grid=(n_slots // MOE_TILE,),
        in_specs=[pl.BlockSpec(memory_space=pl.ANY),
                  pl.BlockSpec(gf.shape, lambda t, *_: (0, 0)),
                  wspec(D_MODEL, D_EXPERT), wspec(D_MODEL, D_EXPERT), wspec(D_EXPERT, D_MODEL)],
        out_specs=pl.BlockSpec((MOE_TILE, D_MODEL // LANES, LANES), lambda t, *_: (t, 0, 0)),
        scratch_shapes=[pltpu.VMEM((2, MOE_TILE, TOKEN_ROWS, LANES), F32),
                        pltpu.VMEM((EXPERTS_PER_GROUP, D_MODEL, 2 * D_EXPERT), BF16),
                        pltpu.VMEM((EXPERTS_PER_GROUP, D_EXPERT, D_MODEL), BF16),
                        pltpu.SemaphoreType.DMA((2,))])
    return pl.pallas_call(
        _moe_kernel, grid_spec=grid_spec,
        out_shape=jax.ShapeDtypeStruct((n_slots, D_MODEL // LANES, LANES), F32),
        compiler_params=_cparams(("arbitrary",)), name="moe",
    )(tile_cls, tile_base, n_tiles, order, wa, wb, hx, gf,
      grouped(w_gate), grouped(w_up), grouped(w_down))


def _unsort_kernel(slot_ref, ys_hbm, y_ref, buf, sem, *, rows, row0):
    i = pl.program_id(0)
    b = i % 2

    def gather(step, dst):
        for j in range(rows):
            pltpu.make_async_copy(ys_hbm.at[slot_ref[row0 + step * rows + j]], buf.at[dst, j],
                                  sem.at[dst]).start()

    @pl.when(i == 0)
    def _():
        gather(0, 0)

    @pl.when(i + 1 < pl.num_programs(0))
    def _():
        gather(i + 1, 1 - b)

    pltpu.make_async_copy(ys_hbm.at[pl.ds(0, rows)], buf.at[b], sem.at[b]).wait()
    y_ref[...] = buf[b].reshape(rows, D_MODEL)


def _unsort(ys, token_slot, *, n, row0, rows):
    grid_spec = pltpu.PrefetchScalarGridSpec(
        num_scalar_prefetch=1, grid=(n // rows,),
        in_specs=[pl.BlockSpec(memory_space=pl.ANY)],
        out_specs=pl.BlockSpec((rows, D_MODEL), lambda i, *_: (i, 0)),
        scratch_shapes=[pltpu.VMEM((2, rows, D_MODEL // LANES, LANES), F32),
                        pltpu.SemaphoreType.DMA((2,))])
    return pl.pallas_call(
        functools.partial(_unsort_kernel, rows=rows, row0=row0), grid_spec=grid_spec,
        out_shape=jax.ShapeDtypeStruct((n, D_MODEL), F32),
        compiler_params=_cparams(("arbitrary",)), name="unsort",
    )(token_slot, ys)


def kernel(x_prompt, x_sample, cache_k, cache_v, state_conv, g_norm_mix, w_in, g_q, g_k, w_conv,
           g_attn_out, g_conv_out, w_out, g_norm_ffn, w_router_group, w_router_expert,
           w_gate, w_up, w_down):
    depth = g_norm_mix.shape[0]
    assert depth == 1
    nb, seq, _ = x_prompt.shape
    db, dec_seq, _ = x_sample.shape
    assert dec_seq == 1
    l = 0

    gmix = g_norm_mix[l][None, :]
    gq = jnp.tile(g_q[l], N_HEADS)[None, :]
    gk = jnp.tile(g_k[l], N_HEADS)[None, :]
    ga = g_attn_out[l][None, :]
    gc = g_conv_out[l][None, :]
    gf = g_norm_ffn[l][None, :]
    w_router = jnp.concatenate(
        [w_router_expert[l], w_router_group[l],
         jnp.zeros((D_MODEL, LANES - N_EXPERTS - N_GROUPS), F32)], axis=1)
    w_router_hi = w_router.astype(BF16)
    w_router_split = jnp.concatenate(
        [w_router_hi, (w_router - w_router_hi.astype(F32)).astype(BF16)], axis=1)

    cos_p, sin_p = _rope_tables(np.arange(seq))
    xp = x_prompt.reshape(nb * seq, D_MODEL)
    q, k, v, bc, tail, kt, vt = _inproj(xp, gmix, w_in[l].astype(BF16), gq, gk, cos_p, sin_p,
                                        w_conv[l], tm=512, seq_len=seq, hi=False,
                                        q_scale=ATTN_SCALE * LOG2_E, n_parts=2)
    attn = _attention(q.reshape(nb, seq, D_ATTN), k.reshape(nb, seq, D_ATTN),
                      v.reshape(nb, seq, D_ATTN))
    n_prompt = nb * seq
    n_total = n_prompt + db
    n_rows = -(-n_total // TOKEN_PAD) * TOKEN_PAD
    hx, route = _outproj(attn.reshape(n_prompt, D_ATTN), bc, xp, ga, gc, w_out[l].astype(BF16),
                         gf, w_router_split, tm=512, hi=False, n_rows=n_rows, row0=0, n_parts=2)
    k_prompt = jnp.transpose(kt, (0, 3, 1, 2))
    v_prompt = jnp.transpose(vt, (0, 3, 1, 2))
    conv_prompt = tail.reshape(nb, SUBLANES, D_CONV)[:, SUBLANES - (CONV_WIDTH - 1):]

    cos_s, sin_s = _rope_tables(np.full((db,), PAST_LEN))
    xs = x_sample.reshape(db, D_MODEL)
    st = state_conv[l]
    qs, ks, vs, bcs, us = _inproj(xs, gmix, w_in[l], gq, gk, cos_s, sin_s, w_conv[l],
                                  tm=db, seq_len=db, hi=True, q_scale=ATTN_SCALE,
                                  prev=(st[:, 1], st[:, 0]))
    heads = lambda t: t.reshape(db, N_HEADS, HEAD_DIM)
    attn_s = _attention_sample(heads(qs), heads(ks), heads(vs),
                               jnp.transpose(cache_k[l], (0, 2, 3, 1)),
                               jnp.transpose(cache_v[l], (0, 2, 3, 1)))
    hx, route = _outproj(attn_s.reshape(db, D_ATTN), bcs, xs, ga, gc, w_out[l], gf, w_router,
                         tm=db, hi=True, n_rows=n_rows, row0=n_prompt, into=(hx, route))

    n_cls = N_GROUPS * EXPERTS_PER_GROUP * (EXPERTS_PER_GROUP - 1) // 2
    n_slots = -(-(n_total + n_cls * (MOE_TILE - 1)) // MOE_TILE) * MOE_TILE
    assert n_slots // MOE_TILE <= N_CLASSES and n_rows % PLAN_ROWS == 0
    plan = _route_plan(route, n_total)
    ys = _moe(hx, gf, w_gate[l], w_up[l], w_down[l], plan, n_slots)
    y_prompt = _unsort(ys, plan[-1], n=n_prompt, row0=0, rows=512).reshape(nb, seq, D_MODEL)
    y_sample = _unsort(ys, plan[-1], n=db, row0=n_prompt, rows=db).reshape(db, 1, D_MODEL)
    k_sample = ks.reshape(db, 1, N_HEADS, HEAD_DIM)
    v_sample = vs.reshape(db, 1, N_HEADS, HEAD_DIM)
    conv_sample = jnp.stack([st[:, 1], us], axis=1)

    return (y_prompt, y_sample, k_prompt[None], v_prompt[None], conv_prompt[None],
            k_sample[None], v_sample[None], conv_sample[None])
```

```python
import functools

import numpy as np
import jax
import jax.numpy as jnp
from jax import lax
from jax.experimental import pallas as pl
from jax.experimental.pallas import tpu as pltpu

F32 = jnp.float32
BF16 = jnp.bfloat16

D_MODEL = 1024
HEAD_DIM = 64
N_HEADS = 12
D_ATTN = N_HEADS * HEAD_DIM
D_CONV = D_MODEL - D_ATTN
D_IN_PROJ = 3 * D_ATTN + 3 * D_CONV
CONV_WIDTH = 3
DILATIONS = (1, 4, 16)
ATT_BLOCK = 128
ATT_UNROLL = 16
REGROUP_STEP = 4
MAX_WINDOW = 2048
PAST_LEN = 8192
ATTN_SCALE = HEAD_DIM ** -0.5
LOG2_E = float(np.log2(np.e))
ROPE_THETA = 10000.0
EPS = 1e-6
N_GROUPS = 4
EXPERTS_PER_GROUP = 8
N_EXPERTS = N_GROUPS * EXPERTS_PER_GROUP
D_EXPERT = 256
MOE_TILE = 192
MOE_STEP_TILES = 2
N_CLASSES = N_GROUPS * EXPERTS_PER_GROUP * EXPERTS_PER_GROUP
PLAN_ROWS = 768
TOKEN_PAD = 512

LANES = 128
SUBLANES = 8
MXU_DIM = 256
TOKEN_ROWS = D_MODEL // LANES
NEG = -1e30
VMEM_LIMIT = 48 * 1024 * 1024


def _cparams(sem):
    return pltpu.CompilerParams(dimension_semantics=sem, vmem_limit_bytes=VMEM_LIMIT)


def _mm(a, b, hi):
    if hi:
        return jnp.dot(a, b, preferred_element_type=F32, precision=lax.Precision.HIGHEST)
    return jnp.dot(a.astype(BF16), b.astype(BF16), preferred_element_type=F32)


def _rms(x, g):
    return x * lax.rsqrt(jnp.mean(x * x, axis=-1, keepdims=True) + EPS) * g


def _rope_tables(pos):
    half = HEAD_DIM // 2
    inv = ROPE_THETA ** (-np.arange(half, dtype=np.float64) / half)
    ang = np.asarray(pos, np.float64)[:, None] * inv[None, :]
    cos, sin = np.cos(ang), np.sin(ang)
    cos2 = np.concatenate([cos, cos, cos, cos], axis=-1)
    sin2 = np.concatenate([-sin, sin, -sin, sin], axis=-1)
    return jnp.asarray(cos2, F32), jnp.asarray(sin2, F32)


def _inproj_kernel(*refs, tm, tiles_per_seq, keep_tiles, hi, given_prev, n_parts, q_scale):
    if given_prev:
        (x_ref, gmix_ref, w_ref, gq_ref, gk_ref, cos_ref, sin_ref, wconv_ref, p1_ref, p2_ref,
         q_ref, k_ref, v_ref, bc_ref, u_ref) = refs
    else:
        (x_ref, gmix_ref, w_ref, gq_ref, gk_ref, cos_ref, sin_ref, wconv_ref,
         q_ref, k_ref, v_ref, bc_ref, tail_ref, kt_ref, vt_ref, ubuf) = refs
    part_rows = tm // n_parts

    r_i = lax.broadcasted_iota(jnp.int32, (MXU_DIM, MXU_DIM), 0) // HEAD_DIM
    c_i = lax.broadcasted_iota(jnp.int32, (MXU_DIM, MXU_DIM), 1) // HEAD_DIM
    head_sum = jnp.where(r_i == c_i, 1.0, 0.0).astype(F32 if hi else BF16)
    lane = lax.broadcasted_iota(jnp.int32, (1, D_ATTN), 1)
    first_half = (lane % HEAD_DIM) < (HEAD_DIM // 2)
    wc = wconv_ref[...]

    def head_ssq(z):
        sq = z * z
        return jnp.concatenate(
            [_mm(sq[:, c * MXU_DIM:(c + 1) * MXU_DIM], head_sum, hi)
             for c in range(D_ATTN // MXU_DIM)], axis=-1)

    def norm_rope(z, ssq, g_ref, cos, sin):
        zn = z * lax.rsqrt(ssq * (1.0 / HEAD_DIM) + EPS) * g_ref[...]
        partner = jnp.where(first_half,
                            pltpu.roll(zn, D_ATTN - HEAD_DIM // 2, 1),
                            pltpu.roll(zn, HEAD_DIM // 2, 1))
        return zn * cos + partner * sin

    def conv_gate(b_gate, u2, u1, u):
        return b_gate * (wc[0:1, :] * u2 + wc[1:2, :] * u1 + wc[2:3, :] * u)

    def part(p, transposed):
        rows = pl.ds(p * part_rows, part_rows)
        xn = _rms(x_ref[rows, :], gmix_ref[...])
        if not hi:
            xn = xn.astype(BF16)
        zq = _mm(xn, w_ref[:, 0:D_ATTN], hi)
        zk = _mm(xn, w_ref[:, D_ATTN:2 * D_ATTN], hi)
        ssq_q = head_ssq(zq)
        v = _mm(xn, w_ref[:, 2 * D_ATTN:3 * D_ATTN], hi)
        ssq_k = head_ssq(zk)
        c0 = 3 * D_ATTN
        b_gate = _mm(xn, w_ref[:, c0:c0 + D_CONV], hi)
        c_gate = _mm(xn, w_ref[:, c0 + D_CONV:c0 + 2 * D_CONV], hi)
        u_in = _mm(xn, w_ref[:, c0 + 2 * D_CONV:c0 + 3 * D_CONV], hi)
        yield
        reps = D_ATTN // LANES
        cos = jnp.concatenate([cos_ref[rows, :]] * reps, axis=-1)
        sin = jnp.concatenate([sin_ref[rows, :]] * reps, axis=-1)
        q_ref[rows, :] = norm_rope(zq, ssq_q, gq_ref, cos, sin) * q_scale
        k = norm_rope(zk, ssq_k, gk_ref, cos, sin)
        k_ref[rows, :] = k
        v_ref[rows, :] = v
        if transposed:
            cols = slice(p * part_rows, (p + 1) * part_rows)
            kt_ref[0, :, :, cols] = k.T.reshape(N_HEADS, HEAD_DIM, part_rows)
            vt_ref[0, :, :, cols] = v.T.reshape(N_HEADS, HEAD_DIM, part_rows)
        u = c_gate * u_in
        if given_prev:
            u_ref[rows, :] = u
            bc_ref[rows, :] = conv_gate(b_gate, p2_ref[rows, :], p1_ref[rows, :], u)
        else:
            ubuf[pl.ds(SUBLANES + p * part_rows, part_rows), :] = u
        return b_gate

    def body(transposed):
        stages = [part(p, transposed) for p in range(n_parts)]
        for stage in stages:
            next(stage)
        gates = []
        for stage in stages:
            try:
                next(stage)
            except StopIteration as done:
                gates.append(done.value)
        if not given_prev:
            u = ubuf[SUBLANES:tm + SUBLANES, :]
            u1 = ubuf[SUBLANES - 1:tm + SUBLANES - 1, :]
            u2 = ubuf[SUBLANES - 2:tm + SUBLANES - 2, :]
            bc_ref[...] = conv_gate(jnp.concatenate(gates, axis=0), u2, u1, u)
            tail_ref[...] = ubuf[tm:tm + SUBLANES, :]

    if given_prev:
        body(False)
        return

    i = pl.program_id(0)

    @pl.when(i % tiles_per_seq == 0)
    def _():
        ubuf[0:SUBLANES, :] = jnp.zeros((SUBLANES, D_CONV), F32)

    @pl.when(i % tiles_per_seq != 0)
    def _():
        ubuf[0:SUBLANES, :] = ubuf[tm:tm + SUBLANES, :]

    kept = i % tiles_per_seq >= tiles_per_seq - keep_tiles
    pl.when(kept)(functools.partial(body, True))
    pl.when(jnp.logical_not(kept))(functools.partial(body, False))


def _inproj(x, gmix, w_in, gq, gk, cos, sin, w_conv, *, tm, seq_len, hi, q_scale, n_parts=1,
            prev=None):
    n = x.shape[0]
    nt = n // tm
    tiles_per_seq = seq_len // tm
    row = lambda w: pl.BlockSpec((tm, w), lambda i: (i, 0))
    full = lambda a: pl.BlockSpec(a.shape, lambda i: (0,) * a.ndim)
    tab = pl.BlockSpec((tm, LANES), lambda i: (i % tiles_per_seq, 0))
    once = pl.BlockSpec(w_in.shape, lambda i: (0, 0), pipeline_mode=pl.Buffered(1))
    in_specs = [row(D_MODEL), full(gmix), once, full(gq), full(gk), tab, tab, full(w_conv)]
    args = [x, gmix, w_in, gq, gk, cos, sin, w_conv]
    out_shape = [jax.ShapeDtypeStruct((n, D_ATTN), F32)] * 3 + [jax.ShapeDtypeStruct((n, D_CONV), F32)]
    out_specs = [row(D_ATTN)] * 3 + [row(D_CONV)]
    scratch = []
    keep_tiles = min(MAX_WINDOW, seq_len) // tm
    if prev is not None:
        in_specs += [row(D_CONV), row(D_CONV)]
        args += list(prev)
        out_shape.append(jax.ShapeDtypeStruct((n, D_CONV), F32))
        out_specs.append(row(D_CONV))
    else:
        n_seq = n // seq_len
        out_shape.append(jax.ShapeDtypeStruct((n_seq * SUBLANES, D_CONV), F32))
        out_specs.append(pl.BlockSpec((SUBLANES, D_CONV), lambda i: (i // tiles_per_seq, 0)))
        kept = pl.BlockSpec(
            (1, N_HEADS, HEAD_DIM, tm),
            lambda i: (i // tiles_per_seq, 0, 0,
                       jnp.maximum(i % tiles_per_seq - (tiles_per_seq - keep_tiles), 0)))
        out_shape += [jax.ShapeDtypeStruct((n_seq, N_HEADS, HEAD_DIM, keep_tiles * tm), F32)] * 2
        out_specs += [kept, kept]
        scratch.append(pltpu.VMEM((tm + SUBLANES, D_CONV), F32))
    return pl.pallas_call(
        functools.partial(_inproj_kernel, tm=tm, tiles_per_seq=tiles_per_seq,
                          keep_tiles=keep_tiles, hi=hi, given_prev=prev is not None,
                          n_parts=n_parts, q_scale=q_scale),
        grid=(nt,), in_specs=in_specs, out_specs=out_specs, out_shape=out_shape,
        scratch_shapes=scratch, compiler_params=_cparams(("arbitrary",)),
        name="inproj_hi" if hi else "inproj",
    )(*args)


def _attn_kernel(q_ref, k_ref, v_ref, o_ref, num_ref, den_ref, m_ref, bias_ref,
                 qd_ref, kd_ref, vd_ref, numd_ref, dend_ref, md_ref, tmp_ref, *, seq_len):
    blk = ATT_BLOCK
    qi = lax.broadcasted_iota(jnp.int32, (blk, 2 * blk), 0)
    kj = lax.broadcasted_iota(jnp.int32, (blk, 2 * blk), 1)
    bias_ref[...] = jnp.where((kj >= qi) & (kj <= qi + blk), 0.0, NEG)
    head0 = lax.broadcasted_iota(jnp.int32, (1, LANES), 1) < HEAD_DIM
    dn = (((1,), (1,)), ((), ()))

    def rows(start, size, d):
        return pl.ds(start, size) if d == 1 else pl.ds(start, size, stride=d)

    natural = (lambda at: q_ref[0, at, :], lambda at: k_ref[0, at, :], lambda at: v_ref[0, at, :],
               (num_ref, den_ref, m_ref))
    regrouped = (lambda at: qd_ref[at, :], lambda at: kd_ref[at, :], lambda at: vd_ref[at, :],
                 (numd_ref, dend_ref, md_ref))

    def group(src, d, r, n0, u, first, mode):
        q_at, k_at, v_at, (num_acc, den_acc, m_acc) = src
        q_rows = rows(n0 * (blk * d) + r, u * blk, d)
        if first:
            k_rows = rows(r, u * blk, d)
        else:
            k_rows = rows((n0 - 1) * (blk * d) + r, (u + 1) * blk, d)
        qa = q_at(q_rows)
        ka = k_at(k_rows).astype(BF16)
        va = v_at(k_rows)
        qa0 = jnp.where(head0, qa, 0.0).astype(BF16)
        qa1 = jnp.where(head0, 0.0, qa).astype(BF16)
        va0 = jnp.where(head0, va, 1.0).astype(BF16)
        va1 = jnp.where(head0, 1.0, va).astype(BF16)
        if mode != "init":
            m_old = m_acc[q_rows, :]
            num_old = num_acc[q_rows, :]
            den_old = den_acc[q_rows, :]
        key_slices, scores = [], []
        for i in range(u):
            qs = slice(i * blk, (i + 1) * blk)
            if first and i == 0:
                ks = slice(0, blk)
                bias = bias_ref[:, blk:2 * blk]
            else:
                k0 = (i - 1) * blk if first else i * blk
                ks = slice(k0, k0 + 2 * blk)
                bias = bias_ref[...]
            key_slices.append(ks)
            scores.append(
                (lax.dot_general(qa0[qs], ka[ks], dn, preferred_element_type=F32) + bias,
                 lax.dot_general(qa1[qs], ka[ks], dn, preferred_element_type=F32) + bias))
        yield
        probs, ms = [], []
        for s0, s1 in scores:
            m0 = jnp.max(s0, axis=-1, keepdims=True)
            m1 = jnp.max(s1, axis=-1, keepdims=True)
            probs.append((jnp.exp2(s0 - m0).astype(BF16), jnp.exp2(s1 - m1).astype(BF16)))
            ms.append(jnp.where(head0, m0, m1))
        yield
        nums, dens = [], []
        for (p0, p1), ks in zip(probs, key_slices):
            r0 = jnp.dot(p0, va0[ks], preferred_element_type=F32)
            r1 = jnp.dot(p1, va1[ks], preferred_element_type=F32)
            nums.append(jnp.where(head0, r0, r1))
            dens.append(pltpu.roll(jnp.where(head0, r1, r0), HEAD_DIM, 1))
        num = jnp.concatenate(nums, axis=0)
        den = jnp.concatenate(dens, axis=0)
        m = jnp.concatenate(ms, axis=0)
        if mode == "init":
            return [(num_acc, q_rows, num), (den_acc, q_rows, den), (m_acc, q_rows, m)]
        m_new = jnp.maximum(m_old, m)
        a = jnp.exp2(m_old - m_new)
        b = jnp.exp2(m - m_new)
        num = num_old * a + num * b
        den = den_old * a + den * b
        if mode == "merge":
            return [(num_acc, q_rows, num), (den_acc, q_rows, den), (m_acc, q_rows, m_new)]
        return [(o_ref.at[0], q_rows, num / den)]

    def run(groups):
        stages = [group(*g) for g in groups]
        for _ in range(2):
            for stage in stages:
                next(stage)
        stores = []
        for stage in stages:
            try:
                next(stage)
            except StopIteration as done:
                stores += done.value
        for ref, at, val in stores:
            ref[at, :] = val

    def branch(src, d, n_seq, sub_len, seq_stride, mode):
        nb = sub_len // blk
        u = min(ATT_UNROLL, nb)
        if nb == u:
            per_body = ATT_UNROLL // u

            def several(rr, carry):
                run([(src, d, (rr * per_body + t) * seq_stride, 0, u, True, mode)
                     for t in range(per_body)])
                return carry

            lax.fori_loop(0, n_seq // per_body, several, 0)
        else:
            def one(r, carry):
                run([(src, d, r * seq_stride, 0, u, True, mode)])

                def later_blocks(g, c):
                    run([(src, d, r * seq_stride, g * u, u, False, mode)])
                    return c

                lax.fori_loop(1, nb // u, later_blocks, 0)
                return carry

            lax.fori_loop(0, n_seq, one, 0)

    step = REGROUP_STEP
    wide = step * step
    assert DILATIONS == (1, step, wide)
    part = seq_len // wide

    def regroup(load, dst_ref):
        for a in range(step):
            tmp_ref[a * step * part:(a + 1) * step * part, :] = load(rows(a, step * part, step))
        for a in range(step):
            for b in range(step):
                s = a * step + b
                dst_ref[s * part:(s + 1) * part, :] = tmp_ref[rows(a * step * part + b, part, step), :]

    def ungroup(src_ref, dst_ref):
        for a in range(step):
            for b in range(step):
                s = a * step + b
                tmp_ref[rows(a * step * part + b, part, step), :] = src_ref[s * part:(s + 1) * part, :]
        for a in range(step):
            dst_ref[rows(a, step * part, step), :] = tmp_ref[a * step * part:(a + 1) * step * part, :]

    for load, dst in zip(natural[:3], (qd_ref, kd_ref, vd_ref)):
        regroup(load, dst)
    branch(regrouped, 1, wide, part, part, "init")
    for src, dst in zip(regrouped[3], natural[3]):
        ungroup(src, dst)
    branch(natural, 1, 1, seq_len, 0, "merge")
    branch(natural, step, step, seq_len // step, 1, "final")


def _attention(q, k, v):
    b, s, _ = q.shape
    spec = pl.BlockSpec((1, s, LANES), lambda i, p: (i, 0, p))
    return pl.pallas_call(
        functools.partial(_attn_kernel, seq_len=s),
        grid=(b, D_ATTN // LANES), in_specs=[spec] * 3, out_specs=spec,
        out_shape=jax.ShapeDtypeStruct((b, s, D_ATTN), F32),
        scratch_shapes=[pltpu.VMEM((s, LANES), F32)] * 3 + [pltpu.VMEM((ATT_BLOCK, 2 * ATT_BLOCK), F32)]
        + [pltpu.VMEM((s, LANES), F32)] * 7,
        compiler_params=_cparams(("arbitrary", "arbitrary")), name="attn_prompt",
    )(q, k, v)


def _attn_sample_kernel(q_ref, kn_ref, vn_ref, kc_ref, vc_ref, o_ref, *, w_buf, heads):
    age = w_buf - lax.broadcasted_iota(jnp.int32, (1, w_buf), 1)
    valid = [(age % d == 0) & (age <= d * ATT_BLOCK) for d in DILATIONS]
    diag = (lax.broadcasted_iota(jnp.int32, (HEAD_DIM, HEAD_DIM), 0)
            == lax.broadcasted_iota(jnp.int32, (HEAD_DIM, HEAD_DIM), 1))
    to_col = lambda row: jnp.sum(jnp.where(diag, row, 0.0), axis=1, keepdims=True)
    to_row = lambda col: jnp.sum(jnp.where(diag, col, 0.0), axis=0, keepdims=True)
    q_rows, kn_rows, vn_rows = q_ref[0], kn_ref[0], vn_ref[0]
    s_self_all = jnp.sum(kn_rows * q_rows, axis=1, keepdims=True)
    for h in range(heads):
        q = to_col(q_rows[h:h + 1])
        vn = to_col(vn_rows[h:h + 1])
        s = jnp.sum(kc_ref[0, h] * q, axis=0, keepdims=True)
        s_self = s_self_all[h:h + 1]
        parts = []
        for ok in valid:
            sd = jnp.where(ok, s, NEG)
            m = jnp.maximum(jnp.max(sd, axis=1, keepdims=True), s_self)
            p = jnp.exp(sd - m)
            p_self = jnp.exp(s_self - m)
            parts.append((p, p_self, jnp.sum(p, axis=1, keepdims=True) + p_self, m))
        m_all = jnp.maximum(jnp.maximum(parts[0][3], parts[1][3]), parts[2][3])
        w = jnp.zeros((1, w_buf), F32)
        w_self = jnp.zeros((1, 1), F32)
        den_t = jnp.zeros((1, 1), F32)
        for p, p_self, den, m in parts:
            c = jnp.exp(m - m_all)
            w = w + p * c
            w_self = w_self + p_self * c
            den_t = den_t + den * c
        num_t = jnp.sum(vc_ref[0, h] * w, axis=1, keepdims=True) + vn * w_self
        o_ref[0, h:h + 1, :] = to_row(num_t / den_t)


def _attention_sample(q, kn, vn, cache_kt, cache_vt):
    db, heads, _, w_buf = cache_kt.shape
    tok = pl.BlockSpec((1, heads, HEAD_DIM), lambda i: (i, 0, 0))
    cache = pl.BlockSpec((1, heads, HEAD_DIM, w_buf), lambda i: (i, 0, 0, 0))
    return pl.pallas_call(
        functools.partial(_attn_sample_kernel, w_buf=w_buf, heads=heads),
        grid=(db,), in_specs=[tok, tok, tok, cache, cache], out_specs=tok,
        out_shape=jax.ShapeDtypeStruct((db, heads, HEAD_DIM), F32),
        compiler_params=_cparams(("arbitrary",)), name="attn_sample",
    )(q, kn, vn, cache_kt, cache_vt)


def _outproj_kernel(attn_ref, bc_ref, x_ref, ga_ref, gc_ref, wo_ref, gf_ref, wr_ref, *rest,
                    hi, n_tiles, n_parts):
    hx_ref, route_ref = rest[-2:]
    part_rows = x_ref.shape[0] // n_parts

    def part(k):
        rows = pl.ds(k * part_rows, part_rows)
        a = _rms(attn_ref[rows, :], ga_ref[...])
        c = _rms(bc_ref[rows, :], gc_ref[...])
        mix = _mm(a, wo_ref[0:D_ATTN, :], hi) + _mm(c, wo_ref[D_ATTN:D_MODEL, :], hi)
        yield
        h = x_ref[rows, :] + mix
        hn = _rms(h, gf_ref[...])
        if hi:
            lg = _mm(hn, wr_ref[...], True)
        else:
            hn_hi = hn.astype(BF16)
            hn_lo = (hn - hn_hi.astype(F32)).astype(BF16)
            both = jnp.dot(hn_hi, wr_ref[...], preferred_element_type=F32)
            lg = (both[:, :LANES] + both[:, LANES:]
                  + jnp.dot(hn_lo, wr_ref[:, :LANES], preferred_element_type=F32))
        yield
        lane_i = lax.broadcasted_iota(jnp.int32, lg.shape, 1)
        lane = lane_i.astype(F32)
        lane_group = (lane_i // EXPERTS_PER_GROUP).astype(F32)
        is_e = lane_i < N_EXPERTS
        is_g = (lane_i >= N_EXPERTS) & (lane_i < N_EXPERTS + N_GROUPS)
        big = float(LANES)
        first_at = lambda t, v: jnp.min(jnp.where(t == v, lane, big), axis=-1, keepdims=True)

        gl = jnp.where(is_g, lg, NEG)
        mg = jnp.max(gl, axis=-1, keepdims=True)
        p_g = 1.0 / jnp.sum(jnp.exp(gl - mg), axis=-1, keepdims=True)
        g_idx = first_at(gl, mg) - float(N_EXPERTS)
        el = jnp.where(is_e & (lane_group == g_idx), lg, NEG)
        v1 = jnp.max(el, axis=-1, keepdims=True)
        i1 = first_at(el, v1)
        el2 = jnp.where(lane == i1, NEG, el)
        v2 = jnp.max(el2, axis=-1, keepdims=True)
        i2 = first_at(el2, v2)
        e2 = jnp.exp(v2 - v1)
        w1 = p_g / (1.0 + e2)
        w2 = p_g * e2 / (1.0 + e2)
        route_ref[rows, :] = sum(jnp.where(lane_i == j, val, 0.0)
                                 for j, val in enumerate([i1, i2, w1, w2]))
        hx_ref[rows, :, :] = h.reshape(part_rows, TOKEN_ROWS, LANES)

    stages = [part(k) for k in range(n_parts)]
    for _ in range(2):
        for stage in stages:
            next(stage)
    for stage in stages:
        for _ in stage:
            pass

    @pl.when(pl.program_id(0) >= n_tiles)
    def _():
        hx_ref[...] = jnp.zeros(hx_ref.shape, F32)
        route_ref[...] = jnp.zeros(route_ref.shape, F32)


def _outproj(attn, bc, x, ga, gc, w_out, gf, w_router, *, tm, hi, n_rows, row0, n_parts=1,
             into=None):
    n = x.shape[0]
    n_tiles = n // tm
    n_fill = 0 if into is not None else -(-(n_rows - n) // tm)
    row = lambda w: pl.BlockSpec((tm, w), lambda i: (jnp.minimum(i, n_tiles - 1), 0))
    full = lambda a: pl.BlockSpec(a.shape, lambda i: (0,) * a.ndim)
    in_specs = [row(D_ATTN), row(D_CONV), row(D_MODEL), full(ga), full(gc), full(w_out),
                full(gf), full(w_router)]
    args = [attn, bc, x, ga, gc, w_out, gf, w_router]
    aliases = {}
    if into is not None:
        in_specs += [pl.BlockSpec(memory_space=pl.ANY)] * 2
        args += list(into)
        aliases = {len(args) - 2: 0, len(args) - 1: 1}
    return pl.pallas_call(
        functools.partial(_outproj_kernel, hi=hi, n_tiles=n_tiles, n_parts=n_parts),
        grid=(n_tiles + n_fill,), in_specs=in_specs,
        out_specs=[pl.BlockSpec((tm, TOKEN_ROWS, LANES), lambda i: (i + row0 // tm, 0, 0)),
                   pl.BlockSpec((tm, LANES), lambda i: (i + row0 // tm, 0))],
        out_shape=[jax.ShapeDtypeStruct((n_rows, TOKEN_ROWS, LANES), F32),
                   jax.ShapeDtypeStruct((n_rows, LANES), F32)],
        input_output_aliases=aliases,
        compiler_params=_cparams(("arbitrary",)), name="outproj_hi" if hi else "outproj",
    )(*args)


def _plan_kernel(route_ref, slot_ref, pos_ref, wa_ref, wb_ref, tiles_ref,
                 run_ref, cls_s, rank_s, later_s, *, n_steps, n_tokens):
    i = pl.program_id(0)
    rows = route_ref.shape[0]
    e = float(EXPERTS_PER_GROUP)

    @pl.when(i == 0)
    def _():
        run_ref[...] = jnp.zeros(run_ref.shape, F32)
        later_s[...] = jnp.where(lax.broadcasted_iota(jnp.int32, (rows, rows), 0)
                                 < lax.broadcasted_iota(jnp.int32, (rows, rows), 1),
                                 1.0, 0.0).astype(BF16)

    r = route_ref[...].T
    i1, i2, w1, w2 = r[0:1], r[1:2], r[2:3], r[3:4]
    grp = jnp.floor(i1 * (1.0 / e))
    cls = grp * (e * e) + (jnp.minimum(i1, i2) - grp * e) * e + (jnp.maximum(i1, i2) - grp * e)
    token = i * rows + lax.broadcasted_iota(jnp.int32, (1, rows), 1)
    class_id = lax.broadcasted_iota(jnp.int32, (N_CLASSES, rows), 0).astype(F32)
    own = (class_id == cls) & (token < n_tokens)
    before = jnp.dot(jnp.where(own, 1.0, 0.0).astype(BF16), later_s[...],
                     preferred_element_type=F32)
    rank = jnp.sum(jnp.where(own, before + run_ref[...], 0.0), axis=0, keepdims=True)
    run_ref[...] += jnp.sum(jnp.where(own, 1.0, 0.0), axis=1, keepdims=True)
    first_is_a = i1 < i2
    lines = ((cls_s, cls), (rank_s, rank),
             (wa_ref, jnp.where(first_is_a, w1, w2)), (wb_ref, jnp.where(first_is_a, w2, w1)))
    for c in range(rows // LANES):
        for ref, line in lines:
            ref[pl.ds(i * (rows // LANES) + c, 1), :] = line[:, c * LANES:(c + 1) * LANES]

    @pl.when(i == n_steps - 1)
    def _():
        counts = run_ref[...]
        tiles = jnp.floor((counts + (MOE_TILE - 1.0)) * (1.0 / MOE_TILE))
        cid = lax.broadcasted_iota(jnp.int32, (N_CLASSES, 1), 0)
        per_group = EXPERTS_PER_GROUP * EXPERTS_PER_GROUP
        for g in range(N_GROUPS):
            g_tiles = jnp.sum(jnp.where(cid // per_group == g, tiles, 0.0), axis=0, keepdims=True)
            spare = jnp.ceil(g_tiles * (1.0 / MOE_STEP_TILES)) * MOE_STEP_TILES - g_tiles
            tiles = tiles + jnp.where(cid == (g + 1) * per_group - 1, spare, 0.0)
        upto = jnp.where(lax.broadcasted_iota(jnp.int32, (N_CLASSES, N_CLASSES), 0)
                         >= lax.broadcasted_iota(jnp.int32, (N_CLASSES, N_CLASSES), 1), 1.0, 0.0)
        lane = lax.broadcasted_iota(jnp.int32, (N_CLASSES, LANES), 1)
        both = jnp.where(lane == 0, tiles, 0.0) + jnp.where(lane == 1, counts, 0.0)
        ends = _mm(upto, both, True)
        tile_end, tok_end = ends[:, 0:1], ends[:, 1:2]
        tile_off, tok_off = tile_end - tiles, tok_end - counts
        n_tiles = jnp.max(tile_end, axis=0, keepdims=True)
        t = lax.broadcasted_iota(jnp.int32, (1, N_CLASSES), 1).astype(F32)
        t = jnp.minimum(t, n_tiles - 1.0)
        cls_t = jnp.sum(jnp.where(tile_end <= t, 1.0, 0.0), axis=0, keepdims=True)
        mine = lax.broadcasted_iota(jnp.int32, (N_CLASSES, N_CLASSES), 0).astype(F32) == cls_t
        pick = lambda v: jnp.sum(jnp.where(mine, v, 0.0), axis=0, keepdims=True)
        base = pick(tok_off) + (t - pick(tile_off)) * MOE_TILE
        row8 = lax.broadcasted_iota(jnp.int32, (SUBLANES, N_CLASSES), 0)
        tiles_ref[...] = sum(jnp.where(row8 == j, v, 0.0)
                             for j, v in enumerate([cls_t, base, n_tiles + jnp.zeros_like(t)]))

        token_class = lax.broadcasted_iota(jnp.int32, (N_CLASSES, LANES), 0).astype(F32)

        def per_block(j, carry):
            at = pl.ds(j, 1)
            hit = token_class == cls_s[at, :]
            look = lambda col: jnp.sum(jnp.where(hit, col, 0.0), axis=0, keepdims=True)
            rank_row = rank_s[at, :]
            slot_ref[at, :] = (look(tile_off) * MOE_TILE + rank_row).astype(jnp.int32)
            pos_ref[at, :] = (look(tok_off) + rank_row).astype(jnp.int32)
            return carry

        lax.fori_loop(0, slot_ref.shape[0], per_block, 0)


def _plan_order_kernel(pos_ref, order_ref, *, n_tokens):
    def place(t, carry):
        order_ref[pos_ref[t]] = t
        return carry

    lax.fori_loop(0, n_tokens, place, 0, unroll=32)
    for j in range(MOE_TILE):
        order_ref[n_tokens + j] = 0


def _route_plan(route, n_tokens):
    n_rows = route.shape[0]
    rows = PLAN_ROWS
    n_steps = n_rows // rows
    dense = lambda dtype: jax.ShapeDtypeStruct((n_rows // LANES, LANES), dtype)
    whole = lambda r, c: pl.BlockSpec((r, c), lambda i: (0, 0))
    slot, pos, wa, wb, tiles = pl.pallas_call(
        functools.partial(_plan_kernel, n_steps=n_steps, n_tokens=n_tokens),
        grid=(n_steps,), in_specs=[pl.BlockSpec((rows, LANES), lambda i: (i, 0))],
        out_specs=[whole(n_rows // LANES, LANES)] * 4 + [whole(SUBLANES, N_CLASSES)],
        out_shape=[dense(jnp.int32), dense(jnp.int32), dense(F32), dense(F32),
                   jax.ShapeDtypeStruct((SUBLANES, N_CLASSES), F32)],
        scratch_shapes=[pltpu.VMEM((N_CLASSES, 1), F32),
                        pltpu.VMEM((n_rows // LANES, LANES), F32),
                        pltpu.VMEM((n_rows // LANES, LANES), F32),
                        pltpu.VMEM((rows, rows), BF16)],
        compiler_params=_cparams(("arbitrary",)), name="plan",
    )(route)
    order = pl.pallas_call(
        functools.partial(_plan_order_kernel, n_tokens=n_tokens),
        in_specs=[pl.BlockSpec(memory_space=pltpu.SMEM)],
        out_specs=pl.BlockSpec(memory_space=pltpu.SMEM),
        out_shape=jax.ShapeDtypeStruct((n_tokens + MOE_TILE,), jnp.int32), name="plan_order",
    )(pos.reshape(-1))
    tile_cls = tiles[0].astype(jnp.int32)
    tile_base = tiles[1].astype(jnp.int32)
    n_tiles = tiles[2, 0:1].astype(jnp.int32)
    return tile_cls, tile_base, n_tiles, order, wa.reshape(-1), wb.reshape(-1), slot.reshape(-1)


def _moe_kernel(tcls_ref, tbase_ref, nt_ref, order_ref, wa_ref, wb_ref,
                hx_hbm, gf_ref, wg_ref, wu_ref, wd_ref, y_ref,
                xbuf, wgu_s, wd_s, sem):
    s = pl.program_id(0)
    n_tiles = nt_ref[0]
    pair = EXPERTS_PER_GROUP * EXPERTS_PER_GROUP
    n_buf = xbuf.shape[0]
    first_tile = s * MOE_STEP_TILES

    def gather(tile):
        base = tbase_ref[tile]
        for j in range(MOE_TILE):
            pltpu.make_async_copy(hx_hbm.at[order_ref[base + j]], xbuf.at[tile % n_buf, j],
                                  sem.at[tile % n_buf]).start()

    for k in range(MOE_STEP_TILES):
        @pl.when((s == 0) & (k < n_tiles))
        def _(k=k):
            gather(k)

    for k in range(MOE_STEP_TILES):
        @pl.when(first_tile + MOE_STEP_TILES + k < n_tiles)
        def _(k=k):
            gather(first_tile + MOE_STEP_TILES + k)

    @pl.when(first_tile < n_tiles)
    def _():
        grp = tcls_ref[first_tile] // pair

        @pl.when((s == 0) | (grp != tcls_ref[jnp.maximum(first_tile - 1, 0)] // pair))
        def _():
            for e in range(EXPERTS_PER_GROUP):
                wgu_s[e, :, 0:D_EXPERT] = wg_ref[0, e].astype(BF16)
                wgu_s[e, :, D_EXPERT:2 * D_EXPERT] = wu_ref[0, e].astype(BF16)
                wd_s[e] = wd_ref[0, e].astype(BF16)

    def run_tile(tile, out_rows):
        cls = tcls_ref[tile]
        slot = tile % n_buf
        pltpu.make_async_copy(hx_hbm.at[pl.ds(0, MOE_TILE)], xbuf.at[slot], sem.at[slot]).wait()
        h = xbuf[slot].reshape(MOE_TILE, D_MODEL)
        xn = _rms(h, gf_ref[...]).astype(BF16)
        ea = (cls % pair) // EXPERTS_PER_GROUP
        eb = cls % EXPERTS_PER_GROUP

        def column(w_ref):
            base = tbase_ref[tile]
            sub = lax.broadcasted_iota(jnp.int32, (SUBLANES, 1), 0)
            chunks = []
            for c in range(MOE_TILE // SUBLANES):
                acc = jnp.zeros((SUBLANES, 1), F32)
                for k in range(SUBLANES):
                    acc = jnp.where(sub == k, w_ref[order_ref[base + c * SUBLANES + k]], acc)
                chunks.append(acc)
            return jnp.concatenate(chunks, axis=0)

        gus = [jnp.dot(xn, wgu_s[e], preferred_element_type=F32) for e in (ea, eb)]
        acts = [(jax.nn.silu(gu[:, 0:D_EXPERT]) * gu[:, D_EXPERT:2 * D_EXPERT]).astype(BF16)
                for gu in gus]
        oa, ob = [jnp.dot(act, wd_s[e], preferred_element_type=F32)
                  for act, e in zip(acts, (ea, eb))]
        y = h + column(wa_ref) * oa + column(wb_ref) * ob
        y_ref[out_rows, :, :] = y.reshape(MOE_TILE, D_MODEL // LANES, LANES)

    for k in range(MOE_STEP_TILES):
        out_rows = pl.ds(k * MOE_TILE, MOE_TILE)
        pl.when(first_tile + k < n_tiles)(functools.partial(run_tile, first_tile + k, out_rows))

        @pl.when(first_tile + k >= n_tiles)
        def _(out_rows=out_rows):
            y_ref[out_rows, :, :] = jnp.zeros((MOE_TILE, D_MODEL // LANES, LANES), F32)


def _moe(hx, gf, w_gate, w_up, w_down, plan, n_slots):
    tile_cls, tile_base, n_tiles, order, wa, wb, _ = plan
    pair = EXPERTS_PER_GROUP * EXPERTS_PER_GROUP
    assert n_slots % (MOE_STEP_TILES * MOE_TILE) == 0 and n_slots // MOE_TILE <= tile_cls.shape[0]
    grouped = lambda w: w.reshape(N_GROUPS, EXPERTS_PER_GROUP, *w.shape[1:])
    wspec = lambda r, c: pl.BlockSpec((1, EXPERTS_PER_GROUP, r, c),
                                      lambda s, tc, *_: (tc[s * MOE_STEP_TILES] // pair, 0, 0, 0),
                                      pipeline_mode=pl.Buffered(1))
    grid_spec = pltpu.PrefetchScalarGridSpec(
        num_scalar_prefetch=6, grid=(n_slots // (MOE_STEP_TILES * MOE_TILE),),
        in_specs=[pl.BlockSpec(memory_space=pl.ANY),
                  pl.BlockSpec(gf.shape, lambda t, *_: (0, 0)),
                  wspec(D_MODEL, D_EXPERT), wspec(D_MODEL, D_EXPERT), wspec(D_EXPERT, D_MODEL)],
        out_specs=pl.BlockSpec((MOE_STEP_TILES * MOE_TILE, D_MODEL // LANES, LANES),
                               lambda s, *_: (s, 0, 0)),
        scratch_shapes=[pltpu.VMEM((2 * MOE_STEP_TILES, MOE_TILE, TOKEN_ROWS, LANES), F32),
                        pltpu.VMEM((EXPERTS_PER_GROUP, D_MODEL, 2 * D_EXPERT), BF16),
                        pltpu.VMEM((EXPERTS_PER_GROUP, D_EXPERT, D_MODEL), BF16),
                        pltpu.SemaphoreType.DMA((2 * MOE_STEP_TILES,))])
    return pl.pallas_call(
        _moe_kernel, grid_spec=grid_spec,
        out_shape=jax.ShapeDtypeStruct((n_slots, D_MODEL // LANES, LANES), F32),
        compiler_params=_cparams(("arbitrary",)), name="moe",
    )(tile_cls, tile_base, n_tiles, order, wa, wb, hx, gf,
      grouped(w_gate), grouped(w_up), grouped(w_down))


def _unsort_kernel(slot_ref, ys_hbm, y_ref, buf, sem, *, rows, row0):
    i = pl.program_id(0)
    b = i % 2

    def gather(step, dst):
        for j in range(rows):
            pltpu.make_async_copy(ys_hbm.at[slot_ref[row0 + step * rows + j]], buf.at[dst, j],
                                  sem.at[dst]).start()

    @pl.when(i == 0)
    def _():
        gather(0, 0)

    @pl.when(i + 1 < pl.num_programs(0))
    def _():
        gather(i + 1, 1 - b)

    pltpu.make_async_copy(ys_hbm.at[pl.ds(0, rows)], buf.at[b], sem.at[b]).wait()
    y_ref[...] = buf[b].reshape(rows, D_MODEL)


def _unsort(ys, token_slot, *, n, row0, rows):
    grid_spec = pltpu.PrefetchScalarGridSpec(
        num_scalar_prefetch=1, grid=(n // rows,),
        in_specs=[pl.BlockSpec(memory_space=pl.ANY)],
        out_specs=pl.BlockSpec((rows, D_MODEL), lambda i, *_: (i, 0)),
        scratch_shapes=[pltpu.VMEM((2, rows, D_MODEL // LANES, LANES), F32),
                        pltpu.SemaphoreType.DMA((2,))])
    return pl.pallas_call(
        functools.partial(_unsort_kernel, rows=rows, row0=row0), grid_spec=grid_spec,
        out_shape=jax.ShapeDtypeStruct((n, D_MODEL), F32),
        compiler_params=_cparams(("arbitrary",)), name="unsort",
    )(token_slot, ys)


def kernel(x_prompt, x_sample, cache_k, cache_v, state_conv, g_norm_mix, w_in, g_q, g_k, w_conv,
           g_attn_out, g_conv_out, w_out, g_norm_ffn, w_router_group, w_router_expert,
           w_gate, w_up, w_down):
    depth = g_norm_mix.shape[0]
    assert depth == 1
    nb, seq, _ = x_prompt.shape
    db, dec_seq, _ = x_sample.shape
    assert dec_seq == 1
    l = 0

    gmix = g_norm_mix[l][None, :]
    gq = jnp.tile(g_q[l], N_HEADS)[None, :]
    gk = jnp.tile(g_k[l], N_HEADS)[None, :]
    ga = g_attn_out[l][None, :]
    gc = g_conv_out[l][None, :]
    gf = g_norm_ffn[l][None, :]
    w_router = jnp.concatenate(
        [w_router_expert[l], w_router_group[l],
         jnp.zeros((D_MODEL, LANES - N_EXPERTS - N_GROUPS), F32)], axis=1)
    w_router_hi = w_router.astype(BF16)
    w_router_split = jnp.concatenate(
        [w_router_hi, (w_router - w_router_hi.astype(F32)).astype(BF16)], axis=1)

    cos_p, sin_p = _rope_tables(np.arange(seq))
    xp = x_prompt.reshape(nb * seq, D_MODEL)
    q, k, v, bc, tail, kt, vt = _inproj(xp, gmix, w_in[l].astype(BF16), gq, gk, cos_p, sin_p,
                                        w_conv[l], tm=512, seq_len=seq, hi=False,
                                        q_scale=ATTN_SCALE * LOG2_E, n_parts=2)
    attn = _attention(q.reshape(nb, seq, D_ATTN), k.reshape(nb, seq, D_ATTN),
                      v.reshape(nb, seq, D_ATTN))
    n_prompt = nb * seq
    n_total = n_prompt + db
    n_rows = -(-n_total // TOKEN_PAD) * TOKEN_PAD
    hx, route = _outproj(attn.reshape(n_prompt, D_ATTN), bc, xp, ga, gc, w_out[l].astype(BF16),
                         gf, w_router_split, tm=512, hi=False, n_rows=n_rows, row0=0, n_parts=2)
    k_prompt = jnp.transpose(kt, (0, 3, 1, 2))
    v_prompt = jnp.transpose(vt, (0, 3, 1, 2))
    conv_prompt = tail.reshape(nb, SUBLANES, D_CONV)[:, SUBLANES - (CONV_WIDTH - 1):]

    cos_s, sin_s = _rope_tables(np.full((db,), PAST_LEN))
    xs = x_sample.reshape(db, D_MODEL)
    st = state_conv[l]
    qs, ks, vs, bcs, us = _inproj(xs, gmix, w_in[l], gq, gk, cos_s, sin_s, w_conv[l],
                                  tm=db, seq_len=db, hi=True, q_scale=ATTN_SCALE,
                                  prev=(st[:, 1], st[:, 0]))
    heads = lambda t: t.reshape(db, N_HEADS, HEAD_DIM)
    attn_s = _attention_sample(heads(qs), heads(ks), heads(vs),
                               jnp.transpose(cache_k[l], (0, 2, 3, 1)),
                               jnp.transpose(cache_v[l], (0, 2, 3, 1)))
    hx, route = _outproj(attn_s.reshape(db, D_ATTN), bcs, xs, ga, gc, w_out[l], gf, w_router,
                         tm=db, hi=True, n_rows=n_rows, row0=n_prompt, into=(hx, route))

    n_cls = N_GROUPS * EXPERTS_PER_GROUP * (EXPERTS_PER_GROUP - 1) // 2
    max_tiles = -(-(n_total + n_cls * (MOE_TILE - 1)) // MOE_TILE) + N_GROUPS * (MOE_STEP_TILES - 1)
    n_slots = -(-max_tiles // MOE_STEP_TILES) * MOE_STEP_TILES * MOE_TILE
    assert n_slots // MOE_TILE <= N_CLASSES and n_rows % PLAN_ROWS == 0
    plan = _route_plan(route, n_total)
    ys = _moe(hx, gf, w_gate[l], w_up[l], w_down[l], plan, n_slots)
    y_prompt = _unsort(ys, plan[-1], n=n_prompt, row0=0, rows=512).reshape(nb, seq, D_MODEL)
    y_sample = _unsort(ys, plan[-1], n=db, row0=n_prompt, rows=db).reshape(db, 1, D_MODEL)
    k_sample = ks.reshape(db, 1, N_HEADS, HEAD_DIM)
    v_sample = vs.reshape(db, 1, N_HEADS, HEAD_DIM)
    conv_sample = jnp.stack([st[:, 1], us], axis=1)

    return (y_prompt, y_sample, k_prompt[None], v_prompt[None], conv_prompt[None],
            k_sample[None], v_sample[None], conv_sample[None])
```

```python
import functools

import numpy as np
import jax
import jax.numpy as jnp
from jax import lax
from jax.experimental import pallas as pl
from jax.experimental.pallas import tpu as pltpu

F32 = jnp.float32
BF16 = jnp.bfloat16

D_MODEL = 1024
HEAD_DIM = 64
N_HEADS = 12
D_ATTN = N_HEADS * HEAD_DIM
D_CONV = D_MODEL - D_ATTN
D_IN_PROJ = 3 * D_ATTN + 3 * D_CONV
CONV_WIDTH = 3
DILATIONS = (1, 4, 16)
ATT_BLOCK = 128
ATT_UNROLL = 32
REGROUP_STEP = 4
MAX_WINDOW = 2048
PAST_LEN = 8192
ATTN_SCALE = HEAD_DIM ** -0.5
LOG2_E = float(np.log2(np.e))
ROPE_THETA = 10000.0
EPS = 1e-6
N_GROUPS = 4
EXPERTS_PER_GROUP = 8
N_EXPERTS = N_GROUPS * EXPERTS_PER_GROUP
D_EXPERT = 256
MOE_TILE = 192
MOE_STEP_TILES = 2
N_CLASSES = N_GROUPS * EXPERTS_PER_GROUP * EXPERTS_PER_GROUP
PLAN_ROWS = 768
TOKEN_PAD = 512

LANES = 128
SUBLANES = 8
MXU_DIM = 256
TOKEN_ROWS = D_MODEL // LANES
NEG = -1e30
VMEM_LIMIT = 48 * 1024 * 1024


def _cparams(sem):
    return pltpu.CompilerParams(dimension_semantics=sem, vmem_limit_bytes=VMEM_LIMIT)


def _mm(a, b, hi):
    if hi:
        return jnp.dot(a, b, preferred_element_type=F32, precision=lax.Precision.HIGHEST)
    return jnp.dot(a.astype(BF16), b.astype(BF16), preferred_element_type=F32)


def _rms(x, g):
    return x * lax.rsqrt(jnp.mean(x * x, axis=-1, keepdims=True) + EPS) * g


def _rope_tables(pos):
    half = HEAD_DIM // 2
    inv = ROPE_THETA ** (-np.arange(half, dtype=np.float64) / half)
    ang = np.asarray(pos, np.float64)[:, None] * inv[None, :]
    cos, sin = np.cos(ang), np.sin(ang)
    cos2 = np.concatenate([cos, cos, cos, cos], axis=-1)
    sin2 = np.concatenate([-sin, sin, -sin, sin], axis=-1)
    return jnp.asarray(cos2, F32), jnp.asarray(sin2, F32)


def _inproj_kernel(*refs, tm, tiles_per_seq, keep_tiles, hi, given_prev, n_parts, q_scale):
    if given_prev:
        (x_ref, gmix_ref, w_ref, gq_ref, gk_ref, cos_ref, sin_ref, wconv_ref, p1_ref, p2_ref,
         q_ref, k_ref, v_ref, bc_ref, u_ref) = refs
    else:
        (x_ref, gmix_ref, w_ref, gq_ref, gk_ref, cos_ref, sin_ref, wconv_ref,
         q_ref, k_ref, v_ref, bc_ref, tail_ref, kt_ref, vt_ref, ubuf) = refs
    part_rows = tm // n_parts

    r_i = lax.broadcasted_iota(jnp.int32, (MXU_DIM, MXU_DIM), 0) // HEAD_DIM
    c_i = lax.broadcasted_iota(jnp.int32, (MXU_DIM, MXU_DIM), 1) // HEAD_DIM
    head_sum = jnp.where(r_i == c_i, 1.0, 0.0).astype(F32 if hi else BF16)
    lane = lax.broadcasted_iota(jnp.int32, (1, D_ATTN), 1)
    first_half = (lane % HEAD_DIM) < (HEAD_DIM // 2)
    wc = wconv_ref[...]

    def head_ssq(z):
        sq = z * z
        return jnp.concatenate(
            [_mm(sq[:, c * MXU_DIM:(c + 1) * MXU_DIM], head_sum, hi)
             for c in range(D_ATTN // MXU_DIM)], axis=-1)

    def norm_rope(z, ssq, g_ref, cos, sin):
        zn = z * lax.rsqrt(ssq * (1.0 / HEAD_DIM) + EPS) * g_ref[...]
        partner = jnp.where(first_half,
                            pltpu.roll(zn, D_ATTN - HEAD_DIM // 2, 1),
                            pltpu.roll(zn, HEAD_DIM // 2, 1))
        return zn * cos + partner * sin

    def conv_gate(b_gate, u2, u1, u):
        return b_gate * (wc[0:1, :] * u2 + wc[1:2, :] * u1 + wc[2:3, :] * u)

    def part(p, transposed):
        rows = pl.ds(p * part_rows, part_rows)
        xn = _rms(x_ref[rows, :], gmix_ref[...])
        if not hi:
            xn = xn.astype(BF16)
        zq = _mm(xn, w_ref[:, 0:D_ATTN], hi)
        zk = _mm(xn, w_ref[:, D_ATTN:2 * D_ATTN], hi)
        ssq_q = head_ssq(zq)
        v = _mm(xn, w_ref[:, 2 * D_ATTN:3 * D_ATTN], hi)
        ssq_k = head_ssq(zk)
        c0 = 3 * D_ATTN
        b_gate = _mm(xn, w_ref[:, c0:c0 + D_CONV], hi)
        c_gate = _mm(xn, w_ref[:, c0 + D_CONV:c0 + 2 * D_CONV], hi)
        u_in = _mm(xn, w_ref[:, c0 + 2 * D_CONV:c0 + 3 * D_CONV], hi)
        yield
        reps = D_ATTN // LANES
        cos = jnp.concatenate([cos_ref[rows, :]] * reps, axis=-1)
        sin = jnp.concatenate([sin_ref[rows, :]] * reps, axis=-1)
        q_ref[rows, :] = norm_rope(zq, ssq_q, gq_ref, cos, sin) * q_scale
        k = norm_rope(zk, ssq_k, gk_ref, cos, sin)
        k_ref[rows, :] = k
        v_ref[rows, :] = v
        if transposed:
            cols = slice(p * part_rows, (p + 1) * part_rows)
            kt_ref[0, :, :, cols] = k.T.reshape(N_HEADS, HEAD_DIM, part_rows)
            vt_ref[0, :, :, cols] = v.T.reshape(N_HEADS, HEAD_DIM, part_rows)
        u = c_gate * u_in
        if given_prev:
            u_ref[rows, :] = u
            bc_ref[rows, :] = conv_gate(b_gate, p2_ref[rows, :], p1_ref[rows, :], u)
        else:
            ubuf[pl.ds(SUBLANES + p * part_rows, part_rows), :] = u
        return b_gate

    def body(transposed):
        stages = [part(p, transposed) for p in range(n_parts)]
        for stage in stages:
            next(stage)
        gates = []
        for stage in stages:
            try:
                next(stage)
            except StopIteration as done:
                gates.append(done.value)
        if not given_prev:
            u = ubuf[SUBLANES:tm + SUBLANES, :]
            u1 = ubuf[SUBLANES - 1:tm + SUBLANES - 1, :]
            u2 = ubuf[SUBLANES - 2:tm + SUBLANES - 2, :]
            bc_ref[...] = conv_gate(jnp.concatenate(gates, axis=0), u2, u1, u)
            tail_ref[...] = ubuf[tm:tm + SUBLANES, :]

    if given_prev:
        body(False)
        return

    i = pl.program_id(0)

    @pl.when(i % tiles_per_seq == 0)
    def _():
        ubuf[0:SUBLANES, :] = jnp.zeros((SUBLANES, D_CONV), F32)

    @pl.when(i % tiles_per_seq != 0)
    def _():
        ubuf[0:SUBLANES, :] = ubuf[tm:tm + SUBLANES, :]

    kept = i % tiles_per_seq >= tiles_per_seq - keep_tiles
    pl.when(kept)(functools.partial(body, True))
    pl.when(jnp.logical_not(kept))(functools.partial(body, False))


def _inproj(x, gmix, w_in, gq, gk, cos, sin, w_conv, *, tm, seq_len, hi, q_scale, n_parts=1,
            prev=None):
    n = x.shape[0]
    nt = n // tm
    tiles_per_seq = seq_len // tm
    row = lambda w: pl.BlockSpec((tm, w), lambda i: (i, 0))
    full = lambda a: pl.BlockSpec(a.shape, lambda i: (0,) * a.ndim)
    tab = pl.BlockSpec((tm, LANES), lambda i: (i % tiles_per_seq, 0))
    once = pl.BlockSpec(w_in.shape, lambda i: (0, 0), pipeline_mode=pl.Buffered(1))
    in_specs = [row(D_MODEL), full(gmix), once, full(gq), full(gk), tab, tab, full(w_conv)]
    args = [x, gmix, w_in, gq, gk, cos, sin, w_conv]
    out_shape = [jax.ShapeDtypeStruct((n, D_ATTN), F32)] * 3 + [jax.ShapeDtypeStruct((n, D_CONV), F32)]
    out_specs = [row(D_ATTN)] * 3 + [row(D_CONV)]
    scratch = []
    keep_tiles = min(MAX_WINDOW, seq_len) // tm
    if prev is not None:
        in_specs += [row(D_CONV), row(D_CONV)]
        args += list(prev)
        out_shape.append(jax.ShapeDtypeStruct((n, D_CONV), F32))
        out_specs.append(row(D_CONV))
    else:
        n_seq = n // seq_len
        out_shape.append(jax.ShapeDtypeStruct((n_seq * SUBLANES, D_CONV), F32))
        out_specs.append(pl.BlockSpec((SUBLANES, D_CONV), lambda i: (i // tiles_per_seq, 0)))
        kept = pl.BlockSpec(
            (1, N_HEADS, HEAD_DIM, tm),
            lambda i: (i // tiles_per_seq, 0, 0,
                       jnp.maximum(i % tiles_per_seq - (tiles_per_seq - keep_tiles), 0)))
        out_shape += [jax.ShapeDtypeStruct((n_seq, N_HEADS, HEAD_DIM, keep_tiles * tm), F32)] * 2
        out_specs += [kept, kept]
        scratch.append(pltpu.VMEM((tm + SUBLANES, D_CONV), F32))
    return pl.pallas_call(
        functools.partial(_inproj_kernel, tm=tm, tiles_per_seq=tiles_per_seq,
                          keep_tiles=keep_tiles, hi=hi, given_prev=prev is not None,
                          n_parts=n_parts, q_scale=q_scale),
        grid=(nt,), in_specs=in_specs, out_specs=out_specs, out_shape=out_shape,
        scratch_shapes=scratch, compiler_params=_cparams(("arbitrary",)),
        name="inproj_hi" if hi else "inproj",
    )(*args)


def _attn_kernel(q_ref, k_ref, v_ref, o_ref, num_ref, den_ref, m_ref, bias_ref,
                 qd_ref, kd_ref, vd_ref, numd_ref, dend_ref, md_ref, tmp_ref, *, seq_len):
    blk = ATT_BLOCK
    qi = lax.broadcasted_iota(jnp.int32, (blk, 2 * blk), 0)
    kj = lax.broadcasted_iota(jnp.int32, (blk, 2 * blk), 1)
    bias_ref[...] = jnp.where((kj >= qi) & (kj <= qi + blk), 0.0, NEG)
    head0 = lax.broadcasted_iota(jnp.int32, (1, LANES), 1) < HEAD_DIM
    dn = (((1,), (1,)), ((), ()))

    def rows(start, size, d):
        return pl.ds(start, size) if d == 1 else pl.ds(start, size, stride=d)

    natural = (lambda at: q_ref[0, at, :], lambda at: k_ref[0, at, :], lambda at: v_ref[0, at, :],
               (num_ref, den_ref, m_ref))
    regrouped = (lambda at: qd_ref[at, :], lambda at: kd_ref[at, :], lambda at: vd_ref[at, :],
                 (numd_ref, dend_ref, md_ref))

    def group(src, d, r, n0, u, first, mode):
        q_at, k_at, v_at, (num_acc, den_acc, m_acc) = src
        q_rows = rows(n0 * (blk * d) + r, u * blk, d)
        if first:
            k_rows = rows(r, u * blk, d)
        else:
            k_rows = rows((n0 - 1) * (blk * d) + r, (u + 1) * blk, d)
        qa = q_at(q_rows)
        ka = k_at(k_rows).astype(BF16)
        va = v_at(k_rows)
        qa0 = jnp.where(head0, qa, 0.0).astype(BF16)
        qa1 = jnp.where(head0, 0.0, qa).astype(BF16)
        va0 = jnp.where(head0, va, 1.0).astype(BF16)
        va1 = jnp.where(head0, 1.0, va).astype(BF16)
        if mode != "init":
            m_old = m_acc[q_rows, :]
            num_old = num_acc[q_rows, :]
            den_old = den_acc[q_rows, :]
        key_slices, scores = [], []
        for i in range(u):
            qs = slice(i * blk, (i + 1) * blk)
            if first and i == 0:
                ks = slice(0, blk)
                bias = bias_ref[:, blk:2 * blk]
            else:
                k0 = (i - 1) * blk if first else i * blk
                ks = slice(k0, k0 + 2 * blk)
                bias = bias_ref[...]
            key_slices.append(ks)
            scores.append(
                (lax.dot_general(qa0[qs], ka[ks], dn, preferred_element_type=F32) + bias,
                 lax.dot_general(qa1[qs], ka[ks], dn, preferred_element_type=F32) + bias))
        yield
        probs, ms = [], []
        for s0, s1 in scores:
            m0 = jnp.max(s0, axis=-1, keepdims=True)
            m1 = jnp.max(s1, axis=-1, keepdims=True)
            probs.append((jnp.exp2(s0 - m0).astype(BF16), jnp.exp2(s1 - m1).astype(BF16)))
            ms.append(jnp.where(head0, m0, m1))
        yield
        nums, dens = [], []
        for (p0, p1), ks in zip(probs, key_slices):
            r0 = jnp.dot(p0, va0[ks], preferred_element_type=F32)
            r1 = jnp.dot(p1, va1[ks], preferred_element_type=F32)
            nums.append(jnp.where(head0, r0, r1))
            dens.append(pltpu.roll(jnp.where(head0, r1, r0), HEAD_DIM, 1))
        num = jnp.concatenate(nums, axis=0)
        den = jnp.concatenate(dens, axis=0)
        m = jnp.concatenate(ms, axis=0)
        if mode == "init":
            return [(num_acc, q_rows, num), (den_acc, q_rows, den), (m_acc, q_rows, m)]
        m_new = jnp.maximum(m_old, m)
        a = jnp.exp2(m_old - m_new)
        b = jnp.exp2(m - m_new)
        num = num_old * a + num * b
        den = den_old * a + den * b
        if mode == "merge":
            return [(num_acc, q_rows, num), (den_acc, q_rows, den), (m_acc, q_rows, m_new)]
        return [(o_ref.at[0], q_rows, num / den)]

    def run(groups):
        stages = [group(*g) for g in groups]
        for _ in range(2):
            for stage in stages:
                next(stage)
        stores = []
        for stage in stages:
            try:
                next(stage)
            except StopIteration as done:
                stores += done.value
        for ref, at, val in stores:
            ref[at, :] = val

    def branch(src, d, n_seq, sub_len, seq_stride, mode):
        nb = sub_len // blk
        u = min(ATT_UNROLL, nb)
        if nb == u:
            per_body = ATT_UNROLL // u

            def several(rr, carry):
                run([(src, d, (rr * per_body + t) * seq_stride, 0, u, True, mode)
                     for t in range(per_body)])
                return carry

            lax.fori_loop(0, n_seq // per_body, several, 0)
        else:
            def one(r, carry):
                run([(src, d, r * seq_stride, 0, u, True, mode)])

                def later_blocks(g, c):
                    run([(src, d, r * seq_stride, g * u, u, False, mode)])
                    return c

                lax.fori_loop(1, nb // u, later_blocks, 0)
                return carry

            lax.fori_loop(0, n_seq, one, 0)

    step = REGROUP_STEP
    wide = step * step
    assert DILATIONS == (1, step, wide)
    part = seq_len // wide

    def regroup(load, dst_ref):
        for a in range(step):
            tmp_ref[a * step * part:(a + 1) * step * part, :] = load(rows(a, step * part, step))
        for a in range(step):
            for b in range(step):
                s = a * step + b
                dst_ref[s * part:(s + 1) * part, :] = tmp_ref[rows(a * step * part + b, part, step), :]

    def ungroup(src_ref, dst_ref):
        for a in range(step):
            for b in range(step):
                s = a * step + b
                tmp_ref[rows(a * step * part + b, part, step), :] = src_ref[s * part:(s + 1) * part, :]
        for a in range(step):
            dst_ref[rows(a, step * part, step), :] = tmp_ref[a * step * part:(a + 1) * step * part, :]

    for load, dst in zip(natural[:3], (qd_ref, kd_ref, vd_ref)):
        regroup(load, dst)
    branch(regrouped, 1, wide, part, part, "init")
    for src, dst in zip(regrouped[3], natural[3]):
        ungroup(src, dst)
    branch(natural, 1, 1, seq_len, 0, "merge")
    branch(natural, step, step, seq_len // step, 1, "final")


def _attention(q, k, v):
    b, s, _ = q.shape
    spec = pl.BlockSpec((1, s, LANES), lambda i, p: (i, 0, p))
    return pl.pallas_call(
        functools.partial(_attn_kernel, seq_len=s),
        grid=(b, D_ATTN // LANES), in_specs=[spec] * 3, out_specs=spec,
        out_shape=jax.ShapeDtypeStruct((b, s, D_ATTN), F32),
        scratch_shapes=[pltpu.VMEM((s, LANES), F32)] * 3 + [pltpu.VMEM((ATT_BLOCK, 2 * ATT_BLOCK), F32)]
        + [pltpu.VMEM((s, LANES), F32)] * 7,
        compiler_params=_cparams(("arbitrary", "arbitrary")), name="attn_prompt",
    )(q, k, v)


def _attn_sample_kernel(q_ref, kn_ref, vn_ref, kc_ref, vc_ref, o_ref, *, w_buf, heads):
    age = w_buf - lax.broadcasted_iota(jnp.int32, (1, w_buf), 1)
    valid = [(age % d == 0) & (age <= d * ATT_BLOCK) for d in DILATIONS]
    diag = (lax.broadcasted_iota(jnp.int32, (HEAD_DIM, HEAD_DIM), 0)
            == lax.broadcasted_iota(jnp.int32, (HEAD_DIM, HEAD_DIM), 1))
    to_col = lambda row: jnp.sum(jnp.where(diag, row, 0.0), axis=1, keepdims=True)
    to_row = lambda col: jnp.sum(jnp.where(diag, col, 0.0), axis=0, keepdims=True)
    q_rows, kn_rows, vn_rows = q_ref[0], kn_ref[0], vn_ref[0]
    s_self_all = jnp.sum(kn_rows * q_rows, axis=1, keepdims=True)
    for h in range(heads):
        q = to_col(q_rows[h:h + 1])
        vn = to_col(vn_rows[h:h + 1])
        s = jnp.sum(kc_ref[0, h] * q, axis=0, keepdims=True)
        s_self = s_self_all[h:h + 1]
        parts = []
        for ok in valid:
            sd = jnp.where(ok, s, NEG)
            m = jnp.maximum(jnp.max(sd, axis=1, keepdims=True), s_self)
            p = jnp.exp(sd - m)
            p_self = jnp.exp(s_self - m)
            parts.append((p, p_self, jnp.sum(p, axis=1, keepdims=True) + p_self, m))
        m_all = jnp.maximum(jnp.maximum(parts[0][3], parts[1][3]), parts[2][3])
        w = jnp.zeros((1, w_buf), F32)
        w_self = jnp.zeros((1, 1), F32)
        den_t = jnp.zeros((1, 1), F32)
        for p, p_self, den, m in parts:
            c = jnp.exp(m - m_all)
            w = w + p * c
            w_self = w_self + p_self * c
            den_t = den_t + den * c
        num_t = jnp.sum(vc_ref[0, h] * w, axis=1, keepdims=True) + vn * w_self
        o_ref[0, h:h + 1, :] = to_row(num_t / den_t)


def _attention_sample(q, kn, vn, cache_kt, cache_vt):
    db, heads, _, w_buf = cache_kt.shape
    tok = pl.BlockSpec((1, heads, HEAD_DIM), lambda i: (i, 0, 0))
    cache = pl.BlockSpec((1, heads, HEAD_DIM, w_buf), lambda i: (i, 0, 0, 0))
    return pl.pallas_call(
        functools.partial(_attn_sample_kernel, w_buf=w_buf, heads=heads),
        grid=(db,), in_specs=[tok, tok, tok, cache, cache], out_specs=tok,
        out_shape=jax.ShapeDtypeStruct((db, heads, HEAD_DIM), F32),
        compiler_params=_cparams(("arbitrary",)), name="attn_sample",
    )(q, kn, vn, cache_kt, cache_vt)


def _outproj_kernel(attn_ref, bc_ref, x_ref, ga_ref, gc_ref, wo_ref, gf_ref, wr_ref, *rest,
                    hi, n_tiles, n_parts):
    hx_ref, route_ref = rest[-2:]
    part_rows = x_ref.shape[0] // n_parts

    def part(k):
        rows = pl.ds(k * part_rows, part_rows)
        a = _rms(attn_ref[rows, :], ga_ref[...])
        c = _rms(bc_ref[rows, :], gc_ref[...])
        mix = _mm(a, wo_ref[0:D_ATTN, :], hi) + _mm(c, wo_ref[D_ATTN:D_MODEL, :], hi)
        yield
        h = x_ref[rows, :] + mix
        hn = _rms(h, gf_ref[...])
        if hi:
            lg = _mm(hn, wr_ref[...], True)
        else:
            hn_hi = hn.astype(BF16)
            hn_lo = (hn - hn_hi.astype(F32)).astype(BF16)
            both = jnp.dot(hn_hi, wr_ref[...], preferred_element_type=F32)
            lg = (both[:, :LANES] + both[:, LANES:]
                  + jnp.dot(hn_lo, wr_ref[:, :LANES], preferred_element_type=F32))
        yield
        lane_i = lax.broadcasted_iota(jnp.int32, lg.shape, 1)
        lane = lane_i.astype(F32)
        lane_group = (lane_i // EXPERTS_PER_GROUP).astype(F32)
        is_e = lane_i < N_EXPERTS
        is_g = (lane_i >= N_EXPERTS) & (lane_i < N_EXPERTS + N_GROUPS)
        big = float(LANES)
        first_at = lambda t, v: jnp.min(jnp.where(t == v, lane, big), axis=-1, keepdims=True)

        gl = jnp.where(is_g, lg, NEG)
        mg = jnp.max(gl, axis=-1, keepdims=True)
        p_g = 1.0 / jnp.sum(jnp.exp(gl - mg), axis=-1, keepdims=True)
        g_idx = first_at(gl, mg) - float(N_EXPERTS)
        el = jnp.where(is_e & (lane_group == g_idx), lg, NEG)
        v1 = jnp.max(el, axis=-1, keepdims=True)
        i1 = first_at(el, v1)
        el2 = jnp.where(lane == i1, NEG, el)
        v2 = jnp.max(el2, axis=-1, keepdims=True)
        i2 = first_at(el2, v2)
        e2 = jnp.exp(v2 - v1)
        w1 = p_g / (1.0 + e2)
        w2 = p_g * e2 / (1.0 + e2)
        route_ref[rows, :] = sum(jnp.where(lane_i == j, val, 0.0)
                                 for j, val in enumerate([i1, i2, w1, w2]))
        hx_ref[rows, :, :] = h.reshape(part_rows, TOKEN_ROWS, LANES)

    stages = [part(k) for k in range(n_parts)]
    for _ in range(2):
        for stage in stages:
            next(stage)
    for stage in stages:
        for _ in stage:
            pass

    @pl.when(pl.program_id(0) >= n_tiles)
    def _():
        hx_ref[...] = jnp.zeros(hx_ref.shape, F32)
        route_ref[...] = jnp.zeros(route_ref.shape, F32)


def _outproj(attn, bc, x, ga, gc, w_out, gf, w_router, *, tm, hi, n_rows, row0, n_parts=1,
             into=None):
    n = x.shape[0]
    n_tiles = n // tm
    n_fill = 0 if into is not None else -(-(n_rows - n) // tm)
    row = lambda w: pl.BlockSpec((tm, w), lambda i: (jnp.minimum(i, n_tiles - 1), 0))
    full = lambda a: pl.BlockSpec(a.shape, lambda i: (0,) * a.ndim)
    in_specs = [row(D_ATTN), row(D_CONV), row(D_MODEL), full(ga), full(gc), full(w_out),
                full(gf), full(w_router)]
    args = [attn, bc, x, ga, gc, w_out, gf, w_router]
    aliases = {}
    if into is not None:
        in_specs += [pl.BlockSpec(memory_space=pl.ANY)] * 2
        args += list(into)
        aliases = {len(args) - 2: 0, len(args) - 1: 1}
    return pl.pallas_call(
        functools.partial(_outproj_kernel, hi=hi, n_tiles=n_tiles, n_parts=n_parts),
        grid=(n_tiles + n_fill,), in_specs=in_specs,
        out_specs=[pl.BlockSpec((tm, TOKEN_ROWS, LANES), lambda i: (i + row0 // tm, 0, 0)),
                   pl.BlockSpec((tm, LANES), lambda i: (i + row0 // tm, 0))],
        out_shape=[jax.ShapeDtypeStruct((n_rows, TOKEN_ROWS, LANES), F32),
                   jax.ShapeDtypeStruct((n_rows, LANES), F32)],
        input_output_aliases=aliases,
        compiler_params=_cparams(("arbitrary",)), name="outproj_hi" if hi else "outproj",
    )(*args)


def _plan_kernel(route_ref, slot_ref, pos_ref, wa_ref, wb_ref, tiles_ref,
                 run_ref, cls_s, rank_s, later_s, *, n_steps, n_tokens):
    i = pl.program_id(0)
    rows = route_ref.shape[0]
    e = float(EXPERTS_PER_GROUP)

    @pl.when(i == 0)
    def _():
        run_ref[...] = jnp.zeros(run_ref.shape, F32)
        later_s[...] = jnp.where(lax.broadcasted_iota(jnp.int32, (rows, rows), 0)
                                 < lax.broadcasted_iota(jnp.int32, (rows, rows), 1),
                                 1.0, 0.0).astype(BF16)

    r = route_ref[...].T
    i1, i2, w1, w2 = r[0:1], r[1:2], r[2:3], r[3:4]
    grp = jnp.floor(i1 * (1.0 / e))
    cls = grp * (e * e) + (jnp.minimum(i1, i2) - grp * e) * e + (jnp.maximum(i1, i2) - grp * e)
    token = i * rows + lax.broadcasted_iota(jnp.int32, (1, rows), 1)
    class_id = lax.broadcasted_iota(jnp.int32, (N_CLASSES, rows), 0).astype(F32)
    own = (class_id == cls) & (token < n_tokens)
    before = jnp.dot(jnp.where(own, 1.0, 0.0).astype(BF16), later_s[...],
                     preferred_element_type=F32)
    rank = jnp.sum(jnp.where(own, before + run_ref[...], 0.0), axis=0, keepdims=True)
    run_ref[...] += jnp.sum(jnp.where(own, 1.0, 0.0), axis=1, keepdims=True)
    first_is_a = i1 < i2
    lines = ((cls_s, cls), (rank_s, rank),
             (wa_ref, jnp.where(first_is_a, w1, w2)), (wb_ref, jnp.where(first_is_a, w2, w1)))
    for c in range(rows // LANES):
        for ref, line in lines:
            ref[pl.ds(i * (rows // LANES) + c, 1), :] = line[:, c * LANES:(c + 1) * LANES]

    @pl.when(i == n_steps - 1)
    def _():
        counts = run_ref[...]
        tiles = jnp.floor((counts + (MOE_TILE - 1.0)) * (1.0 / MOE_TILE))
        cid = lax.broadcasted_iota(jnp.int32, (N_CLASSES, 1), 0)
        per_group = EXPERTS_PER_GROUP * EXPERTS_PER_GROUP
        for g in range(N_GROUPS):
            g_tiles = jnp.sum(jnp.where(cid // per_group == g, tiles, 0.0), axis=0, keepdims=True)
            spare = jnp.ceil(g_tiles * (1.0 / MOE_STEP_TILES)) * MOE_STEP_TILES - g_tiles
            tiles = tiles + jnp.where(cid == (g + 1) * per_group - 1, spare, 0.0)
        upto = jnp.where(lax.broadcasted_iota(jnp.int32, (N_CLASSES, N_CLASSES), 0)
                         >= lax.broadcasted_iota(jnp.int32, (N_CLASSES, N_CLASSES), 1), 1.0, 0.0)
        lane = lax.broadcasted_iota(jnp.int32, (N_CLASSES, LANES), 1)
        both = jnp.where(lane == 0, tiles, 0.0) + jnp.where(lane == 1, counts, 0.0)
        ends = _mm(upto, both, True)
        tile_end, tok_end = ends[:, 0:1], ends[:, 1:2]
        tile_off, tok_off = tile_end - tiles, tok_end - counts
        n_tiles = jnp.max(tile_end, axis=0, keepdims=True)
        t = lax.broadcasted_iota(jnp.int32, (1, N_CLASSES), 1).astype(F32)
        t = jnp.minimum(t, n_tiles - 1.0)
        cls_t = jnp.sum(jnp.where(tile_end <= t, 1.0, 0.0), axis=0, keepdims=True)
        mine = lax.broadcasted_iota(jnp.int32, (N_CLASSES, N_CLASSES), 0).astype(F32) == cls_t
        pick = lambda v: jnp.sum(jnp.where(mine, v, 0.0), axis=0, keepdims=True)
        base = pick(tok_off) + (t - pick(tile_off)) * MOE_TILE
        row8 = lax.broadcasted_iota(jnp.int32, (SUBLANES, N_CLASSES), 0)
        tiles_ref[...] = sum(jnp.where(row8 == j, v, 0.0)
                             for j, v in enumerate([cls_t, base, n_tiles + jnp.zeros_like(t)]))

        token_class = lax.broadcasted_iota(jnp.int32, (N_CLASSES, LANES), 0).astype(F32)

        def per_block(j, carry):
            at = pl.ds(j, 1)
            hit = token_class == cls_s[at, :]
            look = lambda col: jnp.sum(jnp.where(hit, col, 0.0), axis=0, keepdims=True)
            rank_row = rank_s[at, :]
            slot_ref[at, :] = (look(tile_off) * MOE_TILE + rank_row).astype(jnp.int32)
            pos_ref[at, :] = (look(tok_off) + rank_row).astype(jnp.int32)
            return carry

        lax.fori_loop(0, slot_ref.shape[0], per_block, 0)


def _plan_order_kernel(pos_ref, order_ref, *, n_tokens):
    def place(t, carry):
        order_ref[pos_ref[t]] = t
        return carry

    lax.fori_loop(0, n_tokens, place, 0, unroll=32)
    for j in range(MOE_TILE):
        order_ref[n_tokens + j] = 0


def _route_plan(route, n_tokens):
    n_rows = route.shape[0]
    rows = PLAN_ROWS
    n_steps = n_rows // rows
    dense = lambda dtype: jax.ShapeDtypeStruct((n_rows // LANES, LANES), dtype)
    whole = lambda r, c: pl.BlockSpec((r, c), lambda i: (0, 0))
    slot, pos, wa, wb, tiles = pl.pallas_call(
        functools.partial(_plan_kernel, n_steps=n_steps, n_tokens=n_tokens),
        grid=(n_steps,), in_specs=[pl.BlockSpec((rows, LANES), lambda i: (i, 0))],
        out_specs=[whole(n_rows // LANES, LANES)] * 4 + [whole(SUBLANES, N_CLASSES)],
        out_shape=[dense(jnp.int32), dense(jnp.int32), dense(F32), dense(F32),
                   jax.ShapeDtypeStruct((SUBLANES, N_CLASSES), F32)],
        scratch_shapes=[pltpu.VMEM((N_CLASSES, 1), F32),
                        pltpu.VMEM((n_rows // LANES, LANES), F32),
                        pltpu.VMEM((n_rows // LANES, LANES), F32),
                        pltpu.VMEM((rows, rows), BF16)],
        compiler_params=_cparams(("arbitrary",)), name="plan",
    )(route)
    order = pl.pallas_call(
        functools.partial(_plan_order_kernel, n_tokens=n_tokens),
        in_specs=[pl.BlockSpec(memory_space=pltpu.SMEM)],
        out_specs=pl.BlockSpec(memory_space=pltpu.SMEM),
        out_shape=jax.ShapeDtypeStruct((n_tokens + MOE_TILE,), jnp.int32), name="plan_order",
    )(pos.reshape(-1))
    tile_cls = tiles[0].astype(jnp.int32)
    tile_base = tiles[1].astype(jnp.int32)
    n_tiles = tiles[2, 0:1].astype(jnp.int32)
    return tile_cls, tile_base, n_tiles, order, wa.reshape(-1), wb.reshape(-1), slot.reshape(-1)


def _moe_kernel(tcls_ref, tbase_ref, nt_ref, order_ref, wa_ref, wb_ref,
                hx_hbm, gf_ref, wg_ref, wu_ref, wd_ref, y_ref,
                xbuf, wgu_s, wd_s, sem):
    s = pl.program_id(0)
    n_tiles = nt_ref[0]
    pair = EXPERTS_PER_GROUP * EXPERTS_PER_GROUP
    n_buf = xbuf.shape[0]
    first_tile = s * MOE_STEP_TILES

    def gather(tile):
        base = tbase_ref[tile]
        for j in range(MOE_TILE):
            pltpu.make_async_copy(hx_hbm.at[order_ref[base + j]], xbuf.at[tile % n_buf, j],
                                  sem.at[tile % n_buf]).start()

    for k in range(MOE_STEP_TILES):
        @pl.when((s == 0) & (k < n_tiles))
        def _(k=k):
            gather(k)

    for k in range(MOE_STEP_TILES):
        @pl.when(first_tile + MOE_STEP_TILES + k < n_tiles)
        def _(k=k):
            gather(first_tile + MOE_STEP_TILES + k)

    @pl.when(first_tile < n_tiles)
    def _():
        grp = tcls_ref[first_tile] // pair

        @pl.when((s == 0) | (grp != tcls_ref[jnp.maximum(first_tile - 1, 0)] // pair))
        def _():
            for e in range(EXPERTS_PER_GROUP):
                wgu_s[e, :, 0:D_EXPERT] = wg_ref[0, e].astype(BF16)
                wgu_s[e, :, D_EXPERT:2 * D_EXPERT] = wu_ref[0, e].astype(BF16)
                wd_s[e] = wd_ref[0, e].astype(BF16)

    def run_tile(tile, out_rows):
        cls = tcls_ref[tile]
        slot = tile % n_buf
        pltpu.make_async_copy(hx_hbm.at[pl.ds(0, MOE_TILE)], xbuf.at[slot], sem.at[slot]).wait()
        h = xbuf[slot].reshape(MOE_TILE, D_MODEL)
        xn = _rms(h, gf_ref[...]).astype(BF16)
        ea = (cls % pair) // EXPERTS_PER_GROUP
        eb = cls % EXPERTS_PER_GROUP

        def column(w_ref):
            base = tbase_ref[tile]
            sub = lax.broadcasted_iota(jnp.int32, (SUBLANES, 1), 0)
            chunks = []
            for c in range(MOE_TILE // SUBLANES):
                acc = jnp.zeros((SUBLANES, 1), F32)
                for k in range(SUBLANES):
                    acc = jnp.where(sub == k, w_ref[order_ref[base + c * SUBLANES + k]], acc)
                chunks.append(acc)
            return jnp.concatenate(chunks, axis=0)

        gus = [jnp.dot(xn, wgu_s[e], preferred_element_type=F32) for e in (ea, eb)]
        acts = [(jax.nn.silu(gu[:, 0:D_EXPERT]) * gu[:, D_EXPERT:2 * D_EXPERT]).astype(BF16)
                for gu in gus]
        oa, ob = [jnp.dot(act, wd_s[e], preferred_element_type=F32)
                  for act, e in zip(acts, (ea, eb))]
        y = h + column(wa_ref) * oa + column(wb_ref) * ob
        y_ref[out_rows, :, :] = y.reshape(MOE_TILE, D_MODEL // LANES, LANES)

    for k in range(MOE_STEP_TILES):
        out_rows = pl.ds(k * MOE_TILE, MOE_TILE)
        pl.when(first_tile + k < n_tiles)(functools.partial(run_tile, first_tile + k, out_rows))

        @pl.when(first_tile + k >= n_tiles)
        def _(out_rows=out_rows):
            y_ref[out_rows, :, :] = jnp.zeros((MOE_TILE, D_MODEL // LANES, LANES), F32)


def _moe(hx, gf, w_gate, w_up, w_down, plan, n_slots):
    tile_cls, tile_base, n_tiles, order, wa, wb, _ = plan
    pair = EXPERTS_PER_GROUP * EXPERTS_PER_GROUP
    assert n_slots % (MOE_STEP_TILES * MOE_TILE) == 0 and n_slots // MOE_TILE <= tile_cls.shape[0]
    grouped = lambda w: w.reshape(N_GROUPS, EXPERTS_PER_GROUP, *w.shape[1:])
    wspec = lambda r, c: pl.BlockSpec((1, EXPERTS_PER_GROUP, r, c),
                                      lambda s, tc, *_: (tc[s * MOE_STEP_TILES] // pair, 0, 0, 0),
                                      pipeline_mode=pl.Buffered(1))
    grid_spec = pltpu.PrefetchScalarGridSpec(
        num_scalar_prefetch=6, grid=(n_slots // (MOE_STEP_TILES * MOE_TILE),),
        in_specs=[pl.BlockSpec(memory_space=pl.ANY),
                  pl.BlockSpec(gf.shape, lambda t, *_: (0, 0)),
                  wspec(D_MODEL, D_EXPERT), wspec(D_MODEL, D_EXPERT), wspec(D_EXPERT, D_MODEL)],
        out_specs=pl.BlockSpec((MOE_STEP_TILES * MOE_TILE, D_MODEL // LANES, LANES),
                               lambda s, *_: (s, 0, 0)),
        scratch_shapes=[pltpu.VMEM((2 * MOE_STEP_TILES, MOE_TILE, TOKEN_ROWS, LANES), F32),
                        pltpu.VMEM((EXPERTS_PER_GROUP, D_MODEL, 2 * D_EXPERT), BF16),
                        pltpu.VMEM((EXPERTS_PER_GROUP, D_EXPERT, D_MODEL), BF16),
                        pltpu.SemaphoreType.DMA((2 * MOE_STEP_TILES,))])
    return pl.pallas_call(
        _moe_kernel, grid_spec=grid_spec,
        out_shape=jax.ShapeDtypeStruct((n_slots, D_MODEL // LANES, LANES), F32),
        compiler_params=_cparams(("arbitrary",)), name="moe",
    )(tile_cls, tile_base, n_tiles, order, wa, wb, hx, gf,
      grouped(w_gate), grouped(w_up), grouped(w_down))


def _unsort_kernel(slot_ref, ys_hbm, y_ref, buf, sem, *, rows, row0):
    i = pl.program_id(0)
    b = i % 2

    def gather(step, dst):
        for j in range(rows):
            pltpu.make_async_copy(ys_hbm.at[slot_ref[row0 + step * rows + j]], buf.at[dst, j],
                                  sem.at[dst]).start()

    @pl.when(i == 0)
    def _():
        gather(0, 0)

    @pl.when(i + 1 < pl.num_programs(0))
    def _():
        gather(i + 1, 1 - b)

    pltpu.make_async_copy(ys_hbm.at[pl.ds(0, rows)], buf.at[b], sem.at[b]).wait()
    y_ref[...] = buf[b].reshape(rows, D_MODEL)


def _unsort(ys, token_slot, *, n, row0, rows):
    grid_spec = pltpu.PrefetchScalarGridSpec(
        num_scalar_prefetch=1, grid=(n // rows,),
        in_specs=[pl.BlockSpec(memory_space=pl.ANY)],
        out_specs=pl.BlockSpec((rows, D_MODEL), lambda i, *_: (i, 0)),
        scratch_shapes=[pltpu.VMEM((2, rows, D_MODEL // LANES, LANES), F32),
                        pltpu.SemaphoreType.DMA((2,))])
    return pl.pallas_call(
        functools.partial(_unsort_kernel, rows=rows, row0=row0), grid_spec=grid_spec,
        out_shape=jax.ShapeDtypeStruct((n, D_MODEL), F32),
        compiler_params=_cparams(("arbitrary",)), name="unsort",
    )(token_slot, ys)


def kernel(x_prompt, x_sample, cache_k, cache_v, state_conv, g_norm_mix, w_in, g_q, g_k, w_conv,
           g_attn_out, g_conv_out, w_out, g_norm_ffn, w_router_group, w_router_expert,
           w_gate, w_up, w_down):
    depth = g_norm_mix.shape[0]
    assert depth == 1
    nb, seq, _ = x_prompt.shape
    db, dec_seq, _ = x_sample.shape
    assert dec_seq == 1
    l = 0

    gmix = g_norm_mix[l][None, :]
    gq = jnp.tile(g_q[l], N_HEADS)[None, :]
    gk = jnp.tile(g_k[l], N_HEADS)[None, :]
    ga = g_attn_out[l][None, :]
    gc = g_conv_out[l][None, :]
    gf = g_norm_ffn[l][None, :]
    w_router = jnp.concatenate(
        [w_router_expert[l], w_router_group[l],
         jnp.zeros((D_MODEL, LANES - N_EXPERTS - N_GROUPS), F32)], axis=1)
    w_router_hi = w_router.astype(BF16)
    w_router_split = jnp.concatenate(
        [w_router_hi, (w_router - w_router_hi.astype(F32)).astype(BF16)], axis=1)

    cos_p, sin_p = _rope_tables(np.arange(seq))
    xp = x_prompt.reshape(nb * seq, D_MODEL)
    q, k, v, bc, tail, kt, vt = _inproj(xp, gmix, w_in[l].astype(BF16), gq, gk, cos_p, sin_p,
                                        w_conv[l], tm=512, seq_len=seq, hi=False,
                                        q_scale=ATTN_SCALE * LOG2_E, n_parts=2)
    attn = _attention(q.reshape(nb, seq, D_ATTN), k.reshape(nb, seq, D_ATTN),
                      v.reshape(nb, seq, D_ATTN))
    n_prompt = nb * seq
    n_total = n_prompt + db
    n_rows = -(-n_total // TOKEN_PAD) * TOKEN_PAD
    hx, route = _outproj(attn.reshape(n_prompt, D_ATTN), bc, xp, ga, gc, w_out[l].astype(BF16),
                         gf, w_router_split, tm=512, hi=False, n_rows=n_rows, row0=0, n_parts=2)
    k_prompt = jnp.transpose(kt, (0, 3, 1, 2))
    v_prompt = jnp.transpose(vt, (0, 3, 1, 2))
    conv_prompt = tail.reshape(nb, SUBLANES, D_CONV)[:, SUBLANES - (CONV_WIDTH - 1):]

    cos_s, sin_s = _rope_tables(np.full((db,), PAST_LEN))
    xs = x_sample.reshape(db, D_MODEL)
    st = state_conv[l]
    qs, ks, vs, bcs, us = _inproj(xs, gmix, w_in[l], gq, gk, cos_s, sin_s, w_conv[l],
                                  tm=db, seq_len=db, hi=True, q_scale=ATTN_SCALE,
                                  prev=(st[:, 1], st[:, 0]))
    heads = lambda t: t.reshape(db, N_HEADS, HEAD_DIM)
    attn_s = _attention_sample(heads(qs), heads(ks), heads(vs),
                               jnp.transpose(cache_k[l], (0, 2, 3, 1)),
                               jnp.transpose(cache_v[l], (0, 2, 3, 1)))
    hx, route = _outproj(attn_s.reshape(db, D_ATTN), bcs, xs, ga, gc, w_out[l], gf, w_router,
                         tm=db, hi=True, n_rows=n_rows, row0=n_prompt, into=(hx, route))

    n_cls = N_GROUPS * EXPERTS_PER_GROUP * (EXPERTS_PER_GROUP - 1) // 2
    max_tiles = -(-(n_total + n_cls * (MOE_TILE - 1)) // MOE_TILE) + N_GROUPS * (MOE_STEP_TILES - 1)
    n_slots = -(-max_tiles // MOE_STEP_TILES) * MOE_STEP_TILES * MOE_TILE
    assert n_slots // MOE_TILE <= N_CLASSES and n_rows % PLAN_ROWS == 0
    plan = _route_plan(route, n_total)
    ys = _moe(hx, gf, w_gate[l], w_up[l], w_down[l], plan, n_slots)
    y_prompt = _unsort(ys, plan[-1], n=n_prompt, row0=0, rows=512).reshape(nb, seq, D_MODEL)
    y_sample = _unsort(ys, plan[-1], n=db, row0=n_prompt, rows=db).reshape(db, 1, D_MODEL)
    k_sample = ks.reshape(db, 1, N_HEADS, HEAD_DIM)
    v_sample = vs.reshape(db, 1, N_HEADS, HEAD_DIM)
    conv_sample = jnp.stack([st[:, 1], us], axis=1)

    return (y_prompt, y_sample, k_prompt[None], v_prompt[None], conv_prompt[None],
            k_sample[None], v_sample[None], conv_sample[None])
```

```python
import functools

import numpy as np
import jax
import jax.numpy as jnp
from jax import lax
from jax.experimental import pallas as pl
from jax.experimental.pallas import tpu as pltpu

F32 = jnp.float32
BF16 = jnp.bfloat16

D_MODEL = 1024
HEAD_DIM = 64
N_HEADS = 12
D_ATTN = N_HEADS * HEAD_DIM
D_CONV = D_MODEL - D_ATTN
D_IN_PROJ = 3 * D_ATTN + 3 * D_CONV
CONV_WIDTH = 3
DILATIONS = (1, 4, 16)
ATT_BLOCK = 128
ATT_UNROLL = 32
REGROUP_STEP = 4
MAX_WINDOW = 2048
PAST_LEN = 8192
ATTN_SCALE = HEAD_DIM ** -0.5
LOG2_E = float(np.log2(np.e))
ROPE_THETA = 10000.0
EPS = 1e-6
N_GROUPS = 4
EXPERTS_PER_GROUP = 8
N_EXPERTS = N_GROUPS * EXPERTS_PER_GROUP
D_EXPERT = 256
MOE_TILE = 192
MOE_STEP_TILES = 2
N_CLASSES = N_GROUPS * EXPERTS_PER_GROUP * EXPERTS_PER_GROUP
PLAN_ROWS = 768
TOKEN_PAD = 512

LANES = 128
SUBLANES = 8
MXU_DIM = 256
TOKEN_ROWS = D_MODEL // LANES
NEG = -1e30
VMEM_LIMIT = 48 * 1024 * 1024


def _cparams(sem):
    return pltpu.CompilerParams(dimension_semantics=sem, vmem_limit_bytes=VMEM_LIMIT)


def _mm(a, b, hi):
    if hi:
        return jnp.dot(a, b, preferred_element_type=F32, precision=lax.Precision.HIGHEST)
    return jnp.dot(a.astype(BF16), b.astype(BF16), preferred_element_type=F32)


def _rms(x, g):
    return x * lax.rsqrt(jnp.mean(x * x, axis=-1, keepdims=True) + EPS) * g


def _rope_tables(pos):
    half = HEAD_DIM // 2
    inv = ROPE_THETA ** (-np.arange(half, dtype=np.float64) / half)
    ang = np.asarray(pos, np.float64)[:, None] * inv[None, :]
    cos, sin = np.cos(ang), np.sin(ang)
    cos2 = np.concatenate([cos, cos, cos, cos], axis=-1)
    sin2 = np.concatenate([-sin, sin, -sin, sin], axis=-1)
    return jnp.asarray(cos2, F32), jnp.asarray(sin2, F32)


def _inproj_kernel(*refs, tm, tiles_per_seq, keep_tiles, hi, given_prev, n_parts, q_scale):
    if given_prev:
        (x_ref, gmix_ref, w_ref, gq_ref, gk_ref, cos_ref, sin_ref, wconv_ref, p1_ref, p2_ref,
         q_ref, k_ref, v_ref, bc_ref, u_ref) = refs
    else:
        (x_ref, gmix_ref, w_ref, gq_ref, gk_ref, cos_ref, sin_ref, wconv_ref,
         q_ref, k_ref, v_ref, bc_ref, tail_ref, kt_ref, vt_ref, ubuf) = refs
    part_rows = tm // n_parts

    r_i = lax.broadcasted_iota(jnp.int32, (MXU_DIM, MXU_DIM), 0) // HEAD_DIM
    c_i = lax.broadcasted_iota(jnp.int32, (MXU_DIM, MXU_DIM), 1) // HEAD_DIM
    head_sum = jnp.where(r_i == c_i, 1.0, 0.0).astype(F32 if hi else BF16)
    lane = lax.broadcasted_iota(jnp.int32, (1, D_ATTN), 1)
    first_half = (lane % HEAD_DIM) < (HEAD_DIM // 2)
    wc = wconv_ref[...]

    def head_ssq(z):
        sq = z * z
        return jnp.concatenate(
            [_mm(sq[:, c * MXU_DIM:(c + 1) * MXU_DIM], head_sum, hi)
             for c in range(D_ATTN // MXU_DIM)], axis=-1)

    def norm_rope(z, ssq, g_ref, cos, sin):
        zn = z * lax.rsqrt(ssq * (1.0 / HEAD_DIM) + EPS) * g_ref[...]
        partner = jnp.where(first_half,
                            pltpu.roll(zn, D_ATTN - HEAD_DIM // 2, 1),
                            pltpu.roll(zn, HEAD_DIM // 2, 1))
        return zn * cos + partner * sin

    def conv_gate(b_gate, u2, u1, u):
        return b_gate * (wc[0:1, :] * u2 + wc[1:2, :] * u1 + wc[2:3, :] * u)

    def part(p, transposed):
        rows = pl.ds(p * part_rows, part_rows)
        xn = _rms(x_ref[rows, :], gmix_ref[...])
        if not hi:
            xn = xn.astype(BF16)
        zq = _mm(xn, w_ref[:, 0:D_ATTN], hi)
        zk = _mm(xn, w_ref[:, D_ATTN:2 * D_ATTN], hi)
        ssq_q = head_ssq(zq)
        v = _mm(xn, w_ref[:, 2 * D_ATTN:3 * D_ATTN], hi)
        ssq_k = head_ssq(zk)
        c0 = 3 * D_ATTN
        b_gate = _mm(xn, w_ref[:, c0:c0 + D_CONV], hi)
        c_gate = _mm(xn, w_ref[:, c0 + D_CONV:c0 + 2 * D_CONV], hi)
        u_in = _mm(xn, w_ref[:, c0 + 2 * D_CONV:c0 + 3 * D_CONV], hi)
        yield
        reps = D_ATTN // LANES
        cos = jnp.concatenate([cos_ref[rows, :]] * reps, axis=-1)
        sin = jnp.concatenate([sin_ref[rows, :]] * reps, axis=-1)
        q_ref[rows, :] = norm_rope(zq, ssq_q, gq_ref, cos, sin) * q_scale
        k = norm_rope(zk, ssq_k, gk_ref, cos, sin)
        k_ref[rows, :] = k
        v_ref[rows, :] = v
        if transposed:
            cols = slice(p * part_rows, (p + 1) * part_rows)
            kt_ref[0, :, :, cols] = k.T.reshape(N_HEADS, HEAD_DIM, part_rows)
            vt_ref[0, :, :, cols] = v.T.reshape(N_HEADS, HEAD_DIM, part_rows)
        u = c_gate * u_in
        if given_prev:
            u_ref[rows, :] = u
            bc_ref[rows, :] = conv_gate(b_gate, p2_ref[rows, :], p1_ref[rows, :], u)
        else:
            ubuf[pl.ds(SUBLANES + p * part_rows, part_rows), :] = u
        return b_gate

    def body(transposed):
        stages = [part(p, transposed) for p in range(n_parts)]
        for stage in stages:
            next(stage)
        gates = []
        for stage in stages:
            try:
                next(stage)
            except StopIteration as done:
                gates.append(done.value)
        if not given_prev:
            u = ubuf[SUBLANES:tm + SUBLANES, :]
            u1 = ubuf[SUBLANES - 1:tm + SUBLANES - 1, :]
            u2 = ubuf[SUBLANES - 2:tm + SUBLANES - 2, :]
            bc_ref[...] = conv_gate(jnp.concatenate(gates, axis=0), u2, u1, u)
            tail_ref[...] = ubuf[tm:tm + SUBLANES, :]

    if given_prev:
        body(False)
        return

    i = pl.program_id(0)

    @pl.when(i % tiles_per_seq == 0)
    def _():
        ubuf[0:SUBLANES, :] = jnp.zeros((SUBLANES, D_CONV), F32)

    @pl.when(i % tiles_per_seq != 0)
    def _():
        ubuf[0:SUBLANES, :] = ubuf[tm:tm + SUBLANES, :]

    kept = i % tiles_per_seq >= tiles_per_seq - keep_tiles
    pl.when(kept)(functools.partial(body, True))
    pl.when(jnp.logical_not(kept))(functools.partial(body, False))


def _inproj(x, gmix, w_in, gq, gk, cos, sin, w_conv, *, tm, seq_len, hi, q_scale, n_parts=1,
            prev=None):
    n = x.shape[0]
    nt = n // tm
    tiles_per_seq = seq_len // tm
    row = lambda w: pl.BlockSpec((tm, w), lambda i: (i, 0))
    full = lambda a: pl.BlockSpec(a.shape, lambda i: (0,) * a.ndim)
    tab = pl.BlockSpec((tm, LANES), lambda i: (i % tiles_per_seq, 0))
    once = pl.BlockSpec(w_in.shape, lambda i: (0, 0), pipeline_mode=pl.Buffered(1))
    in_specs = [row(D_MODEL), full(gmix), once, full(gq), full(gk), tab, tab, full(w_conv)]
    args = [x, gmix, w_in, gq, gk, cos, sin, w_conv]
    out_shape = [jax.ShapeDtypeStruct((n, D_ATTN), F32)] * 3 + [jax.ShapeDtypeStruct((n, D_CONV), F32)]
    out_specs = [row(D_ATTN)] * 3 + [row(D_CONV)]
    scratch = []
    keep_tiles = min(MAX_WINDOW, seq_len) // tm
    if prev is not None:
        in_specs += [row(D_CONV), row(D_CONV)]
        args += list(prev)
        out_shape.append(jax.ShapeDtypeStruct((n, D_CONV), F32))
        out_specs.append(row(D_CONV))
    else:
        n_seq = n // seq_len
        out_shape.append(jax.ShapeDtypeStruct((n_seq * SUBLANES, D_CONV), F32))
        out_specs.append(pl.BlockSpec((SUBLANES, D_CONV), lambda i: (i // tiles_per_seq, 0)))
        kept = pl.BlockSpec(
            (1, N_HEADS, HEAD_DIM, tm),
            lambda i: (i // tiles_per_seq, 0, 0,
                       jnp.maximum(i % tiles_per_seq - (tiles_per_seq - keep_tiles), 0)))
        out_shape += [jax.ShapeDtypeStruct((n_seq, N_HEADS, HEAD_DIM, keep_tiles * tm), F32)] * 2
        out_specs += [kept, kept]
        scratch.append(pltpu.VMEM((tm + SUBLANES, D_CONV), F32))
    return pl.pallas_call(
        functools.partial(_inproj_kernel, tm=tm, tiles_per_seq=tiles_per_seq,
                          keep_tiles=keep_tiles, hi=hi, given_prev=prev is not None,
                          n_parts=n_parts, q_scale=q_scale),
        grid=(nt,), in_specs=in_specs, out_specs=out_specs, out_shape=out_shape,
        scratch_shapes=scratch, compiler_params=_cparams(("arbitrary",)),
        name="inproj_hi" if hi else "inproj",
    )(*args)


def _attn_kernel(q_ref, k_ref, v_ref, o_ref, num_ref, den_ref, m_ref, bias_ref,
                 qd_ref, kd_ref, vd_ref, numd_ref, dend_ref, md_ref, tmp_ref, *, seq_len):
    blk = ATT_BLOCK
    qi = lax.broadcasted_iota(jnp.int32, (blk, 2 * blk), 0)
    kj = lax.broadcasted_iota(jnp.int32, (blk, 2 * blk), 1)
    bias_ref[...] = jnp.where((kj >= qi) & (kj <= qi + blk), 0.0, NEG)
    head0 = lax.broadcasted_iota(jnp.int32, (1, LANES), 1) < HEAD_DIM
    dn = (((1,), (1,)), ((), ()))

    def rows(start, size, d):
        return pl.ds(start, size) if d == 1 else pl.ds(start, size, stride=d)

    natural = (lambda at: q_ref[0, at, :], lambda at: k_ref[0, at, :], lambda at: v_ref[0, at, :],
               (num_ref, den_ref, m_ref))
    regrouped = (lambda at: qd_ref[at, :], lambda at: kd_ref[at, :], lambda at: vd_ref[at, :],
                 (numd_ref, dend_ref, md_ref))

    def group(src, d, r, n0, u, first, mode):
        q_at, k_at, v_at, (num_acc, den_acc, m_acc) = src
        q_rows = rows(n0 * (blk * d) + r, u * blk, d)
        if first:
            k_rows = rows(r, u * blk, d)
        else:
            k_rows = rows((n0 - 1) * (blk * d) + r, (u + 1) * blk, d)
        qa = q_at(q_rows)
        ka = k_at(k_rows).astype(BF16)
        va = v_at(k_rows)
        qa0 = jnp.where(head0, qa, 0.0).astype(BF16)
        qa1 = jnp.where(head0, 0.0, qa).astype(BF16)
        va0 = jnp.where(head0, va, 1.0).astype(BF16)
        va1 = jnp.where(head0, 1.0, va).astype(BF16)
        if mode != "init":
            m_old = m_acc[q_rows, :]
            num_old = num_acc[q_rows, :]
            den_old = den_acc[q_rows, :]
        key_slices, scores = [], []
        for i in range(u):
            qs = slice(i * blk, (i + 1) * blk)
            if first and i == 0:
                ks = slice(0, blk)
                bias = bias_ref[:, blk:2 * blk]
            else:
                k0 = (i - 1) * blk if first else i * blk
                ks = slice(k0, k0 + 2 * blk)
                bias = bias_ref[...]
            key_slices.append(ks)
            scores.append(
                (lax.dot_general(qa0[qs], ka[ks], dn, preferred_element_type=F32) + bias,
                 lax.dot_general(qa1[qs], ka[ks], dn, preferred_element_type=F32) + bias))
        yield
        probs, ms = [], []
        for s0, s1 in scores:
            m0 = jnp.max(s0, axis=-1, keepdims=True)
            m1 = jnp.max(s1, axis=-1, keepdims=True)
            probs.append((jnp.exp2(s0 - m0).astype(BF16), jnp.exp2(s1 - m1).astype(BF16)))
            ms.append(jnp.where(head0, m0, m1))
        yield
        nums, dens = [], []
        for (p0, p1), ks in zip(probs, key_slices):
            r0 = jnp.dot(p0, va0[ks], preferred_element_type=F32)
            r1 = jnp.dot(p1, va1[ks], preferred_element_type=F32)
            nums.append(jnp.where(head0, r0, r1))
            dens.append(pltpu.roll(jnp.where(head0, r1, r0), HEAD_DIM, 1))
        num = jnp.concatenate(nums, axis=0)
        den = jnp.concatenate(dens, axis=0)
        m = jnp.concatenate(ms, axis=0)
        if mode == "init":
            return [(num_acc, q_rows, num), (den_acc, q_rows, den), (m_acc, q_rows, m)]
        m_new = jnp.maximum(m_old, m)
        a = jnp.exp2(m_old - m_new)
        b = jnp.exp2(m - m_new)
        num = num_old * a + num * b
        den = den_old * a + den * b
        if mode == "merge":
            return [(num_acc, q_rows, num), (den_acc, q_rows, den), (m_acc, q_rows, m_new)]
        return [(o_ref.at[0], q_rows, num / den)]

    def run(groups):
        stages = [group(*g) for g in groups]
        for _ in range(2):
            for stage in stages:
                next(stage)
        stores = []
        for stage in stages:
            try:
                next(stage)
            except StopIteration as done:
                stores += done.value
        for ref, at, val in stores:
            ref[at, :] = val

    def branch(src, d, n_seq, sub_len, seq_stride, mode):
        nb = sub_len // blk
        u = min(ATT_UNROLL, nb)
        if nb == u:
            per_body = ATT_UNROLL // u

            def several(rr, carry):
                run([(src, d, (rr * per_body + t) * seq_stride, 0, u, True, mode)
                     for t in range(per_body)])
                return carry

            lax.fori_loop(0, n_seq // per_body, several, 0)
        else:
            def one(r, carry):
                run([(src, d, r * seq_stride, 0, u, True, mode)])

                def later_blocks(g, c):
                    run([(src, d, r * seq_stride, g * u, u, False, mode)])
                    return c

                lax.fori_loop(1, nb // u, later_blocks, 0)
                return carry

            lax.fori_loop(0, n_seq, one, 0)

    step = REGROUP_STEP
    wide = step * step
    assert DILATIONS == (1, step, wide)
    part = seq_len // wide

    def regroup(load, dst_ref):
        for a in range(step):
            tmp_ref[a * step * part:(a + 1) * step * part, :] = load(rows(a, step * part, step))
        for a in range(step):
            for b in range(step):
                s = a * step + b
                dst_ref[s * part:(s + 1) * part, :] = tmp_ref[rows(a * step * part + b, part, step), :]

    def ungroup(src_ref, dst_ref):
        for a in range(step):
            for b in range(step):
                s = a * step + b
                tmp_ref[rows(a * step * part + b, part, step), :] = src_ref[s * part:(s + 1) * part, :]
        for a in range(step):
            dst_ref[rows(a, step * part, step), :] = tmp_ref[a * step * part:(a + 1) * step * part, :]

    for load, dst in zip(natural[:3], (qd_ref, kd_ref, vd_ref)):
        regroup(load, dst)
    branch(regrouped, 1, wide, part, part, "init")
    for src, dst in zip(regrouped[3], natural[3]):
        ungroup(src, dst)
    branch(natural, 1, 1, seq_len, 0, "merge")
    branch(natural, step, step, seq_len // step, 1, "final")


def _attention(q, k, v):
    b, s, _ = q.shape
    spec = pl.BlockSpec((1, s, LANES), lambda i, p: (i, 0, p))
    return pl.pallas_call(
        functools.partial(_attn_kernel, seq_len=s),
        grid=(b, D_ATTN // LANES), in_specs=[spec] * 3, out_specs=spec,
        out_shape=jax.ShapeDtypeStruct((b, s, D_ATTN), F32),
        scratch_shapes=[pltpu.VMEM((s, LANES), F32)] * 3 + [pltpu.VMEM((ATT_BLOCK, 2 * ATT_BLOCK), F32)]
        + [pltpu.VMEM((s, LANES), F32)] * 7,
        compiler_params=_cparams(("arbitrary", "arbitrary")), name="attn_prompt",
    )(q, k, v)


def _attn_sample_kernel(q_ref, kn_ref, vn_ref, kc_ref, vc_ref, o_ref, *, w_buf, heads):
    age = w_buf - lax.broadcasted_iota(jnp.int32, (1, w_buf), 1)
    valid = [(age % d == 0) & (age <= d * ATT_BLOCK) for d in DILATIONS]
    diag = (lax.broadcasted_iota(jnp.int32, (HEAD_DIM, HEAD_DIM), 0)
            == lax.broadcasted_iota(jnp.int32, (HEAD_DIM, HEAD_DIM), 1))
    to_col = lambda row: jnp.sum(jnp.where(diag, row, 0.0), axis=1, keepdims=True)
    to_row = lambda col: jnp.sum(jnp.where(diag, col, 0.0), axis=0, keepdims=True)
    q_rows, kn_rows, vn_rows = q_ref[0], kn_ref[0], vn_ref[0]
    s_self_all = jnp.sum(kn_rows * q_rows, axis=1, keepdims=True)
    for h in range(heads):
        q = to_col(q_rows[h:h + 1])
        vn = to_col(vn_rows[h:h + 1])
        s = jnp.sum(kc_ref[0, h] * q, axis=0, keepdims=True)
        s_self = s_self_all[h:h + 1]
        parts = []
        for ok in valid:
            sd = jnp.where(ok, s, NEG)
            m = jnp.maximum(jnp.max(sd, axis=1, keepdims=True), s_self)
            p = jnp.exp(sd - m)
            p_self = jnp.exp(s_self - m)
            parts.append((p, p_self, jnp.sum(p, axis=1, keepdims=True) + p_self, m))
        m_all = jnp.maximum(jnp.maximum(parts[0][3], parts[1][3]), parts[2][3])
        w = jnp.zeros((1, w_buf), F32)
        w_self = jnp.zeros((1, 1), F32)
        den_t = jnp.zeros((1, 1), F32)
        for p, p_self, den, m in parts:
            c = jnp.exp(m - m_all)
            w = w + p * c
            w_self = w_self + p_self * c
            den_t = den_t + den * c
        num_t = jnp.sum(vc_ref[0, h] * w, axis=1, keepdims=True) + vn * w_self
        o_ref[0, h:h + 1, :] = to_row(num_t / den_t)


def _attention_sample(q, kn, vn, cache_kt, cache_vt):
    db, heads, _, w_buf = cache_kt.shape
    tok = pl.BlockSpec((1, heads, HEAD_DIM), lambda i: (i, 0, 0))
    cache = pl.BlockSpec((1, heads, HEAD_DIM, w_buf), lambda i: (i, 0, 0, 0))
    return pl.pallas_call(
        functools.partial(_attn_sample_kernel, w_buf=w_buf, heads=heads),
        grid=(db,), in_specs=[tok, tok, tok, cache, cache], out_specs=tok,
        out_shape=jax.ShapeDtypeStruct((db, heads, HEAD_DIM), F32),
        compiler_params=_cparams(("arbitrary",)), name="attn_sample",
    )(q, kn, vn, cache_kt, cache_vt)


def _outproj_kernel(attn_ref, bc_ref, x_ref, ga_ref, gc_ref, wo_ref, gf_ref, wr_ref, *rest,
                    hi, n_tiles, n_parts):
    hx_ref, route_ref = rest[-2:]
    part_rows = x_ref.shape[0] // n_parts

    def part(k):
        rows = pl.ds(k * part_rows, part_rows)
        a = _rms(attn_ref[rows, :], ga_ref[...])
        c = _rms(bc_ref[rows, :], gc_ref[...])
        mix = _mm(a, wo_ref[0:D_ATTN, :], hi) + _mm(c, wo_ref[D_ATTN:D_MODEL, :], hi)
        yield
        h = x_ref[rows, :] + mix
        hn = _rms(h, gf_ref[...])
        if hi:
            lg = _mm(hn, wr_ref[...], True)
        else:
            hn_hi = hn.astype(BF16)
            hn_lo = (hn - hn_hi.astype(F32)).astype(BF16)
            both = jnp.dot(hn_hi, wr_ref[...], preferred_element_type=F32)
            lg = (both[:, :LANES] + both[:, LANES:]
                  + jnp.dot(hn_lo, wr_ref[:, :LANES], preferred_element_type=F32))
        yield
        lane_i = lax.broadcasted_iota(jnp.int32, lg.shape, 1)
        lane = lane_i.astype(F32)
        lane_group = (lane_i // EXPERTS_PER_GROUP).astype(F32)
        is_e = lane_i < N_EXPERTS
        is_g = (lane_i >= N_EXPERTS) & (lane_i < N_EXPERTS + N_GROUPS)
        big = float(LANES)
        first_at = lambda t, v: jnp.min(jnp.where(t == v, lane, big), axis=-1, keepdims=True)

        gl = jnp.where(is_g, lg, NEG)
        mg = jnp.max(gl, axis=-1, keepdims=True)
        p_g = 1.0 / jnp.sum(jnp.exp(gl - mg), axis=-1, keepdims=True)
        g_idx = first_at(gl, mg) - float(N_EXPERTS)
        el = jnp.where(is_e & (lane_group == g_idx), lg, NEG)
        v1 = jnp.max(el, axis=-1, keepdims=True)
        i1 = first_at(el, v1)
        el2 = jnp.where(lane == i1, NEG, el)
        v2 = jnp.max(el2, axis=-1, keepdims=True)
        i2 = first_at(el2, v2)
        e2 = jnp.exp(v2 - v1)
        w1 = p_g / (1.0 + e2)
        w2 = p_g * e2 / (1.0 + e2)
        route_ref[rows, :] = sum(jnp.where(lane_i == j, val, 0.0)
                                 for j, val in enumerate([i1, i2, w1, w2]))
        hx_ref[rows, :, :] = h.reshape(part_rows, TOKEN_ROWS, LANES)

    stages = [part(k) for k in range(n_parts)]
    for _ in range(2):
        for stage in stages:
            next(stage)
    for stage in stages:
        for _ in stage:
            pass

    @pl.when(pl.program_id(0) >= n_tiles)
    def _():
        hx_ref[...] = jnp.zeros(hx_ref.shape, F32)
        route_ref[...] = jnp.zeros(route_ref.shape, F32)


def _outproj(attn, bc, x, ga, gc, w_out, gf, w_router, *, tm, hi, n_rows, row0, n_parts=1,
             into=None):
    n = x.shape[0]
    n_tiles = n // tm
    n_fill = 0 if into is not None else -(-(n_rows - n) // tm)
    row = lambda w: pl.BlockSpec((tm, w), lambda i: (jnp.minimum(i, n_tiles - 1), 0))
    full = lambda a: pl.BlockSpec(a.shape, lambda i: (0,) * a.ndim)
    in_specs = [row(D_ATTN), row(D_CONV), row(D_MODEL), full(ga), full(gc), full(w_out),
                full(gf), full(w_router)]
    args = [attn, bc, x, ga, gc, w_out, gf, w_router]
    aliases = {}
    if into is not None:
        in_specs += [pl.BlockSpec(memory_space=pl.ANY)] * 2
        args += list(into)
        aliases = {len(args) - 2: 0, len(args) - 1: 1}
    return pl.pallas_call(
        functools.partial(_outproj_kernel, hi=hi, n_tiles=n_tiles, n_parts=n_parts),
        grid=(n_tiles + n_fill,), in_specs=in_specs,
        out_specs=[pl.BlockSpec((tm, TOKEN_ROWS, LANES), lambda i: (i + row0 // tm, 0, 0)),
                   pl.BlockSpec((tm, LANES), lambda i: (i + row0 // tm, 0))],
        out_shape=[jax.ShapeDtypeStruct((n_rows, TOKEN_ROWS, LANES), F32),
                   jax.ShapeDtypeStruct((n_rows, LANES), F32)],
        input_output_aliases=aliases,
        compiler_params=_cparams(("arbitrary",)), name="outproj_hi" if hi else "outproj",
    )(*args)


def _plan_kernel(route_ref, slot_ref, pos_ref, wa_ref, wb_ref, tiles_ref,
                 run_ref, cls_s, rank_s, later_s, *, n_steps, n_tokens):
    i = pl.program_id(0)
    rows = route_ref.shape[0]
    e = float(EXPERTS_PER_GROUP)

    @pl.when(i == 0)
    def _():
        run_ref[...] = jnp.zeros(run_ref.shape, F32)
        later_s[...] = jnp.where(lax.broadcasted_iota(jnp.int32, (rows, rows), 0)
                                 < lax.broadcasted_iota(jnp.int32, (rows, rows), 1),
                                 1.0, 0.0).astype(BF16)

    r = route_ref[...].T
    i1, i2, w1, w2 = r[0:1], r[1:2], r[2:3], r[3:4]
    grp = jnp.floor(i1 * (1.0 / e))
    cls = grp * (e * e) + (jnp.minimum(i1, i2) - grp * e) * e + (jnp.maximum(i1, i2) - grp * e)
    token = i * rows + lax.broadcasted_iota(jnp.int32, (1, rows), 1)
    class_id = lax.broadcasted_iota(jnp.int32, (N_CLASSES, rows), 0).astype(F32)
    own = (class_id == cls) & (token < n_tokens)
    before = jnp.dot(jnp.where(own, 1.0, 0.0).astype(BF16), later_s[...],
                     preferred_element_type=F32)
    rank = jnp.sum(jnp.where(own, before + run_ref[...], 0.0), axis=0, keepdims=True)
    run_ref[...] += jnp.sum(jnp.where(own, 1.0, 0.0), axis=1, keepdims=True)
    first_is_a = i1 < i2
    lines = ((cls_s, cls), (rank_s, rank),
             (wa_ref, jnp.where(first_is_a, w1, w2)), (wb_ref, jnp.where(first_is_a, w2, w1)))
    for c in range(rows // LANES):
        for ref, line in lines:
            ref[pl.ds(i * (rows // LANES) + c, 1), :] = line[:, c * LANES:(c + 1) * LANES]

    @pl.when(i == n_steps - 1)
    def _():
        counts = run_ref[...]
        tiles = jnp.floor((counts + (MOE_TILE - 1.0)) * (1.0 / MOE_TILE))
        cid = lax.broadcasted_iota(jnp.int32, (N_CLASSES, 1), 0)
        per_group = EXPERTS_PER_GROUP * EXPERTS_PER_GROUP
        for g in range(N_GROUPS):
            g_tiles = jnp.sum(jnp.where(cid // per_group == g, tiles, 0.0), axis=0, keepdims=True)
            spare = jnp.ceil(g_tiles * (1.0 / MOE_STEP_TILES)) * MOE_STEP_TILES - g_tiles
            tiles = tiles + jnp.where(cid == (g + 1) * per_group - 1, spare, 0.0)
        upto = jnp.where(lax.broadcasted_iota(jnp.int32, (N_CLASSES, N_CLASSES), 0)
                         >= lax.broadcasted_iota(jnp.int32, (N_CLASSES, N_CLASSES), 1), 1.0, 0.0)
        lane = lax.broadcasted_iota(jnp.int32, (N_CLASSES, LANES), 1)
        both = jnp.where(lane == 0, tiles, 0.0) + jnp.where(lane == 1, counts, 0.0)
        ends = _mm(upto, both, True)
        tile_end, tok_end = ends[:, 0:1], ends[:, 1:2]
        tile_off, tok_off = tile_end - tiles, tok_end - counts
        n_tiles = jnp.max(tile_end, axis=0, keepdims=True)
        t = lax.broadcasted_iota(jnp.int32, (1, N_CLASSES), 1).astype(F32)
        t = jnp.minimum(t, n_tiles - 1.0)
        cls_t = jnp.sum(jnp.where(tile_end <= t, 1.0, 0.0), axis=0, keepdims=True)
        mine = lax.broadcasted_iota(jnp.int32, (N_CLASSES, N_CLASSES), 0).astype(F32) == cls_t
        pick = lambda v: jnp.sum(jnp.where(mine, v, 0.0), axis=0, keepdims=True)
        base = pick(tok_off) + (t - pick(tile_off)) * MOE_TILE
        row8 = lax.broadcasted_iota(jnp.int32, (SUBLANES, N_CLASSES), 0)
        tiles_ref[...] = sum(jnp.where(row8 == j, v, 0.0)
                             for j, v in enumerate([cls_t, base, n_tiles + jnp.zeros_like(t)]))

        token_class = lax.broadcasted_iota(jnp.int32, (N_CLASSES, LANES), 0).astype(F32)

        def per_block(j, carry):
            at = pl.ds(j, 1)
            hit = token_class == cls_s[at, :]
            look = lambda col: jnp.sum(jnp.where(hit, col, 0.0), axis=0, keepdims=True)
            rank_row = rank_s[at, :]
            slot_ref[at, :] = (look(tile_off) * MOE_TILE + rank_row).astype(jnp.int32)
            pos_ref[at, :] = (look(tok_off) + rank_row).astype(jnp.int32)
            return carry

        lax.fori_loop(0, slot_ref.shape[0], per_block, 0)


def _plan_order_kernel(pos_ref, order_ref, *, n_tokens):
    def place(t, carry):
        order_ref[pos_ref[t]] = t
        return carry

    lax.fori_loop(0, n_tokens, place, 0, unroll=32)
    for j in range(MOE_TILE):
        order_ref[n_tokens + j] = 0


def _route_plan(route, n_tokens):
    n_rows = route.shape[0]
    rows = PLAN_ROWS
    n_steps = n_rows // rows
    dense = lambda dtype: jax.ShapeDtypeStruct((n_rows // LANES, LANES), dtype)
    whole = lambda r, c: pl.BlockSpec((r, c), lambda i: (0, 0))
    slot, pos, wa, wb, tiles = pl.pallas_call(
        functools.partial(_plan_kernel, n_steps=n_steps, n_tokens=n_tokens),
        grid=(n_steps,), in_specs=[pl.BlockSpec((rows, LANES), lambda i: (i, 0))],
        out_specs=[whole(n_rows // LANES, LANES)] * 4 + [whole(SUBLANES, N_CLASSES)],
        out_shape=[dense(jnp.int32), dense(jnp.int32), dense(F32), dense(F32),
                   jax.ShapeDtypeStruct((SUBLANES, N_CLASSES), F32)],
        scratch_shapes=[pltpu.VMEM((N_CLASSES, 1), F32),
                        pltpu.VMEM((n_rows // LANES, LANES), F32),
                        pltpu.VMEM((n_rows // LANES, LANES), F32),
                        pltpu.VMEM((rows, rows), BF16)],
        compiler_params=_cparams(("arbitrary",)), name="plan",
    )(route)
    order = pl.pallas_call(
        functools.partial(_plan_order_kernel, n_tokens=n_tokens),
        in_specs=[pl.BlockSpec(memory_space=pltpu.SMEM)],
        out_specs=pl.BlockSpec(memory_space=pltpu.SMEM),
        out_shape=jax.ShapeDtypeStruct((n_tokens + MOE_TILE,), jnp.int32), name="plan_order",
    )(pos.reshape(-1))
    tile_cls = tiles[0].astype(jnp.int32)
    tile_base = tiles[1].astype(jnp.int32)
    n_tiles = tiles[2, 0:1].astype(jnp.int32)
    return tile_cls, tile_base, n_tiles, order, wa.reshape(-1), wb.reshape(-1), slot.reshape(-1)


def _moe_kernel(tcls_ref, tbase_ref, nt_ref, order_ref, wa_ref, wb_ref,
                hx_hbm, gf_ref, wg_ref, wu_ref, wd_ref, y_ref,
                xbuf, wgu_s, wd_s, sem):
    s = pl.program_id(0)
    n_tiles = nt_ref[0]
    pair = EXPERTS_PER_GROUP * EXPERTS_PER_GROUP
    n_buf = xbuf.shape[0]
    first_tile = s * MOE_STEP_TILES

    def gather(tile):
        base = tbase_ref[tile]
        for j in range(MOE_TILE):
            pltpu.make_async_copy(hx_hbm.at[order_ref[base + j]], xbuf.at[tile % n_buf, j],
                                  sem.at[tile % n_buf]).start()

    for k in range(MOE_STEP_TILES):
        @pl.when((s == 0) & (k < n_tiles))
        def _(k=k):
            gather(k)

    for k in range(MOE_STEP_TILES):
        @pl.when(first_tile + MOE_STEP_TILES + k < n_tiles)
        def _(k=k):
            gather(first_tile + MOE_STEP_TILES + k)

    @pl.when(first_tile < n_tiles)
    def _():
        grp = tcls_ref[first_tile] // pair

        @pl.when((s == 0) | (grp != tcls_ref[jnp.maximum(first_tile - 1, 0)] // pair))
        def _():
            for e in range(EXPERTS_PER_GROUP):
                wgu_s[e, :, 0:D_EXPERT] = wg_ref[0, e].astype(BF16)
                wgu_s[e, :, D_EXPERT:2 * D_EXPERT] = wu_ref[0, e].astype(BF16)
                wd_s[e] = wd_ref[0, e].astype(BF16)

    def run_tile(tile, out_rows):
        cls = tcls_ref[tile]
        slot = tile % n_buf
        pltpu.make_async_copy(hx_hbm.at[pl.ds(0, MOE_TILE)], xbuf.at[slot], sem.at[slot]).wait()
        h = xbuf[slot].reshape(MOE_TILE, D_MODEL)
        xn = _rms(h, gf_ref[...]).astype(BF16)
        ea = (cls % pair) // EXPERTS_PER_GROUP
        eb = cls % EXPERTS_PER_GROUP

        def column(w_ref):
            base = tbase_ref[tile]
            sub = lax.broadcasted_iota(jnp.int32, (SUBLANES, 1), 0)
            chunks = []
            for c in range(MOE_TILE // SUBLANES):
                acc = jnp.zeros((SUBLANES, 1), F32)
                for k in range(SUBLANES):
                    acc = jnp.where(sub == k, w_ref[order_ref[base + c * SUBLANES + k]], acc)
                chunks.append(acc)
            return jnp.concatenate(chunks, axis=0)

        gus = [jnp.dot(xn, wgu_s[e], preferred_element_type=F32) for e in (ea, eb)]
        acts = [(jax.nn.silu(gu[:, 0:D_EXPERT]) * gu[:, D_EXPERT:2 * D_EXPERT]).astype(BF16)
                for gu in gus]
        oa, ob = [jnp.dot(act, wd_s[e], preferred_element_type=F32)
                  for act, e in zip(acts, (ea, eb))]
        y = h + column(wa_ref) * oa + column(wb_ref) * ob
        y_ref[out_rows, :, :] = y.reshape(MOE_TILE, D_MODEL // LANES, LANES)

    for k in range(MOE_STEP_TILES):
        out_rows = pl.ds(k * MOE_TILE, MOE_TILE)
        pl.when(first_tile + k < n_tiles)(functools.partial(run_tile, first_tile + k, out_rows))

        @pl.when(first_tile + k >= n_tiles)
        def _(out_rows=out_rows):
            y_ref[out_rows, :, :] = jnp.zeros((MOE_TILE, D_MODEL // LANES, LANES), F32)


def _moe(hx, gf, w_gate, w_up, w_down, plan, n_slots):
    tile_cls, tile_base, n_tiles, order, wa, wb, _ = plan
    pair = EXPERTS_PER_GROUP * EXPERTS_PER_GROUP
    assert n_slots % (MOE_STEP_TILES * MOE_TILE) == 0 and n_slots // MOE_TILE <= tile_cls.shape[0]
    grouped = lambda w: w.reshape(N_GROUPS, EXPERTS_PER_GROUP, *w.shape[1:])
    wspec = lambda r, c: pl.BlockSpec((1, EXPERTS_PER_GROUP, r, c),
                                      lambda s, tc, *_: (tc[s * MOE_STEP_TILES] // pair, 0, 0, 0),
                                      pipeline_mode=pl.Buffered(1))
    grid_spec = pltpu.PrefetchScalarGridSpec(
        num_scalar_prefetch=6, grid=(n_slots // (MOE_STEP_TILES * MOE_TILE),),
        in_specs=[pl.BlockSpec(memory_space=pl.ANY),
                  pl.BlockSpec(gf.shape, lambda t, *_: (0, 0)),
                  wspec(D_MODEL, D_EXPERT), wspec(D_MODEL, D_EXPERT), wspec(D_EXPERT, D_MODEL)],
        out_specs=pl.BlockSpec((MOE_STEP_TILES * MOE_TILE, D_MODEL // LANES, LANES),
                               lambda s, *_: (s, 0, 0)),
        scratch_shapes=[pltpu.VMEM((2 * MOE_STEP_TILES, MOE_TILE, TOKEN_ROWS, LANES), F32),
                        pltpu.VMEM((EXPERTS_PER_GROUP, D_MODEL, 2 * D_EXPERT), BF16),
                        pltpu.VMEM((EXPERTS_PER_GROUP, D_EXPERT, D_MODEL), BF16),
                        pltpu.SemaphoreType.DMA((2 * MOE_STEP_TILES,))])
    return pl.pallas_call(
        _moe_kernel, grid_spec=grid_spec,
        out_shape=jax.ShapeDtypeStruct((n_slots, D_MODEL // LANES, LANES), F32),
        compiler_params=_cparams(("arbitrary",)), name="moe",
    )(tile_cls, tile_base, n_tiles, order, wa, wb, hx, gf,
      grouped(w_gate), grouped(w_up), grouped(w_down))


def _unsort_kernel(slot_ref, ys_hbm, y_ref, buf, sem, *, rows, row0):
    i = pl.program_id(0)
    b = i % 2

    def gather(step, dst):
        for j in range(rows):
            pltpu.make_async_copy(ys_hbm.at[slot_ref[row0 + step * rows + j]], buf.at[dst, j],
                                  sem.at[dst]).start(priority=j % 2)

    @pl.when(i == 0)
    def _():
        gather(0, 0)

    @pl.when(i + 1 < pl.num_programs(0))
    def _():
        gather(i + 1, 1 - b)

    pltpu.make_async_copy(ys_hbm.at[pl.ds(0, rows)], buf.at[b], sem.at[b]).wait()
    y_ref[...] = buf[b].reshape(rows, D_MODEL)


def _unsort(ys, token_slot, *, n, row0, rows):
    grid_spec = pltpu.PrefetchScalarGridSpec(
        num_scalar_prefetch=1, grid=(n // rows,),
        in_specs=[pl.BlockSpec(memory_space=pl.ANY)],
        out_specs=pl.BlockSpec((rows, D_MODEL), lambda i, *_: (i, 0)),
        scratch_shapes=[pltpu.VMEM((2, rows, D_MODEL // LANES, LANES), F32),
                        pltpu.SemaphoreType.DMA((2,))])
    return pl.pallas_call(
        functools.partial(_unsort_kernel, rows=rows, row0=row0), grid_spec=grid_spec,
        out_shape=jax.ShapeDtypeStruct((n, D_MODEL), F32),
        compiler_params=_cparams(("arbitrary",)), name="unsort",
    )(token_slot, ys)


def kernel(x_prompt, x_sample, cache_k, cache_v, state_conv, g_norm_mix, w_in, g_q, g_k, w_conv,
           g_attn_out, g_conv_out, w_out, g_norm_ffn, w_router_group, w_router_expert,
           w_gate, w_up, w_down):
    depth = g_norm_mix.shape[0]
    assert depth == 1
    nb, seq, _ = x_prompt.shape
    db, dec_seq, _ = x_sample.shape
    assert dec_seq == 1
    l = 0

    gmix = g_norm_mix[l][None, :]
    gq = jnp.tile(g_q[l], N_HEADS)[None, :]
    gk = jnp.tile(g_k[l], N_HEADS)[None, :]
    ga = g_attn_out[l][None, :]
    gc = g_conv_out[l][None, :]
    gf = g_norm_ffn[l][None, :]
    w_router = jnp.concatenate(
        [w_router_expert[l], w_router_group[l],
         jnp.zeros((D_MODEL, LANES - N_EXPERTS - N_GROUPS), F32)], axis=1)
    w_router_hi = w_router.astype(BF16)
    w_router_split = jnp.concatenate(
        [w_router_hi, (w_router - w_router_hi.astype(F32)).astype(BF16)], axis=1)

    cos_p, sin_p = _rope_tables(np.arange(seq))
    xp = x_prompt.reshape(nb * seq, D_MODEL)
    q, k, v, bc, tail, kt, vt = _inproj(xp, gmix, w_in[l].astype(BF16), gq, gk, cos_p, sin_p,
                                        w_conv[l], tm=512, seq_len=seq, hi=False,
                                        q_scale=ATTN_SCALE * LOG2_E, n_parts=2)
    attn = _attention(q.reshape(nb, seq, D_ATTN), k.reshape(nb, seq, D_ATTN),
                      v.reshape(nb, seq, D_ATTN))
    n_prompt = nb * seq
    n_total = n_prompt + db
    n_rows = -(-n_total // TOKEN_PAD) * TOKEN_PAD
    hx, route = _outproj(attn.reshape(n_prompt, D_ATTN), bc, xp, ga, gc, w_out[l].astype(BF16),
                         gf, w_router_split, tm=512, hi=False, n_rows=n_rows, row0=0, n_parts=2)
    k_prompt = jnp.transpose(kt, (0, 3, 1, 2))
    v_prompt = jnp.transpose(vt, (0, 3, 1, 2))
    conv_prompt = tail.reshape(nb, SUBLANES, D_CONV)[:, SUBLANES - (CONV_WIDTH - 1):]

    cos_s, sin_s = _rope_tables(np.full((db,), PAST_LEN))
    xs = x_sample.reshape(db, D_MODEL)
    st = state_conv[l]
    qs, ks, vs, bcs, us = _inproj(xs, gmix, w_in[l], gq, gk, cos_s, sin_s, w_conv[l],
                                  tm=db, seq_len=db, hi=True, q_scale=ATTN_SCALE,
                                  prev=(st[:, 1], st[:, 0]))
    heads = lambda t: t.reshape(db, N_HEADS, HEAD_DIM)
    attn_s = _attention_sample(heads(qs), heads(ks), heads(vs),
                               jnp.transpose(cache_k[l], (0, 2, 3, 1)),
                               jnp.transpose(cache_v[l], (0, 2, 3, 1)))
    hx, route = _outproj(attn_s.reshape(db, D_ATTN), bcs, xs, ga, gc, w_out[l], gf, w_router,
                         tm=db, hi=True, n_rows=n_rows, row0=n_prompt, into=(hx, route))

    n_cls = N_GROUPS * EXPERTS_PER_GROUP * (EXPERTS_PER_GROUP - 1) // 2
    max_tiles = -(-(n_total + n_cls * (MOE_TILE - 1)) // MOE_TILE) + N_GROUPS * (MOE_STEP_TILES - 1)
    n_slots = -(-max_tiles // MOE_STEP_TILES) * MOE_STEP_TILES * MOE_TILE
    assert n_slots // MOE_TILE <= N_CLASSES and n_rows % PLAN_ROWS == 0
    plan = _route_plan(route, n_total)
    ys = _moe(hx, gf, w_gate[l], w_up[l], w_down[l], plan, n_slots)
    y_prompt = _unsort(ys, plan[-1], n=n_prompt, row0=0, rows=512).reshape(nb, seq, D_MODEL)
    y_sample = _unsort(ys, plan[-1], n=db, row0=n_prompt, rows=db).reshape(db, 1, D_MODEL)
    k_sample = ks.reshape(db, 1, N_HEADS, HEAD_DIM)
    v_sample = vs.reshape(db, 1, N_HEADS, HEAD_DIM)
    conv_sample = jnp.stack([st[:, 1], us], axis=1)

    return (y_prompt, y_sample, k_prompt[None], v_prompt[None], conv_prompt[None],
            k_sample[None], v_sample[None], conv_sample[None])
```
